```python
import math
import jax, jax.numpy as jnp
from jax import lax
import numpy as np

D_MODEL = 1024
BATCH = 16
SEQ = 2048
DEPTH = 4
DEC_BATCH = 32
DEC_SEQ = 16
PAST_LEN = 1024

CHUNK = 64
HEAD_DIM = 64
A_HEADS = 6
A_WIDTH = A_HEADS * HEAD_DIM
A_LEFT_CHUNKS = 8
A_BAND_PAST = A_LEFT_CHUNKS * CHUNK
A_MAX_REL = 128
S5_GROUPS = 16
S5_GROUP_CH = 16
S5_WIDTH = S5_GROUPS * S5_GROUP_CH
S5_STATE = 64
C_HEADS = 6
C_WIDTH = C_HEADS * HEAD_DIM
IDX_HEADS = 8
IDX_DIM = 32
INDEX_SCALE = (IDX_HEADS * IDX_DIM) ** -0.5
TOPK_MAX = 256
Q_BLOCK = 128
T5_BUCKETS = 32
T5_MAX_DIST = 128
MIX_WIDTH = A_WIDTH + S5_WIDTH + C_WIDTH
N_IN = 3 * A_WIDTH + S5_WIDTH + C_WIDTH + 2 * HEAD_DIM + IDX_HEADS * IDX_DIM + IDX_DIM + IDX_HEADS
FFN_DIM = 2816
CONV_W = 3
EPS = 1e-6
NEG_INF = -1e30

kernel_name = "hymba_streaming_chunkattn_s5_dsa_convffn_step"


def rmsnorm(x, g):
    x32 = x.astype(jnp.float32)
    y = x32 * lax.rsqrt(jnp.mean(x32 * x32, axis=-1, keepdims=True) + EPS)
    return (y * g.astype(jnp.float32)).astype(x.dtype)


def t5_bucket(rel):
    half = T5_BUCKETS // 2
    max_exact = half // 2
    n = jnp.abs(rel)
    log_val = jnp.log(jnp.maximum(n, 1).astype(jnp.float32) / max_exact) / math.log(T5_MAX_DIST / max_exact)
    large = jnp.minimum(max_exact + (log_val * (half - max_exact)).astype(jnp.int32), half - 1)
    return jnp.where(rel > 0, half, 0) + jnp.where(n < max_exact, n, large)


def masked_softmax(s, valid):
    return jax.nn.softmax(jnp.where(valid, s, NEG_INF), axis=-1)


def project_in(xn, w_in):
    B, L, _ = xn.shape
    z = xn @ w_in
    sizes = (A_WIDTH, A_WIDTH, A_WIDTH, S5_WIDTH, C_WIDTH, HEAD_DIM, HEAD_DIM, IDX_HEADS * IDX_DIM, IDX_DIM, IDX_HEADS)
    cuts = np.cumsum(sizes)[:-1].tolist()
    aq, ak, av, u, cq, ck, cv, qi, ki, wi = jnp.split(z, cuts, axis=-1)
    return (aq.reshape(B, L, A_HEADS, HEAD_DIM), ak.reshape(B, L, A_HEADS, HEAD_DIM),
            av.reshape(B, L, A_HEADS, HEAD_DIM), u.reshape(B, L, S5_GROUPS, S5_GROUP_CH),
            cq.reshape(B, L, C_HEADS, HEAD_DIM), ck, cv, qi.reshape(B, L, IDX_HEADS, IDX_DIM), ki, wi)


def chunk_attn_prompt(q, k, v, g_q, g_k, rel_table):
    B, L, H, Dh = q.shape
    q = rmsnorm(q, g_q)
    k = rmsnorm(k, g_k)
    nc = L // CHUNK
    band = A_BAND_PAST + CHUNK
    pad = ((0, 0), (A_BAND_PAST, 0), (0, 0), (0, 0))
    kp, vp = jnp.pad(k, pad), jnp.pad(v, pad)
    idx = (jnp.arange(nc) * CHUNK)[:, None] + jnp.arange(band)[None, :]
    kb, vb = kp[:, idx], vp[:, idx]
    qc = q.reshape(B, nc, CHUNK, H, Dh)
    key_pos = idx - A_BAND_PAST
    q_pos = jnp.arange(L).reshape(nc, CHUNK)
    rel = jnp.clip(key_pos[:, None, :] - q_pos[:, :, None], -A_MAX_REL, A_MAX_REL) + A_MAX_REL
    bias = rel_table.astype(jnp.float32)[:, rel].transpose(1, 0, 2, 3)
    s = jnp.einsum("bcqhd,bckhd->bchqk", qc, kb).astype(jnp.float32) * Dh ** -0.5 + bias[None]
    p = masked_softmax(s, (key_pos >= 0)[None, :, None, None, :])
    o = jnp.einsum("bchqk,bckhd->bcqhd", p.astype(vb.dtype), vb).reshape(B, L, H * Dh)
    keep = min(A_BAND_PAST, L)
    return o, k[:, L - keep:], v[:, L - keep:]


def chunk_attn_sample(q, k, v, cache_k, cache_v, g_q, g_k, rel_table, past_len):
    B, T, H, Dh = q.shape
    q = rmsnorm(q, g_q)
    k = rmsnorm(k, g_k)
    P = cache_k.shape[1]
    kk = jnp.concatenate([cache_k.astype(k.dtype), k], axis=1)
    vv = jnp.concatenate([cache_v.astype(v.dtype), v], axis=1)
    key_pos = past_len - P + jnp.arange(P + T)
    q_pos = past_len + jnp.arange(T)
    rel = jnp.clip(key_pos[None, :] - q_pos[:, None], -A_MAX_REL, A_MAX_REL) + A_MAX_REL
    bias = rel_table.astype(jnp.float32)[:, rel]
    s = jnp.einsum("bqhd,bkhd->bhqk", q, kk).astype(jnp.float32) * Dh ** -0.5 + bias[None]
    p = jax.nn.softmax(s, axis=-1)
    o = jnp.einsum("bhqk,bkhd->bqhd", p.astype(vv.dtype), vv).reshape(B, T, H * Dh)
    return o, k, v


def complex_affine_combine(e1, e2):
    a1r, a1i, b1r, b1i = e1
    a2r, a2i, b2r, b2i = e2
    return (a2r * a1r - a2i * a1i, a2r * a1i + a2i * a1r,
            a2r * b1r - a2i * b1i + b2r, a2r * b1i + a2i * b1r + b2i)


def s5_mixer(u, h0_re, h0_im, p):
    B, L, G, C = u.shape
    f32 = jnp.float32
    dt = jnp.exp(p["s5_log_dt"].astype(f32))[:, None]
    lr, li = p["s5_lam_re"].astype(f32), p["s5_lam_im"].astype(f32)
    mag = jnp.exp(lr * dt)
    ab_re, ab_im = mag * jnp.cos(li * dt), mag * jnp.sin(li * dt)
    den = lr * lr + li * li
    cr = ((ab_re - 1.0) * lr + ab_im * li) / den
    ci = (ab_im * lr - (ab_re - 1.0) * li) / den
    br, bi = p["s5_b_re"].astype(f32), p["s5_b_im"].astype(f32)
    bb_re = cr[..., None] * br - ci[..., None] * bi
    bb_im = cr[..., None] * bi + ci[..., None] * br
    u32 = u.astype(f32)
    x_re = jnp.einsum("blgc,gpc->blgp", u32, bb_re)
    x_im = jnp.einsum("blgc,gpc->blgp", u32, bb_im)
    a_re = jnp.broadcast_to(ab_re, x_re.shape)
    a_im = jnp.broadcast_to(ab_im, x_re.shape)
    A_re, A_im, H_re, H_im = lax.associative_scan(complex_affine_combine, (a_re, a_im, x_re, x_im), axis=1)
    h0r, h0i = h0_re.astype(f32)[:, None], h0_im.astype(f32)[:, None]
    h_re = H_re + A_re * h0r - A_im * h0i
    h_im = H_im + A_re * h0i + A_im * h0r
    y = (jnp.einsum("blgp,gcp->blgc", h_re, p["s5_c_re"].astype(f32))
         - jnp.einsum("blgp,gcp->blgc", h_im, p["s5_c_im"].astype(f32)))
    y = y.reshape(B, L, G * C) + p["s5_d"].astype(f32) * u32.reshape(B, L, G * C)
    g = jax.nn.gelu(y)
    out = g * jax.nn.sigmoid(g @ p["s5_w_glu"].astype(f32) + p["s5_b_glu"].astype(f32))
    return out.astype(u.dtype), h_re[:, -1], h_im[:, -1]


def indexer_scores(qi, wi, ki):
    dots = jax.nn.relu(jnp.einsum("bqhd,bkd->bqhk", qi, ki).astype(jnp.float32))
    return jnp.einsum("bqh,bqhk->bqk", wi.astype(jnp.float32) * INDEX_SCALE, dots)


def gathered_attn(q, keys, vals, sel, q_pos, valid, t5_table):
    ks = jax.vmap(lambda a, i: a[i])(keys, sel)
    vs = jax.vmap(lambda a, i: a[i])(vals, sel)
    bias = t5_table.astype(jnp.float32)[t5_bucket(sel - q_pos[None, :, None])]
    s = (jnp.einsum("bqhd,bqkd->bhqk", q, ks).astype(jnp.float32) * q.shape[-1] ** -0.5
         + bias.transpose(0, 3, 1, 2))
    p = masked_softmax(s, valid[:, None])
    return jnp.einsum("bhqk,bqkd->bqhd", p.astype(vs.dtype), vs)


def dsa_prompt(q, k, v, qi, ki, wi, g_q, g_k, t5_table):
    B, L, H, Dh = q.shape
    q = rmsnorm(q, g_q)
    k = rmsnorm(k, g_k)
    topk = min(TOPK_MAX, L // 4)
    nb = L // Q_BLOCK
    key_chunk = jnp.arange(L) // CHUNK

    def block(args):
        qb, qib, wib, start = args
        q_pos = start + jnp.arange(Q_BLOCK)
        adm = key_chunk[None, :] <= (q_pos // CHUNK)[:, None]
        score = jnp.where(adm[None], indexer_scores(qib, wib, ki), NEG_INF)
        _, sel = lax.top_k(score, topk)
        valid = (sel // CHUNK) <= (q_pos // CHUNK)[None, :, None]
        return gathered_attn(qb, k, v, sel, q_pos, valid, t5_table)

    def blocks(t):
        return t.reshape((B, nb, Q_BLOCK) + t.shape[2:]).swapaxes(0, 1)

    o = lax.map(block, (blocks(q), blocks(qi), blocks(wi), jnp.arange(nb) * Q_BLOCK))
    o = o.swapaxes(0, 1).reshape(B, L, H * Dh)
    return o, k, v


def dsa_sample(q, k, v, qi, ki, wi, cache_k, cache_v, cache_ki, g_q, g_k, t5_table):
    B, T, H, Dh = q.shape
    q = rmsnorm(q, g_q)
    k = rmsnorm(k, g_k)
    P = cache_k.shape[1]
    kk = jnp.concatenate([cache_k.astype(k.dtype), k], axis=1)
    vv = jnp.concatenate([cache_v.astype(v.dtype), v], axis=1)
    kki = jnp.concatenate([cache_ki.astype(ki.dtype), ki], axis=1)
    Lk = P + T
    topk = min(TOPK_MAX, Lk // 4)
    q_pos = P + jnp.arange(T)
    adm = (jnp.arange(Lk) // CHUNK)[None, :] <= (q_pos // CHUNK)[:, None]
    score = jnp.where(adm[None], indexer_scores(qi, wi, kki), NEG_INF)
    _, sel = lax.top_k(score, topk)
    valid = (sel // CHUNK) <= (q_pos // CHUNK)[None, :, None]
    o = gathered_attn(q, kk, vv, sel, q_pos, valid, t5_table).reshape(B, T, H * Dh)
    return o, k, v, ki


def conv_ffn(xn, prev, p):
    L = xn.shape[1]
    g = xn @ p["ffn_w_gate"]
    gp = jnp.concatenate([prev.astype(g.dtype), g], axis=1)
    w = p["ffn_conv_w"]
    gc = sum(w[j] * gp[:, j:j + L] for j in range(CONV_W)) + p["ffn_conv_b"]
    h = jax.nn.silu(gc) * (xn @ p["ffn_w_up"])
    return h @ p["ffn_w_down"], gp[:, L:]


def prompt_layer(x, p, t5_table):
    B, L, _ = x.shape
    xn = rmsnorm(x, p["norm1"])
    aq, ak, av, u, cq, ck, cv, qi, ki, wi = project_in(xn, p["w_in"])
    oa, a_k, a_v = chunk_attn_prompt(aq, ak, av, p["a_gq"], p["a_gk"], p["a_rel"])
    zero_h = jnp.zeros((B, S5_GROUPS, S5_STATE), jnp.float32)
    ob, h_re, h_im = s5_mixer(u, zero_h, zero_h, p)
    oc, c_k, c_v = dsa_prompt(cq, ck, cv, qi, ki, wi, p["c_gq"], p["c_gk"], t5_table)
    x = x + jnp.concatenate([oa, ob, oc], axis=-1) @ p["w_out"]
    f, conv_buf = conv_ffn(rmsnorm(x, p["norm2"]), jnp.zeros((B, CONV_W - 1, FFN_DIM), x.dtype), p)
    return x + f, (a_k, a_v, h_re, h_im, c_k, c_v, ki, conv_buf)


def sample_layer(x, p, t5_table, ca_k, ca_v, h_re0, h_im0, cc_k, cc_v, cc_ki, conv_prev):
    xn = rmsnorm(x, p["norm1"])
    aq, ak, av, u, cq, ck, cv, qi, ki, wi = project_in(xn, p["w_in"])
    oa, a_k, a_v = chunk_attn_sample(aq, ak, av, ca_k, ca_v, p["a_gq"], p["a_gk"], p["a_rel"], cc_k.shape[1])
    ob, h_re, h_im = s5_mixer(u, h_re0, h_im0, p)
    oc, c_k, c_v, c_ki = dsa_sample(cq, ck, cv, qi, ki, wi, cc_k, cc_v, cc_ki, p["c_gq"], p["c_gk"], t5_table)
    x = x + jnp.concatenate([oa, ob, oc], axis=-1) @ p["w_out"]
    f, conv_buf = conv_ffn(rmsnorm(x, p["norm2"]), conv_prev, p)
    return x + f, (a_k, a_v, h_re, h_im, c_k, c_v, c_ki, conv_buf)


def setup_inputs(seed: int = 0) -> dict:
    keys = iter(jax.random.split(jax.random.key(seed), 48))

    def nrm(shape, scale):
        return scale * jax.random.normal(next(keys), shape, jnp.float32)

    a_cache = min(A_BAND_PAST, PAST_LEN)
    lam_im0 = jnp.pi * jnp.arange(S5_STATE, dtype=jnp.float32)
    return {
        "x_prompt": nrm((BATCH, SEQ, D_MODEL), 1.0),
        "x_sample": nrm((DEC_BATCH, DEC_SEQ, D_MODEL), 1.0),
        "cache_a_k": nrm((DEPTH, DEC_BATCH, a_cache, A_HEADS, HEAD_DIM), 1.0),
        "cache_a_v": nrm((DEPTH, DEC_BATCH, a_cache, A_HEADS, HEAD_DIM), 1.0),
        "state_s5_re": nrm((DEPTH, DEC_BATCH, S5_GROUPS, S5_STATE), 0.5),
        "state_s5_im": nrm((DEPTH, DEC_BATCH, S5_GROUPS, S5_STATE), 0.5),
        "cache_c_k": nrm((DEPTH, DEC_BATCH, PAST_LEN, HEAD_DIM), 1.0),
        "cache_c_v": nrm((DEPTH, DEC_BATCH, PAST_LEN, HEAD_DIM), 1.0),
        "cache_c_idx_k": nrm((DEPTH, DEC_BATCH, PAST_LEN, IDX_DIM), 1.0),
        "state_ffn_conv": nrm((DEPTH, DEC_BATCH, CONV_W - 1, FFN_DIM), 1.0),
        "t5_bias": nrm((T5_BUCKETS, C_HEADS), 0.2),
        "norm1_g": 1.0 + nrm((DEPTH, D_MODEL), 0.02),
        "w_in": nrm((DEPTH, D_MODEL, N_IN), D_MODEL ** -0.5),
        "a_q_gain": 1.0 + nrm((DEPTH, HEAD_DIM), 0.02),
        "a_k_gain": 1.0 + nrm((DEPTH, HEAD_DIM), 0.02),
        "a_rel_bias": nrm((DEPTH, A_HEADS, 2 * A_MAX_REL + 1), 0.2),
        "s5_lam_re": -0.5 + nrm((DEPTH, S5_GROUPS, S5_STATE), 0.01),
        "s5_lam_im": lam_im0 + nrm((DEPTH, S5_GROUPS, S5_STATE), 0.01),
        "s5_log_dt": jax.random.uniform(next(keys), (DEPTH, S5_GROUPS), jnp.float32,
                                        minval=math.log(1e-3), maxval=math.log(1e-1)),
        "s5_b_re": nrm((DEPTH, S5_GROUPS, S5_STATE, S5_GROUP_CH), (2 * S5_GROUP_CH) ** -0.5),
        "s5_b_im": nrm((DEPTH, S5_GROUPS, S5_STATE, S5_GROUP_CH), (2 * S5_GROUP_CH) ** -0.5),
        "s5_c_re": nrm((DEPTH, S5_GROUPS, S5_GROUP_CH, S5_STATE), (2 * S5_STATE) ** -0.5),
        "s5_c_im": nrm((DEPTH, S5_GROUPS, S5_GROUP_CH, S5_STATE), (2 * S5_STATE) ** -0.5),
        "s5_d": nrm((DEPTH, S5_WIDTH), 1.0),
        "s5_w_glu": nrm((DEPTH, S5_WIDTH, S5_WIDTH), S5_WIDTH ** -0.5),
        "s5_b_glu": nrm((DEPTH, S5_WIDTH), 0.02),
        "c_q_gain": 1.0 + nrm((DEPTH, HEAD_DIM), 0.02),
        "c_k_gain": 1.0 + nrm((DEPTH, HEAD_DIM), 0.02),
        "w_out": nrm((DEPTH, MIX_WIDTH, D_MODEL), MIX_WIDTH ** -0.5),
        "norm2_g": 1.0 + nrm((DEPTH, D_MODEL), 0.02),
        "ffn_w_gate": nrm((DEPTH, D_MODEL, FFN_DIM), D_MODEL ** -0.5),
        "ffn_w_up": nrm((DEPTH, D_MODEL, FFN_DIM), D_MODEL ** -0.5),
        "ffn_conv_w": nrm((DEPTH, CONV_W, FFN_DIM), CONV_W ** -0.5),
        "ffn_conv_b": nrm((DEPTH, FFN_DIM), 0.02),
        "ffn_w_down": nrm((DEPTH, FFN_DIM, D_MODEL), FFN_DIM ** -0.5),
    }


def reference(x_prompt, x_sample, cache_a_k, cache_a_v, state_s5_re, state_s5_im, cache_c_k, cache_c_v,
              cache_c_idx_k, state_ffn_conv, t5_bias, norm1_g, w_in, a_q_gain, a_k_gain, a_rel_bias,
              s5_lam_re, s5_lam_im, s5_log_dt, s5_b_re, s5_b_im, s5_c_re, s5_c_im, s5_d, s5_w_glu, s5_b_glu,
              c_q_gain, c_k_gain, w_out, norm2_g, ffn_w_gate, ffn_w_up, ffn_conv_w, ffn_conv_b, ffn_w_down):
    yp, ys = x_prompt, x_sample
    prompt_states, sample_states = [], []
    for l in range(DEPTH):
        p = {
            "norm1": norm1_g[l], "w_in": w_in[l], "a_gq": a_q_gain[l], "a_gk": a_k_gain[l],
            "a_rel": a_rel_bias[l], "s5_lam_re": s5_lam_re[l], "s5_lam_im": s5_lam_im[l],
            "s5_log_dt": s5_log_dt[l], "s5_b_re": s5_b_re[l], "s5_b_im": s5_b_im[l],
            "s5_c_re": s5_c_re[l], "s5_c_im": s5_c_im[l], "s5_d": s5_d[l], "s5_w_glu": s5_w_glu[l],
            "s5_b_glu": s5_b_glu[l], "c_gq": c_q_gain[l], "c_gk": c_k_gain[l], "w_out": w_out[l],
            "norm2": norm2_g[l], "ffn_w_gate": ffn_w_gate[l], "ffn_w_up": ffn_w_up[l],
            "ffn_conv_w": ffn_conv_w[l], "ffn_conv_b": ffn_conv_b[l], "ffn_w_down": ffn_w_down[l],
        }
        yp, st_p = prompt_layer(yp, p, t5_bias)
        ys, st_s = sample_layer(ys, p, t5_bias, cache_a_k[l], cache_a_v[l], state_s5_re[l], state_s5_im[l],
                                cache_c_k[l], cache_c_v[l], cache_c_idx_k[l], state_ffn_conv[l])
        prompt_states.append(st_p)
        sample_states.append(st_s)
    (a_k_p, a_v_p, s5_re_p, s5_im_p, c_k_p, c_v_p, c_ki_p, conv_p) = [jnp.stack(z) for z in zip(*prompt_states)]
    (a_k_s, a_v_s, s5_re_s, s5_im_s, c_k_s, c_v_s, c_ki_s, conv_s) = [jnp.stack(z) for z in zip(*sample_states)]
    return (yp, ys, a_k_p, a_v_p, a_k_s, a_v_s, s5_re_p, s5_im_p, s5_re_s, s5_im_s,
            c_k_p, c_v_p, c_ki_p, c_k_s, c_v_s, c_ki_s, conv_p, conv_s)
```

```python
import functools
import math

import jax
import jax.numpy as jnp
from jax import lax
from jax.experimental import pallas as pl
from jax.experimental.pallas import tpu as pltpu

F32 = jnp.float32
MXU_DTYPE = jnp.bfloat16

D_MODEL = 1024
CHUNK = 64
HEAD_DIM = 64
A_HEADS = 6
A_WIDTH = A_HEADS * HEAD_DIM
A_BAND_PAST = 8 * CHUNK
A_MAX_REL = 128
S5_GROUPS = 16
S5_GROUP_CH = 16
S5_WIDTH = S5_GROUPS * S5_GROUP_CH
S5_STATE = 64
S5_FLAT = S5_GROUPS * S5_STATE
C_HEADS = 6
C_WIDTH = C_HEADS * HEAD_DIM
IDX_HEADS = 8
IDX_DIM = 32
INDEX_SCALE = (IDX_HEADS * IDX_DIM) ** -0.5
TOPK_MAX = 256
Q_BLOCK = 128
T5_BUCKETS = 32
T5_MAX_DIST = 128
FFN_DIM = 2816
CONV_W = 3
EPS = 1e-6
NEG_INF = -1e30
INT_MIN = -(2 ** 31)

COL_AQ, COL_AK, COL_AV, COL_CQ = 0, 384, 768, 1152
COL_U, COL_QI, COL_CKV, COL_KIW = 1536, 1792, 2048, 2176
Z_WIDTH = 2304
VMEM_LIMIT = 56 * 1024 * 1024


def _nt_dot(a, b):
    return lax.dot_general(a, b, (((1,), (1,)), ((), ())), preferred_element_type=F32)


def _dot(a, b):
    return jnp.dot(a, b, preferred_element_type=F32)


def _rms(x, g):
    return x * lax.rsqrt(jnp.mean(x * x, axis=-1, keepdims=True) + EPS) * g


def _in_proj_kernel(x_ref, g_ref, w_ref, z_ref):
    xn = _rms(x_ref[...], g_ref[...]).astype(MXU_DTYPE)
    z_ref[...] = _dot(xn, w_ref[...])


def _in_proj(x2d, g, w_r, tm):
    m = x2d.shape[0]
    return pl.pallas_call(
        _in_proj_kernel,
        grid=(m // tm,),
        in_specs=[
            pl.BlockSpec((tm, D_MODEL), lambda i: (i, 0)),
            pl.BlockSpec((1, D_MODEL), lambda i: (0, 0)),
            pl.BlockSpec((D_MODEL, Z_WIDTH), lambda i: (0, 0)),
        ],
        out_specs=pl.BlockSpec((tm, Z_WIDTH), lambda i: (i, 0)),
        out_shape=jax.ShapeDtypeStruct((m, Z_WIDTH), F32),
        compiler_params=pltpu.CompilerParams(dimension_semantics=("parallel",), vmem_limit_bytes=VMEM_LIMIT),
        name="in_proj",
    )(x2d, g, w_r)


A_QB = 2 * CHUNK
A_BAND = A_BAND_PAST + A_QB


def _attn_a_kernel(q_ref, k_ref, v_ref, bias_ref, gq_ref, gk_ref, o_ref, ak_ref, av_ref, kn_s, v_s, *, seq, keep):
    j = pl.program_id(1)

    @pl.when(j == 0)
    def _prep():
        kn_s[0:A_BAND_PAST, :] = jnp.zeros((A_BAND_PAST, A_WIDTH), MXU_DTYPE)
        v_s[0:A_BAND_PAST, :] = jnp.zeros((A_BAND_PAST, A_WIDTH), MXU_DTYPE)
        for h in range(A_HEADS):
            sl = slice(h * HEAD_DIM, (h + 1) * HEAD_DIM)
            khn = _rms(k_ref[0, :, sl], gk_ref[...])
            kn_s[A_BAND_PAST:A_BAND_PAST + seq, sl] = khn.astype(MXU_DTYPE)
            ak_ref[0, :, sl] = khn[seq - keep:, :]
        v = v_ref[0]
        v_s[A_BAND_PAST:A_BAND_PAST + seq, :] = v.astype(MXU_DTYPE)
        av_ref[0] = v[seq - keep:, :]

    start = pl.multiple_of(j * A_QB, A_QB)
    kpos = j * A_QB - A_BAND_PAST + lax.broadcasted_iota(jnp.int32, (1, A_BAND), 1)
    live = kpos >= 0
    for h in range(A_HEADS):
        sl = slice(h * HEAD_DIM, (h + 1) * HEAD_DIM)
        qh = (_rms(q_ref[0, :, sl], gq_ref[...]) * HEAD_DIM ** -0.5).astype(MXU_DTYPE)
        kh = kn_s[pl.ds(start, A_BAND), sl]
        s = _nt_dot(qh, kh) + bias_ref[h]
        s = jnp.where(live, s, NEG_INF)
        p = jnp.exp(s - jnp.max(s, axis=-1, keepdims=True))
        denom = jnp.sum(p, axis=-1, keepdims=True)
        oh = _dot(p.astype(MXU_DTYPE), v_s[pl.ds(start, A_BAND), sl])
        o_ref[0, :, sl] = oh / denom


def _attn_a_bias(rel_table):
    r = jnp.arange(A_QB)[:, None]
    c = jnp.arange(A_BAND)[None, :]
    rel = jnp.clip(c - A_BAND_PAST - r, -A_MAX_REL, A_MAX_REL) + A_MAX_REL
    lo = (r // CHUNK) * CHUNK
    in_band = (c >= lo) & (c < lo + A_BAND_PAST + CHUNK)
    return jnp.where(in_band[None], rel_table.astype(F32)[:, rel], NEG_INF)


def _attn_a(z3, bias, gq, gk):
    b, seq, _ = z3.shape
    keep = min(A_BAND_PAST, seq)
    kern = functools.partial(_attn_a_kernel, seq=seq, keep=keep)
    return pl.pallas_call(
        kern,
        grid=(b, seq // A_QB),
        in_specs=[
            pl.BlockSpec((1, A_QB, A_WIDTH), lambda i, j: (i, j, COL_AQ // A_WIDTH)),
            pl.BlockSpec((1, seq, A_WIDTH), lambda i, j: (i, 0, COL_AK // A_WIDTH)),
            pl.BlockSpec((1, seq, A_WIDTH), lambda i, j: (i, 0, COL_AV // A_WIDTH)),
            pl.BlockSpec((A_HEADS, A_QB, A_BAND), lambda i, j: (0, 0, 0)),
            pl.BlockSpec((1, HEAD_DIM), lambda i, j: (0, 0)),
            pl.BlockSpec((1, HEAD_DIM), lambda i, j: (0, 0)),
        ],
        out_specs=[
            pl.BlockSpec((1, A_QB, A_WIDTH), lambda i, j: (i, j, 0)),
            pl.BlockSpec((1, keep, A_WIDTH), lambda i, j: (i, 0, 0)),
            pl.BlockSpec((1, keep, A_WIDTH), lambda i, j: (i, 0, 0)),
        ],
        out_shape=[
            jax.ShapeDtypeStruct((b, seq, A_WIDTH), F32),
            jax.ShapeDtypeStruct((b, keep, A_WIDTH), F32),
            jax.ShapeDtypeStruct((b, keep, A_WIDTH), F32),
        ],
        scratch_shapes=[
            pltpu.VMEM((A_BAND_PAST + seq, A_WIDTH), MXU_DTYPE),
            pltpu.VMEM((A_BAND_PAST + seq, A_WIDTH), MXU_DTYPE),
        ],
        compiler_params=pltpu.CompilerParams(
            dimension_semantics=("parallel", "arbitrary"), vmem_limit_bytes=VMEM_LIMIT),
        name="attn_a",
    )(z3, z3, z3, bias, gq, gk)


def _s5_kernel(u_ref, h0r_ref, h0i_ref, ar_ref, ai_ref, wbr_ref, wbi_ref, wcr_ref, wci_ref, d_ref, wg_ref, bg_ref,
               o_ref, hr_out, hi_out, xr_s, xi_s, hr_c, hi_c, *, bt, tc, exact_in):
    c = pl.program_id(1)

    @pl.when(c == 0)
    def _init():
        hr_c[...] = h0r_ref[...]
        hi_c[...] = h0i_ref[...]

    u = u_ref[...].reshape(bt * tc, S5_WIDTH)
    if exact_in:
        xr = jnp.dot(u, wbr_ref[...], preferred_element_type=F32, precision=lax.Precision.HIGHEST)
        xi = jnp.dot(u, wbi_ref[...], preferred_element_type=F32, precision=lax.Precision.HIGHEST)
    else:
        ub = u.astype(MXU_DTYPE)
        xr = _dot(ub, wbr_ref[...].astype(MXU_DTYPE))
        xi = _dot(ub, wbi_ref[...].astype(MXU_DTYPE))
    planes = S5_FLAT // 128
    lanes = [slice(k * 128, (k + 1) * 128) for k in range(planes)]
    for k in range(planes):
        xr_s[k] = xr[:, lanes[k]]
        xi_s[k] = xi[:, lanes[k]]
    ar = [ar_ref[:, lanes[k]] for k in range(planes)]
    ai = [ai_ref[:, lanes[k]] for k in range(planes)]

    def step(t, carry):
        rows = pl.ds(t, bt, stride=tc)
        new = []
        for k in range(planes):
            hr, hi = carry[2 * k], carry[2 * k + 1]
            nhr = ar[k] * hr - ai[k] * hi + xr_s[k, rows, :]
            nhi = ar[k] * hi + ai[k] * hr + xi_s[k, rows, :]
            xr_s[k, rows, :] = nhr
            xi_s[k, rows, :] = nhi
            new += [nhr, nhi]
        return tuple(new)

    init = []
    for k in range(planes):
        init += [hr_c[:, lanes[k]], hi_c[:, lanes[k]]]
    last = lax.fori_loop(0, tc, step, tuple(init))
    for k in range(planes):
        hr_c[:, lanes[k]] = last[2 * k]
        hi_c[:, lanes[k]] = last[2 * k + 1]
    hr_out[...] = hr_c[...]
    hi_out[...] = hi_c[...]
    h_re = jnp.concatenate([xr_s[k] for k in range(planes)], axis=1).astype(MXU_DTYPE)
    h_im = jnp.concatenate([xi_s[k] for k in range(planes)], axis=1).astype(MXU_DTYPE)
    y = _dot(h_re, wcr_ref[...]) - _dot(h_im, wci_ref[...]) + d_ref[...] * u
    g = jax.nn.gelu(y)
    out = g * jax.nn.sigmoid(_dot(g.astype(MXU_DTYPE), wg_ref[...]) + bg_ref[...])
    o_ref[...] = out.reshape(bt, tc, S5_WIDTH)


def _s5_params(lam_re, lam_im, log_dt, b_re, b_im, c_re, c_im):
    dt = jnp.exp(log_dt.astype(F32))[:, None]
    lr, li = lam_re.astype(F32), lam_im.astype(F32)
    mag = jnp.exp(lr * dt)
    ab_re, ab_im = mag * jnp.cos(li * dt), mag * jnp.sin(li * dt)
    den = lr * lr + li * li
    cr = ((ab_re - 1.0) * lr + ab_im * li) / den
    ci = (ab_im * lr - (ab_re - 1.0) * li) / den
    br, bi = b_re.astype(F32), b_im.astype(F32)
    bb_re = cr[..., None] * br - ci[..., None] * bi
    bb_im = cr[..., None] * bi + ci[..., None] * br
    eye = jnp.eye(S5_GROUPS, dtype=F32)
    wb_re = jnp.einsum("gpc,gh->gchp", bb_re, eye).reshape(S5_WIDTH, S5_FLAT)
    wb_im = jnp.einsum("gpc,gh->gchp", bb_im, eye).reshape(S5_WIDTH, S5_FLAT)
    wc_re = jnp.einsum("gcp,gh->gphc", c_re.astype(F32), eye).reshape(S5_FLAT, S5_WIDTH).astype(MXU_DTYPE)
    wc_im = jnp.einsum("gcp,gh->gphc", c_im.astype(F32), eye).reshape(S5_FLAT, S5_WIDTH).astype(MXU_DTYPE)
    return ab_re.reshape(1, S5_FLAT), ab_im.reshape(1, S5_FLAT), wb_re, wb_im, wc_re, wc_im


def _s5(z3, h0r, h0i, sp, d, wg, bg, *, bt, tc, exact_in):
    b, seq, _ = z3.shape
    ar, ai, wbr, wbi, wcr, wci = sp
    kern = functools.partial(_s5_kernel, bt=bt, tc=tc, exact_in=exact_in)
    const = lambda shape: pl.BlockSpec(shape, lambda i, c: (0,) * len(shape))
    return pl.pallas_call(
        kern,
        grid=(b // bt, seq // tc),
        in_specs=[
            pl.BlockSpec((bt, tc, S5_WIDTH), lambda i, c: (i, c, COL_U // S5_WIDTH)),
            pl.BlockSpec((bt, S5_FLAT), lambda i, c: (i, 0)),
            pl.BlockSpec((bt, S5_FLAT), lambda i, c: (i, 0)),
            const((1, S5_FLAT)), const((1, S5_FLAT)),
            const((S5_WIDTH, S5_FLAT)), const((S5_WIDTH, S5_FLAT)),
            const((S5_FLAT, S5_WIDTH)), const((S5_FLAT, S5_WIDTH)),
            const((1, S5_WIDTH)), const((S5_WIDTH, S5_WIDTH)), const((1, S5_WIDTH)),
        ],
        out_specs=[
            pl.BlockSpec((bt, tc, S5_WIDTH), lambda i, c: (i, c, 0)),
            pl.BlockSpec((bt, S5_FLAT), lambda i, c: (i, 0)),
            pl.BlockSpec((bt, S5_FLAT), lambda i, c: (i, 0)),
        ],
        out_shape=[
            jax.ShapeDtypeStruct((b, seq, S5_WIDTH), F32),
            jax.ShapeDtypeStruct((b, S5_FLAT), F32),
            jax.ShapeDtypeStruct((b, S5_FLAT), F32),
        ],
        scratch_shapes=[
            pltpu.VMEM((S5_FLAT // 128, bt * tc, 128), F32), pltpu.VMEM((S5_FLAT // 128, bt * tc, 128), F32),
            pltpu.VMEM((bt, S5_FLAT), F32), pltpu.VMEM((bt, S5_FLAT), F32),
        ],
        compiler_params=pltpu.CompilerParams(
            dimension_semantics=("parallel", "arbitrary"), vmem_limit_bytes=VMEM_LIMIT),
        name="s5",
    )(z3, h0r, h0i, ar, ai, wbr, wbi, wcr, wci, d, wg, bg)


def _sortable_key(score):
    bits = lax.bitcast_convert_type(score + 0.0, jnp.int32)
    return bits ^ ((bits >> 31) & jnp.int32(0x7FFFFFFF))


def _kth_largest(count_ge, k):
    def body(it, res):
        cand = res | lax.shift_left(jnp.int32(1), (31 - it).astype(jnp.int32))
        cnt = count_ge(cand ^ jnp.int32(INT_MIN))
        return jnp.where(cnt >= k, cand, res)

    res = lax.fori_loop(0, 32, body, jnp.zeros((1, Q_BLOCK), jnp.int32))
    return res ^ jnp.int32(INT_MIN)


def _dsa_kernel(cq_ref, ckv_ref, qi_ref, kiwq_ref, kiwk_ref, gq_ref, gk_ref, bias_ref,
                o_ref, ck_out, cv_out, cki_out, kn_s, vt_s, ki_s, keys_s, m_s, l_s, acc_s, *, seq, topk):
    j = pl.program_id(1)
    nkb = j + 1

    @pl.when(j == 0)
    def _prep():
        kn = _rms(ckv_ref[0, :, 0:HEAD_DIM], gk_ref[...])
        cv = ckv_ref[0, :, HEAD_DIM:2 * HEAD_DIM]
        ck_out[0] = kn
        cv_out[0] = cv
        kn_s[...] = kn.astype(MXU_DTYPE)
        eye = (lax.broadcasted_iota(jnp.int32, (HEAD_DIM, HEAD_DIM), 0)
               == lax.broadcasted_iota(jnp.int32, (HEAD_DIM, HEAD_DIM), 1)).astype(MXU_DTYPE)
        vt_s[...] = _nt_dot(eye, cv.astype(MXU_DTYPE)).astype(MXU_DTYPE)
        ki = kiwk_ref[0, :, 0:IDX_DIM]
        cki_out[0] = ki
        ki_s[...] = ki.astype(MXU_DTYPE)

    cq = cq_ref[0]
    q_all = jnp.concatenate(
        [(_rms(cq[:, h * HEAD_DIM:(h + 1) * HEAD_DIM], gq_ref[...]) * HEAD_DIM ** -0.5).astype(MXU_DTYPE)
         for h in range(C_HEADS)], axis=0)
    qi = qi_ref[0].astype(MXU_DTYPE)
    w_t = kiwq_ref[0].T
    w_rows = [w_t[IDX_DIM + h:IDX_DIM + h + 1, :] * INDEX_SCALE for h in range(IDX_HEADS)]
    q_chunk = (j * Q_BLOCK + lax.broadcasted_iota(jnp.int32, (1, Q_BLOCK), 1)) // CHUNK
    s_iota = lax.broadcasted_iota(jnp.int32, (Q_BLOCK, 1), 0)

    def idx_body(kb, _):
        off = pl.multiple_of(kb * Q_BLOCK, Q_BLOCK)
        kic = ki_s[pl.ds(off, Q_BLOCK), :]
        score = jnp.zeros((Q_BLOCK, Q_BLOCK), F32)
        for h in range(IDX_HEADS):
            dots = _nt_dot(kic, qi[:, h * IDX_DIM:(h + 1) * IDX_DIM])
            score = score + w_rows[h] * jnp.maximum(dots, 0.0)
        adm = ((off + s_iota) // CHUNK) <= q_chunk
        keys_s[pl.ds(off, Q_BLOCK), :] = jnp.where(adm, _sortable_key(score), jnp.int32(INT_MIN))
        return 0

    lax.fori_loop(0, nkb, idx_body, 0)

    def count(pred):
        def body(kb, acc):
            off = pl.multiple_of(kb * Q_BLOCK, Q_BLOCK)
            hit = jnp.where(pred(keys_s[pl.ds(off, Q_BLOCK), :], off + s_iota), 1, 0)
            return acc + jnp.sum(hit.reshape(Q_BLOCK // 8, 8, Q_BLOCK), axis=0)

        acc = lax.fori_loop(0, nkb, body, jnp.zeros((8, Q_BLOCK), jnp.int32))
        return jnp.sum(acc, axis=0, keepdims=True)

    thr = _kth_largest(lambda c: count(lambda k, pos: k >= c), topk)
    n_ge = count(lambda k, pos: k >= thr)
    real_thr = thr != jnp.int32(INT_MIN)
    pos_bits = (seq - 1).bit_length()

    def _tie_search():
        need = topk - count(lambda k, pos: k > thr)

        def body(it, v):
            cand = v | lax.shift_left(jnp.int32(1), (pos_bits - 1 - it).astype(jnp.int32))
            return jnp.where(count(lambda k, pos: (k == thr) & (pos < cand)) < need, cand, v)

        return lax.fori_loop(0, pos_bits, body, jnp.zeros((1, Q_BLOCK), jnp.int32))

    has_ties = jnp.max(jnp.where(real_thr & (n_ge > topk), 1, 0)) > 0
    tie_last = lax.cond(has_ties, _tie_search, lambda: jnp.full((1, Q_BLOCK), 2 ** pos_bits - 1, jnp.int32))
    tie_last = jnp.where(real_thr, tie_last, -1)

    m_s[...] = jnp.full(m_s.shape, NEG_INF, F32)
    l_s[...] = jnp.zeros(l_s.shape, F32)
    acc_s[...] = jnp.zeros(acc_s.shape, F32)

    def att_body(kb, _):
        off = pl.multiple_of(kb * Q_BLOCK, Q_BLOCK)
        s_all = _nt_dot(kn_s[pl.ds(off, Q_BLOCK), :], q_all)
        keyc = keys_s[pl.ds(off, Q_BLOCK), :]
        sel = (keyc > thr) | ((keyc == thr) & (off + s_iota <= tie_last))
        bidx = jnp.minimum(j - kb, 2)
        vtc = vt_s[:, pl.ds(off, Q_BLOCK)]
        for h in range(C_HEADS):
            ls = slice(h * Q_BLOCK, (h + 1) * Q_BLOCK)
            s = jnp.where(sel, s_all[:, ls] + bias_ref[bidx, :, ls], NEG_INF)
            m_old = m_s[h:h + 1, :]
            m_new = jnp.maximum(m_old, jnp.max(s, axis=0, keepdims=True))
            alpha = jnp.exp(m_old - m_new)
            p = jnp.where(sel, jnp.exp(s - m_new), 0.0)
            l_s[h:h + 1, :] = alpha * l_s[h:h + 1, :] + jnp.sum(p, axis=0, keepdims=True)
            rs = slice(h * HEAD_DIM, (h + 1) * HEAD_DIM)
            acc_s[rs, :] = alpha * acc_s[rs, :] + _dot(vtc, p.astype(MXU_DTYPE))
            m_s[h:h + 1, :] = m_new
        return 0

    lax.fori_loop(0, nkb, att_body, 0)
    out_t = jnp.concatenate(
        [acc_s[h * HEAD_DIM:(h + 1) * HEAD_DIM, :] / l_s[h:h + 1, :] for h in range(C_HEADS)], axis=0)
    o_ref[0] = out_t.T


def _t5_bucket(rel):
    half = T5_BUCKETS // 2
    max_exact = half // 2
    n = jnp.abs(rel)
    log_val = jnp.log(jnp.maximum(n, 1).astype(F32) / max_exact) / math.log(T5_MAX_DIST / max_exact)
    large = jnp.minimum(max_exact + (log_val * (half - max_exact)).astype(jnp.int32), half - 1)
    return jnp.where(rel > 0, half, 0) + jnp.where(n < max_exact, n, large)


def _dsa_bias_tiles(t5_table):
    s = jnp.arange(Q_BLOCK)[:, None]
    t = jnp.arange(Q_BLOCK)[None, :]
    tiles = []
    for d in range(3):
        rel = s - d * Q_BLOCK - t
        tile = t5_table.astype(F32)[_t5_bucket(rel)]
        tiles.append(tile.transpose(0, 2, 1).reshape(Q_BLOCK, C_HEADS * Q_BLOCK))
    return jnp.stack(tiles)


def _dsa(z3, bias, gq, gk):
    b, seq, _ = z3.shape
    topk = min(TOPK_MAX, seq // 4)
    kern = functools.partial(_dsa_kernel, seq=seq, topk=topk)
    return pl.pallas_call(
        kern,
        grid=(b, seq // Q_BLOCK),
        in_specs=[
            pl.BlockSpec((1, Q_BLOCK, C_WIDTH), lambda i, j: (i, j, COL_CQ // C_WIDTH)),
            pl.BlockSpec((1, seq, 128), lambda i, j: (i, 0, COL_CKV // 128)),
            pl.BlockSpec((1, Q_BLOCK, 256), lambda i, j: (i, j, COL_QI // 256)),
            pl.BlockSpec((1, Q_BLOCK, 128), lambda i, j: (i, j, COL_KIW // 128)),
            pl.BlockSpec((1, seq, 128), lambda i, j: (i, 0, COL_KIW // 128)),
            pl.BlockSpec((1, HEAD_DIM), lambda i, j: (0, 0)),
            pl.BlockSpec((1, HEAD_DIM), lambda i, j: (0, 0)),
            pl.BlockSpec((3, Q_BLOCK, C_HEADS * Q_BLOCK), lambda i, j: (0, 0, 0)),
        ],
        out_specs=[
            pl.BlockSpec((1, Q_BLOCK, C_WIDTH), lambda i, j: (i, j, 0)),
            pl.BlockSpec((1, seq, HEAD_DIM), lambda i, j: (i, 0, 0)),
            pl.BlockSpec((1, seq, HEAD_DIM), lambda i, j: (i, 0, 0)),
            pl.BlockSpec((1, seq, IDX_DIM), lambda i, j: (i, 0, 0)),
        ],
        out_shape=[
            jax.ShapeDtypeStruct((b, seq, C_WIDTH), F32),
            jax.ShapeDtypeStruct((b, seq, HEAD_DIM), F32),
            jax.ShapeDtypeStruct((b, seq, HEAD_DIM), F32),
            jax.ShapeDtypeStruct((b, seq, IDX_DIM), F32),
        ],
        scratch_shapes=[
            pltpu.VMEM((seq, HEAD_DIM), MXU_DTYPE),
            pltpu.VMEM((HEAD_DIM, seq), MXU_DTYPE),
            pltpu.VMEM((seq, IDX_DIM), MXU_DTYPE),
            pltpu.VMEM((seq, Q_BLOCK), jnp.int32),
            pltpu.VMEM((8, Q_BLOCK), F32),
            pltpu.VMEM((8, Q_BLOCK), F32),
            pltpu.VMEM((C_WIDTH, Q_BLOCK), F32),
        ],
        compiler_params=pltpu.CompilerParams(
            dimension_semantics=("parallel", "arbitrary"), vmem_limit_bytes=VMEM_LIMIT),
        name="dsa",
    )(z3, z3, z3, z3, z3, gq, gk, bias)


def _mix_residual(x_ref, oa_ref, ob_ref, oc_ref, wo_ref):
    mix = (_dot(oa_ref[...].astype(MXU_DTYPE), wo_ref[0:A_WIDTH, :])
           + _dot(ob_ref[...].astype(MXU_DTYPE), wo_ref[A_WIDTH:A_WIDTH + S5_WIDTH, :])
           + _dot(oc_ref[...].astype(MXU_DTYPE), wo_ref[A_WIDTH + S5_WIDTH:, :]))
    return x_ref[...] + mix


def _ffn_chunk(xn, g, g1, g2, sl, wu_ref, cw_ref, cb_ref, wd_ref):
    up = _dot(xn, wu_ref[:, sl])
    gc = cw_ref[0:1, sl] * g2 + cw_ref[1:2, sl] * g1 + cw_ref[2:3, sl] * g + cb_ref[:, sl]
    return _dot((jax.nn.silu(gc) * up).astype(MXU_DTYPE), wd_ref[sl, :])


def _out_ffn_kernel(x_ref, oa_ref, ob_ref, oc_ref, wo_ref, g2_ref, wg_ref, wu_ref, cw_ref, cb_ref, wd_ref,
                    y_ref, conv_ref, carry_s, *, tiles_per_seq, tf):
    i = pl.program_id(0)
    tm = x_ref.shape[0]
    x1 = _mix_residual(x_ref, oa_ref, ob_ref, oc_ref, wo_ref)
    xn = _rms(x1, g2_ref[...]).astype(MXU_DTYPE)

    @pl.when(i % tiles_per_seq == 0)
    def _seq_start():
        carry_s[...] = jnp.zeros(carry_s.shape, F32)

    row = lax.broadcasted_iota(jnp.int32, (tm, 1), 0)
    acc = x1
    for f in range(FFN_DIM // tf):
        sl = slice(f * tf, (f + 1) * tf)
        g = _dot(xn, wg_ref[:, sl])
        prev2, prev1 = carry_s[0:1, sl], carry_s[1:2, sl]
        g1 = jnp.where(row == 0, prev1, pltpu.roll(g, 1, 0))
        g2 = jnp.where(row == 0, prev2, jnp.where(row == 1, prev1, pltpu.roll(g, 2, 0)))
        acc = acc + _ffn_chunk(xn, g, g1, g2, sl, wu_ref, cw_ref, cb_ref, wd_ref)
        carry_s[0:2, sl] = g[tm - 2:tm, :]
    y_ref[...] = acc
    conv_ref[0] = carry_s[0:2, :]


def _out_ffn(x2d, oa, ob, oc, wo, g2, wg, wu, cw, cb, wd, *, seq, tm, tf):
    m = x2d.shape[0]
    tiles_per_seq = seq // tm
    kern = functools.partial(_out_ffn_kernel, tiles_per_seq=tiles_per_seq, tf=tf)
    row = lambda w: pl.BlockSpec((tm, w), lambda i: (i, 0))
    const = lambda shape: pl.BlockSpec(shape, lambda i: (0,) * len(shape), pipeline_mode=pl.Buffered(1))
    return pl.pallas_call(
        kern,
        grid=(m // tm,),
        in_specs=[
            row(D_MODEL), row(A_WIDTH), row(S5_WIDTH), row(C_WIDTH),
            const((D_MODEL, D_MODEL)), const((1, D_MODEL)),
            const((D_MODEL, FFN_DIM)), const((D_MODEL, FFN_DIM)),
            const((CONV_W, FFN_DIM)), const((1, FFN_DIM)), const((FFN_DIM, D_MODEL)),
        ],
        out_specs=[
            row(D_MODEL),
            pl.BlockSpec((1, CONV_W - 1, FFN_DIM), lambda i: (i // tiles_per_seq, 0, 0)),
        ],
        out_shape=[
            jax.ShapeDtypeStruct((m, D_MODEL), F32),
            jax.ShapeDtypeStruct((m // seq, CONV_W - 1, FFN_DIM), F32),
        ],
        scratch_shapes=[pltpu.VMEM((8, FFN_DIM), F32)],
        compiler_params=pltpu.CompilerParams(dimension_semantics=("arbitrary",), vmem_limit_bytes=VMEM_LIMIT),
        name="out_ffn",
    )(x2d, oa, ob, oc, wo, g2, wg, wu, cw, cb, wd)


def _layer_params(l, t5_bias, norm1_g, w_in, a_q_gain, a_k_gain, a_rel_bias, s5_lam_re, s5_lam_im, s5_log_dt,
                  s5_b_re, s5_b_im, s5_c_re, s5_c_im, s5_d, s5_w_glu, s5_b_glu, c_q_gain, c_k_gain, w_out,
                  norm2_g, ffn_w_gate, ffn_w_up, ffn_conv_w, ffn_conv_b, ffn_w_down):
    w = w_in[l]
    sizes = (A_WIDTH, A_WIDTH, A_WIDTH, S5_WIDTH, C_WIDTH, HEAD_DIM, HEAD_DIM, IDX_HEADS * IDX_DIM, IDX_DIM, IDX_HEADS)
    cuts = [0]
    for s in sizes:
        cuts.append(cuts[-1] + s)
    aq, ak, av, u, cq, ck, cv, qi, ki, wi = [w[:, cuts[n]:cuts[n + 1]] for n in range(len(sizes))]
    pad = jnp.zeros((D_MODEL, Z_WIDTH - COL_KIW - IDX_DIM - IDX_HEADS), w.dtype)
    w_r = jnp.concatenate([aq, ak, av, cq, u, qi, ck, cv, ki, wi, pad], axis=1).astype(MXU_DTYPE)
    return dict(
        norm1=norm1_g[l].reshape(1, D_MODEL), w_in=w_r,
        a_gq=a_q_gain[l].reshape(1, HEAD_DIM), a_gk=a_k_gain[l].reshape(1, HEAD_DIM), a_rel=a_rel_bias[l],
        s5=_s5_params(s5_lam_re[l], s5_lam_im[l], s5_log_dt[l], s5_b_re[l], s5_b_im[l], s5_c_re[l], s5_c_im[l]),
        s5_d=s5_d[l].reshape(1, S5_WIDTH), s5_wg=s5_w_glu[l].astype(MXU_DTYPE), s5_bg=s5_b_glu[l].reshape(1, S5_WIDTH),
        c_gq=c_q_gain[l].reshape(1, HEAD_DIM), c_gk=c_k_gain[l].reshape(1, HEAD_DIM),
        w_out=w_out[l].astype(MXU_DTYPE), norm2=norm2_g[l].reshape(1, D_MODEL),
        wg=ffn_w_gate[l].astype(MXU_DTYPE), wu=ffn_w_up[l].astype(MXU_DTYPE), cw=ffn_conv_w[l],
        cb=ffn_conv_b[l].reshape(1, FFN_DIM), wd=ffn_w_down[l].astype(MXU_DTYPE),
    )


def _prompt_layer(x, p, dsa_bias, *, tm, tf, s5_bt, s5_tc):
    b, seq, _ = x.shape
    x2d = x.reshape(b * seq, D_MODEL)
    z3 = _in_proj(x2d, p["norm1"], p["w_in"], tm).reshape(b, seq, Z_WIDTH)
    oa, a_k, a_v = _attn_a(z3, _attn_a_bias(p["a_rel"]), p["a_gq"], p["a_gk"])
    zero_h = jnp.zeros((b, S5_FLAT), F32)
    ob, h_re, h_im = _s5(z3, zero_h, zero_h, p["s5"], p["s5_d"], p["s5_wg"], p["s5_bg"],
                         bt=s5_bt, tc=s5_tc, exact_in=False)
    oc, c_k, c_v, c_ki = _dsa(z3, dsa_bias, p["c_gq"], p["c_gk"])
    y, conv = _out_ffn(x2d, oa.reshape(b * seq, A_WIDTH), ob.reshape(b * seq, S5_WIDTH),
                       oc.reshape(b * seq, C_WIDTH), p["w_out"], p["norm2"], p["wg"], p["wu"], p["cw"], p["cb"],
                       p["wd"], seq=seq, tm=tm, tf=tf)
    keep = a_k.shape[1]
    states = (a_k.reshape(b, keep, A_HEADS, HEAD_DIM), a_v.reshape(b, keep, A_HEADS, HEAD_DIM),
              h_re.reshape(b, S5_GROUPS, S5_STATE), h_im.reshape(b, S5_GROUPS, S5_STATE), c_k, c_v, c_ki, conv)
    return y.reshape(b, seq, D_MODEL), states


def _attn_a_step_kernel(q_ref, k_ref, v_ref, ck_ref, cv_ref, bc_ref, bn_ref, gq_ref, gk_ref, o_ref, ak_ref, av_ref):
    v = v_ref[0]
    av_ref[0] = v
    for h in range(A_HEADS):
        sl = slice(h * HEAD_DIM, (h + 1) * HEAD_DIM)
        qh = (_rms(q_ref[0, :, sl], gq_ref[...]) * HEAD_DIM ** -0.5).astype(MXU_DTYPE)
        khn = _rms(k_ref[0, :, sl], gk_ref[...])
        ak_ref[0, :, sl] = khn
        s_c = _nt_dot(qh, ck_ref[0, :, sl].astype(MXU_DTYPE)) + bc_ref[h]
        s_n = _nt_dot(qh, khn.astype(MXU_DTYPE)) + bn_ref[h]
        m = jnp.maximum(jnp.max(s_c, axis=-1, keepdims=True), jnp.max(s_n, axis=-1, keepdims=True))
        p_c = jnp.exp(s_c - m)
        p_n = jnp.exp(s_n - m)
        denom = jnp.sum(p_c, axis=-1, keepdims=True) + jnp.sum(p_n, axis=-1, keepdims=True)
        oh = (_dot(p_c.astype(MXU_DTYPE), cv_ref[0, :, sl].astype(MXU_DTYPE))
              + _dot(p_n.astype(MXU_DTYPE), v[:, sl].astype(MXU_DTYPE)))
        o_ref[0, :, sl] = oh / denom


def _attn_a_step_bias(rel_table, past, t_new):
    t = jnp.arange(t_new)[:, None]
    rel = jnp.concatenate([jnp.arange(past) - past, jnp.arange(t_new)])[None, :] - t
    bias = rel_table.astype(F32)[:, jnp.clip(rel, -A_MAX_REL, A_MAX_REL) + A_MAX_REL]
    return bias[:, :, :past], bias[:, :, past:]


def _attn_a_step(z3, cache_k, cache_v, bias_c, bias_n, gq, gk):
    b, t_new, _ = z3.shape
    past = cache_k.shape[1]
    new = lambda col: pl.BlockSpec((1, t_new, A_WIDTH), lambda i: (i, 0, col // A_WIDTH))
    const = lambda shape: pl.BlockSpec(shape, lambda i: (0,) * len(shape))
    return pl.pallas_call(
        _attn_a_step_kernel,
        grid=(b,),
        in_specs=[
            new(COL_AQ), new(COL_AK), new(COL_AV),
            pl.BlockSpec((1, past, A_WIDTH), lambda i: (i, 0, 0)),
            pl.BlockSpec((1, past, A_WIDTH), lambda i: (i, 0, 0)),
            const((A_HEADS, t_new, past)), const((A_HEADS, t_new, t_new)),
            const((1, HEAD_DIM)), const((1, HEAD_DIM)),
        ],
        out_specs=[pl.BlockSpec((1, t_new, A_WIDTH), lambda i: (i, 0, 0))] * 3,
        out_shape=[jax.ShapeDtypeStruct((b, t_new, A_WIDTH), F32)] * 3,
        compiler_params=pltpu.CompilerParams(dimension_semantics=("parallel",), vmem_limit_bytes=VMEM_LIMIT),
        name="attn_a_step",
    )(z3, z3, z3, cache_k, cache_v, bias_c, bias_n, gq, gk)


def _dsa_step_kernel(cq_ref, ckv_ref, qi_ref, kiw_ref, ck_ref, cv_ref, cki_ref, gq_ref, gk_ref, bias_ref,
                     o_ref, ck_out, cv_out, cki_out, *, past, t_new, topk):
    n_keys = past + t_new
    kn_new = _rms(ckv_ref[0, :, 0:HEAD_DIM], gk_ref[...])
    cv_new = ckv_ref[0, :, HEAD_DIM:2 * HEAD_DIM]
    ki_new = kiw_ref[0, :, 0:IDX_DIM]
    ck_out[0] = kn_new
    cv_out[0] = cv_new
    cki_out[0] = ki_new
    k_all = jnp.concatenate([ck_ref[0], kn_new], axis=0).astype(MXU_DTYPE)
    v_all = jnp.concatenate([cv_ref[0], cv_new], axis=0).astype(MXU_DTYPE)
    ki_all = jnp.concatenate([cki_ref[0], ki_new], axis=0).astype(MXU_DTYPE)
    eye = (lax.broadcasted_iota(jnp.int32, (HEAD_DIM, HEAD_DIM), 0)
           == lax.broadcasted_iota(jnp.int32, (HEAD_DIM, HEAD_DIM), 1)).astype(MXU_DTYPE)
    v_t = _nt_dot(eye, v_all).astype(MXU_DTYPE)

    qi = qi_ref[0].astype(MXU_DTYPE)
    kiw_pad = jnp.concatenate([kiw_ref[0], jnp.zeros((128 - t_new, 128), F32)], axis=0)
    w_t = kiw_pad.T[:, 0:t_new]
    score = jnp.zeros((n_keys, t_new), F32)
    for h in range(IDX_HEADS):
        dots = _nt_dot(ki_all, qi[:, h * IDX_DIM:(h + 1) * IDX_DIM])
        score = score + (w_t[IDX_DIM + h:IDX_DIM + h + 1, :] * INDEX_SCALE) * jnp.maximum(dots, 0.0)
    pos = lax.broadcasted_iota(jnp.int32, (n_keys, 1), 0)
    q_chunk = (past + lax.broadcasted_iota(jnp.int32, (1, t_new), 1)) // CHUNK
    keys = jnp.where((pos // CHUNK) <= q_chunk, _sortable_key(score), jnp.int32(INT_MIN))

    def count(pred):
        return jnp.sum(jnp.where(pred(keys, pos), 1, 0), axis=0, keepdims=True)

    def kth_body(it, res):
        cand = res | lax.shift_left(jnp.int32(1), (31 - it).astype(jnp.int32))
        thr_c = cand ^ jnp.int32(INT_MIN)
        return jnp.where(count(lambda k, p: k >= thr_c) >= topk, cand, res)

    thr = lax.fori_loop(0, 32, kth_body, jnp.zeros((1, t_new), jnp.int32)) ^ jnp.int32(INT_MIN)
    real_thr = thr != jnp.int32(INT_MIN)
    need = topk - count(lambda k, p: k > thr)
    pos_bits = (n_keys - 1).bit_length()

    def tie_body(it, v):
        cand = v | lax.shift_left(jnp.int32(1), (pos_bits - 1 - it).astype(jnp.int32))
        return jnp.where(count(lambda k, p: (k == thr) & (p < cand)) < need, cand, v)

    tie_last = lax.fori_loop(0, pos_bits, tie_body, jnp.zeros((1, t_new), jnp.int32))
    tie_last = jnp.where(real_thr, tie_last, -1)
    sel = (keys > thr) | ((keys == thr) & (pos <= tie_last))

    cq = cq_ref[0]
    outs = []
    for h in range(C_HEADS):
        qh = (_rms(cq[:, h * HEAD_DIM:(h + 1) * HEAD_DIM], gq_ref[...]) * HEAD_DIM ** -0.5).astype(MXU_DTYPE)
        s = jnp.where(sel, _nt_dot(k_all, qh) + bias_ref[h], NEG_INF)
        p = jnp.where(sel, jnp.exp(s - jnp.max(s, axis=0, keepdims=True)), 0.0)
        outs.append(_dot(v_t, p.astype(MXU_DTYPE)) / jnp.sum(p, axis=0, keepdims=True))
    out_t = jnp.concatenate(outs, axis=0)
    out_pad = jnp.concatenate([out_t, jnp.zeros((C_WIDTH, 128 - t_new), F32)], axis=1)
    o_ref[0] = out_pad.T[0:t_new, :]


def _dsa_step_bias(t5_table, past, t_new):
    s = jnp.arange(past + t_new)[:, None]
    t = jnp.arange(t_new)[None, :]
    return t5_table.astype(F32)[_t5_bucket(s - (past + t))].transpose(2, 0, 1)


def _dsa_step(z3, cache_k, cache_v, cache_ki, bias, gq, gk):
    b, t_new, _ = z3.shape
    past = cache_k.shape[1]
    topk = min(TOPK_MAX, (past + t_new) // 4)
    kern = functools.partial(_dsa_step_kernel, past=past, t_new=t_new, topk=topk)
    new = lambda w, col: pl.BlockSpec((1, t_new, w), lambda i: (i, 0, col // w))
    per_seq = lambda n, w: pl.BlockSpec((1, n, w), lambda i: (i, 0, 0))
    const = lambda shape: pl.BlockSpec(shape, lambda i: (0,) * len(shape))
    return pl.pallas_call(
        kern,
        grid=(b,),
        in_specs=[
            new(C_WIDTH, COL_CQ), new(128, COL_CKV), new(256, COL_QI), new(128, COL_KIW),
            per_seq(past, HEAD_DIM), per_seq(past, HEAD_DIM), per_seq(past, IDX_DIM),
            const((1, HEAD_DIM)), const((1, HEAD_DIM)), const((C_HEADS, past + t_new, t_new)),
        ],
        out_specs=[per_seq(t_new, C_WIDTH), per_seq(t_new, HEAD_DIM), per_seq(t_new, HEAD_DIM),
                   per_seq(t_new, IDX_DIM)],
        out_shape=[
            jax.ShapeDtypeStruct((b, t_new, C_WIDTH), F32),
            jax.ShapeDtypeStruct((b, t_new, HEAD_DIM), F32),
            jax.ShapeDtypeStruct((b, t_new, HEAD_DIM), F32),
            jax.ShapeDtypeStruct((b, t_new, IDX_DIM), F32),
        ],
        compiler_params=pltpu.CompilerParams(dimension_semantics=("parallel",), vmem_limit_bytes=VMEM_LIMIT),
        name="dsa_step",
    )(z3, z3, z3, z3, cache_k, cache_v, cache_ki, gq, gk, bias)


def _out_ffn_step_kernel(x_ref, oa_ref, ob_ref, oc_ref, wo_ref, g2_ref, wg_ref, wu_ref, cw_ref, cb_ref, wd_ref,
                         e1_ref, e2_ref, y_ref, gate_ref, *, t_new, tf):
    tm = x_ref.shape[0]
    x1 = _mix_residual(x_ref, oa_ref, ob_ref, oc_ref, wo_ref)
    xn = _rms(x1, g2_ref[...]).astype(MXU_DTYPE)
    t = lax.broadcasted_iota(jnp.int32, (tm, 1), 0) % t_new
    acc = x1
    for f in range(FFN_DIM // tf):
        sl = slice(f * tf, (f + 1) * tf)
        g = _dot(xn, wg_ref[:, sl])
        gate_ref[:, sl] = g
        g1 = jnp.where(t == 0, e1_ref[:, sl], pltpu.roll(g, 1, 0))
        g2 = jnp.where(t <= 1, e2_ref[:, sl], pltpu.roll(g, 2, 0))
        acc = acc + _ffn_chunk(xn, g, g1, g2, sl, wu_ref, cw_ref, cb_ref, wd_ref)
    y_ref[...] = acc


def _out_ffn_step(x2d, oa, ob, oc, wo, g2, wg, wu, cw, cb, wd, conv_prev, *, t_new, tf):
    m = x2d.shape[0]
    b = m // t_new
    e1 = jnp.zeros((b, t_new, FFN_DIM), F32).at[:, 0].set(conv_prev[:, 1]).reshape(m, FFN_DIM)
    e2 = jnp.zeros((b, t_new, FFN_DIM), F32).at[:, 0].set(conv_prev[:, 0]).at[:, 1].set(conv_prev[:, 1])
    e2 = e2.reshape(m, FFN_DIM)
    kern = functools.partial(_out_ffn_step_kernel, t_new=t_new, tf=tf)
    full = lambda shape: pl.BlockSpec(shape, lambda i: (0,) * len(shape))
    y, gate = pl.pallas_call(
        kern,
        grid=(1,),
        in_specs=[
            full((m, D_MODEL)), full((m, A_WIDTH)), full((m, S5_WIDTH)), full((m, C_WIDTH)),
            full((D_MODEL, D_MODEL)), full((1, D_MODEL)),
            full((D_MODEL, FFN_DIM)), full((D_MODEL, FFN_DIM)),
            full((CONV_W, FFN_DIM)), full((1, FFN_DIM)), full((FFN_DIM, D_MODEL)),
            full((m, FFN_DIM)), full((m, FFN_DIM)),
        ],
        out_specs=[full((m, D_MODEL)), full((m, FFN_DIM))],
        out_shape=[jax.ShapeDtypeStruct((m, D_MODEL), F32), jax.ShapeDtypeStruct((m, FFN_DIM), F32)],
        compiler_params=pltpu.CompilerParams(dimension_semantics=("arbitrary",), vmem_limit_bytes=VMEM_LIMIT),
        name="out_ffn_step",
    )(x2d, oa, ob, oc, wo, g2, wg, wu, cw, cb, wd, e1, e2)
    return y, gate.reshape(b, t_new, FFN_DIM)[:, t_new - (CONV_W - 1):]


def _sample_layer(x, p, t5_table, ca_k, ca_v, h_re0, h_im0, cc_k, cc_v, cc_ki, conv_prev, *, tf):
    b, t_new, _ = x.shape
    m = b * t_new
    x2d = x.reshape(m, D_MODEL)
    z3 = _in_proj(x2d, p["norm1"], p["w_in"], m).reshape(b, t_new, Z_WIDTH)
    a_past = ca_k.shape[1]
    bias_c, bias_n = _attn_a_step_bias(p["a_rel"], a_past, t_new)
    oa, a_k, a_v = _attn_a_step(z3, ca_k.reshape(b, a_past, A_WIDTH), ca_v.reshape(b, a_past, A_WIDTH),
                                bias_c, bias_n, p["a_gq"], p["a_gk"])
    ob, h_re, h_im = _s5(z3, h_re0.reshape(b, S5_FLAT), h_im0.reshape(b, S5_FLAT), p["s5"], p["s5_d"], p["s5_wg"],
                         p["s5_bg"], bt=8, tc=t_new, exact_in=True)
    oc, c_k, c_v, c_ki = _dsa_step(z3, cc_k, cc_v, cc_ki, _dsa_step_bias(t5_table, cc_k.shape[1], t_new),
                                   p["c_gq"], p["c_gk"])
    y, conv = _out_ffn_step(x2d, oa.reshape(m, A_WIDTH), ob.reshape(m, S5_WIDTH), oc.reshape(m, C_WIDTH),
                            p["w_out"], p["norm2"], p["wg"], p["wu"], p["cw"], p["cb"], p["wd"], conv_prev,
                            t_new=t_new, tf=tf)
    states = (a_k.reshape(b, t_new, A_HEADS, HEAD_DIM), a_v.reshape(b, t_new, A_HEADS, HEAD_DIM),
              h_re.reshape(b, S5_GROUPS, S5_STATE), h_im.reshape(b, S5_GROUPS, S5_STATE), c_k, c_v, c_ki, conv)
    return y.reshape(b, t_new, D_MODEL), states


def kernel(x_prompt, x_sample, cache_a_k, cache_a_v, state_s5_re, state_s5_im, cache_c_k, cache_c_v, cache_c_idx_k,
           state_ffn_conv, t5_bias, norm1_g, w_in, a_q_gain, a_k_gain, a_rel_bias, s5_lam_re, s5_lam_im, s5_log_dt,
           s5_b_re, s5_b_im, s5_c_re, s5_c_im, s5_d, s5_w_glu, s5_b_glu, c_q_gain, c_k_gain, w_out, norm2_g,
           ffn_w_gate, ffn_w_up, ffn_conv_w, ffn_conv_b, ffn_w_down):
    depth = w_in.shape[0]
    dsa_bias = _dsa_bias_tiles(t5_bias)
    yp, ys = x_prompt, x_sample
    prompt_states, sample_states = [], []
    for l in range(depth):
        p = _layer_params(l, t5_bias, norm1_g, w_in, a_q_gain, a_k_gain, a_rel_bias, s5_lam_re, s5_lam_im, s5_log_dt,
                          s5_b_re, s5_b_im, s5_c_re, s5_c_im, s5_d, s5_w_glu, s5_b_glu, c_q_gain, c_k_gain, w_out,
                          norm2_g, ffn_w_gate, ffn_w_up, ffn_conv_w, ffn_conv_b, ffn_w_down)
        yp, st_p = _prompt_layer(yp, p, dsa_bias, tm=512, tf=256, s5_bt=8, s5_tc=256)
        ys, st_s = _sample_layer(ys, p, t5_bias, cache_a_k[l], cache_a_v[l], state_s5_re[l], state_s5_im[l],
                                 cache_c_k[l], cache_c_v[l], cache_c_idx_k[l], state_ffn_conv[l], tf=256)
        prompt_states.append(st_p)
        sample_states.append(st_s)
    (a_k_p, a_v_p, s5_re_p, s5_im_p, c_k_p, c_v_p, c_ki_p, conv_p) = [jnp.stack(z) for z in zip(*prompt_states)]
    (a_k_s, a_v_s, s5_re_s, s5_im_s, c_k_s, c_v_s, c_ki_s, conv_s) = [jnp.stack(z) for z in zip(*sample_states)]
    return (yp, ys, a_k_p, a_v_p, a_k_s, a_v_s, s5_re_p, s5_im_p, s5_re_s, s5_im_s,
            c_k_p, c_v_p, c_ki_p, c_k_s, c_v_s, c_ki_s, conv_p, conv_s)
```

```python
import functools
import math

import jax
import jax.numpy as jnp
from jax import lax
from jax.experimental import pallas as pl
from jax.experimental.pallas import tpu as pltpu

F32 = jnp.float32
MXU_DTYPE = jnp.bfloat16

D_MODEL = 1024
CHUNK = 64
HEAD_DIM = 64
A_HEADS = 6
A_WIDTH = A_HEADS * HEAD_DIM
A_BAND_PAST = 8 * CHUNK
A_MAX_REL = 128
S5_GROUPS = 16
S5_GROUP_CH = 16
S5_WIDTH = S5_GROUPS * S5_GROUP_CH
S5_STATE = 64
S5_FLAT = S5_GROUPS * S5_STATE
C_HEADS = 6
C_WIDTH = C_HEADS * HEAD_DIM
IDX_HEADS = 8
IDX_DIM = 32
INDEX_SCALE = (IDX_HEADS * IDX_DIM) ** -0.5
TOPK_MAX = 256
Q_BLOCK = 128
T5_BUCKETS = 32
T5_MAX_DIST = 128
FFN_DIM = 2816
CONV_W = 3
EPS = 1e-6
NEG_INF = -1e30
INT_MIN = -(2 ** 31)

COL_AQ, COL_AK, COL_AV, COL_CQ = 0, 384, 768, 1152
COL_U, COL_QI, COL_CKV, COL_KIW = 1536, 1792, 2048, 2176
Z_WIDTH = 2304
VMEM_LIMIT = 56 * 1024 * 1024


def _nt_dot(a, b):
    return lax.dot_general(a, b, (((1,), (1,)), ((), ())), preferred_element_type=F32)


def _dot(a, b):
    return jnp.dot(a, b, preferred_element_type=F32)


def _rms(x, g):
    return x * lax.rsqrt(jnp.mean(x * x, axis=-1, keepdims=True) + EPS) * g


def _in_proj_kernel(x_ref, g_ref, w_ref, z_ref):
    xn = _rms(x_ref[...], g_ref[...]).astype(MXU_DTYPE)
    z_ref[...] = _dot(xn, w_ref[...])


def _in_proj(x2d, g, w_r, tm):
    m = x2d.shape[0]
    return pl.pallas_call(
        _in_proj_kernel,
        grid=(m // tm,),
        in_specs=[
            pl.BlockSpec((tm, D_MODEL), lambda i: (i, 0)),
            pl.BlockSpec((1, D_MODEL), lambda i: (0, 0)),
            pl.BlockSpec((D_MODEL, Z_WIDTH), lambda i: (0, 0)),
        ],
        out_specs=pl.BlockSpec((tm, Z_WIDTH), lambda i: (i, 0)),
        out_shape=jax.ShapeDtypeStruct((m, Z_WIDTH), F32),
        compiler_params=pltpu.CompilerParams(dimension_semantics=("parallel",), vmem_limit_bytes=VMEM_LIMIT),
        name="in_proj",
    )(x2d, g, w_r)


A_QB = 2 * CHUNK
A_BAND = A_BAND_PAST + A_QB


def _eye(n, dtype):
    return (lax.broadcasted_iota(jnp.int32, (n, n), 0) == lax.broadcasted_iota(jnp.int32, (n, n), 1)).astype(dtype)


def _attn_a_kernel(q_ref, k_ref, v_ref, bias_ref, gq_ref, gk_ref, o_ref, ak_ref, av_ref, kn_s, vt_s, s_s, p_s,
                   *, seq, keep):
    j = pl.program_id(1)

    @pl.when(j == 0)
    def _prep():
        for h in range(A_HEADS):
            sl = slice(h * HEAD_DIM, (h + 1) * HEAD_DIM)
            khn = _rms(k_ref[0, :, sl], gk_ref[...])
            kn_s[h, 0:A_BAND_PAST, :] = jnp.zeros((A_BAND_PAST, HEAD_DIM), MXU_DTYPE)
            kn_s[h, A_BAND_PAST:A_BAND_PAST + seq, :] = khn.astype(MXU_DTYPE)
            ak_ref[0, :, sl] = khn[seq - keep:, :]
        v = v_ref[0]
        av_ref[0] = v[seq - keep:, :]
        vt_s[:, 0:A_BAND_PAST] = jnp.zeros((A_WIDTH, A_BAND_PAST), MXU_DTYPE)
        vt_s[:, A_BAND_PAST:A_BAND_PAST + seq] = _nt_dot(_eye(A_WIDTH, MXU_DTYPE), v.astype(MXU_DTYPE)).astype(MXU_DTYPE)

    start = pl.multiple_of(j * A_QB, A_QB)
    variant = jnp.minimum(j, A_LEAD_STEPS)
    for h in range(A_HEADS):
        sl = slice(h * HEAD_DIM, (h + 1) * HEAD_DIM)
        qh = (_rms(q_ref[0, :, sl], gq_ref[...]) * HEAD_DIM ** -0.5).astype(MXU_DTYPE)
        s_s[h] = _nt_dot(kn_s[h, pl.ds(start, A_BAND), :], qh) + bias_ref[variant, h]
    denoms = []
    for h in range(A_HEADS):
        s = s_s[h]
        p = jnp.exp(s - jnp.max(s, axis=0, keepdims=True))
        denoms.append(jnp.sum(p, axis=0, keepdims=True))
        p_s[h] = p.astype(MXU_DTYPE)
    outs = [_dot(vt_s[h * HEAD_DIM:(h + 1) * HEAD_DIM, pl.ds(start, A_BAND)], p_s[h]) / denoms[h]
            for h in range(A_HEADS)]
    o_ref[0] = jnp.concatenate(outs, axis=0).T


A_LEAD_STEPS = A_BAND_PAST // A_QB


def _attn_a_bias(rel_table):
    m = jnp.arange(A_BAND + A_QB - 1)
    ext = rel_table.astype(F32)[:, jnp.clip(m - (A_QB - 1) - A_BAND_PAST, -A_MAX_REL, A_MAX_REL) + A_MAX_REL]
    bias = jnp.stack([ext[:, A_QB - 1 - r:A_QB - 1 - r + A_BAND] for r in range(A_QB)], axis=-1)
    c = jnp.arange(A_BAND)[:, None]
    lo = (jnp.arange(A_QB)[None, :] // CHUNK) * CHUNK
    in_band = (c >= lo) & (c < lo + A_BAND_PAST + CHUNK)
    first_live = (A_LEAD_STEPS - jnp.arange(A_LEAD_STEPS + 1)) * A_QB
    live = in_band[None] & (c[None] >= first_live[:, None, None])
    return jnp.where(live[:, None], bias[None], NEG_INF)


def _attn_a(z3, bias, gq, gk):
    b, seq, _ = z3.shape
    keep = min(A_BAND_PAST, seq)
    kern = functools.partial(_attn_a_kernel, seq=seq, keep=keep)
    return pl.pallas_call(
        kern,
        grid=(b, seq // A_QB),
        in_specs=[
            pl.BlockSpec((1, A_QB, A_WIDTH), lambda i, j: (i, j, COL_AQ // A_WIDTH)),
            pl.BlockSpec((1, seq, A_WIDTH), lambda i, j: (i, 0, COL_AK // A_WIDTH)),
            pl.BlockSpec((1, seq, A_WIDTH), lambda i, j: (i, 0, COL_AV // A_WIDTH)),
            pl.BlockSpec((A_LEAD_STEPS + 1, A_HEADS, A_BAND, A_QB), lambda i, j: (0, 0, 0, 0),
                         pipeline_mode=pl.Buffered(1)),
            pl.BlockSpec((1, HEAD_DIM), lambda i, j: (0, 0)),
            pl.BlockSpec((1, HEAD_DIM), lambda i, j: (0, 0)),
        ],
        out_specs=[
            pl.BlockSpec((1, A_QB, A_WIDTH), lambda i, j: (i, j, 0)),
            pl.BlockSpec((1, keep, A_WIDTH), lambda i, j: (i, 0, 0)),
            pl.BlockSpec((1, keep, A_WIDTH), lambda i, j: (i, 0, 0)),
        ],
        out_shape=[
            jax.ShapeDtypeStruct((b, seq, A_WIDTH), F32),
            jax.ShapeDtypeStruct((b, keep, A_WIDTH), F32),
            jax.ShapeDtypeStruct((b, keep, A_WIDTH), F32),
        ],
        scratch_shapes=[
            pltpu.VMEM((A_HEADS, A_BAND_PAST + seq, HEAD_DIM), MXU_DTYPE),
            pltpu.VMEM((A_WIDTH, A_BAND_PAST + seq), MXU_DTYPE),
            pltpu.VMEM((A_HEADS, A_BAND, A_QB), F32),
            pltpu.VMEM((A_HEADS, A_BAND, A_QB), MXU_DTYPE),
        ],
        compiler_params=pltpu.CompilerParams(
            dimension_semantics=("parallel", "arbitrary"), vmem_limit_bytes=VMEM_LIMIT),
        name="attn_a",
    )(z3, z3, z3, bias, gq, gk)


def _s5_kernel(u_ref, h0r_ref, h0i_ref, ar_ref, ai_ref, wbr_ref, wbi_ref, wcr_ref, wci_ref, d_ref, wg_ref, bg_ref,
               o_ref, hr_out, hi_out, xr_s, xi_s, hr_c, hi_c, *, bt, tc, exact_in):
    c = pl.program_id(1)

    @pl.when(c == 0)
    def _init():
        hr_c[...] = h0r_ref[...]
        hi_c[...] = h0i_ref[...]

    u = jnp.swapaxes(u_ref[...], 0, 1).reshape(tc * bt, S5_WIDTH)
    if exact_in:
        xr_s[...] = jnp.dot(u, wbr_ref[...], preferred_element_type=F32, precision=lax.Precision.HIGHEST)
        xi_s[...] = jnp.dot(u, wbi_ref[...], preferred_element_type=F32, precision=lax.Precision.HIGHEST)
    else:
        ub = u.astype(MXU_DTYPE)
        xr_s[...] = _dot(ub, wbr_ref[...])
        xi_s[...] = _dot(ub, wbi_ref[...])
    ar = ar_ref[...]
    ai = ai_ref[...]

    def step(t, carry):
        hr, hi = carry
        rows = pl.ds(pl.multiple_of(t * bt, bt), bt)
        nhr = ar * hr - ai * hi + xr_s[rows, :]
        nhi = ar * hi + ai * hr + xi_s[rows, :]
        xr_s[rows, :] = nhr
        xi_s[rows, :] = nhi
        return nhr, nhi

    hr, hi = lax.fori_loop(0, tc, step, (hr_c[...], hi_c[...]))
    hr_c[...] = hr
    hi_c[...] = hi
    hr_out[...] = hr
    hi_out[...] = hi
    y = (_dot(xr_s[...].astype(MXU_DTYPE), wcr_ref[...]) - _dot(xi_s[...].astype(MXU_DTYPE), wci_ref[...])
         + d_ref[...] * u)
    g = jax.nn.gelu(y)
    out = g * jax.nn.sigmoid(_dot(g.astype(MXU_DTYPE), wg_ref[...]) + bg_ref[...])
    o_ref[...] = jnp.swapaxes(out.reshape(tc, bt, S5_WIDTH), 0, 1)


def _s5_params(lam_re, lam_im, log_dt, b_re, b_im, c_re, c_im):
    dt = jnp.exp(log_dt.astype(F32))[:, None]
    lr, li = lam_re.astype(F32), lam_im.astype(F32)
    mag = jnp.exp(lr * dt)
    ab_re, ab_im = mag * jnp.cos(li * dt), mag * jnp.sin(li * dt)
    den = lr * lr + li * li
    cr = ((ab_re - 1.0) * lr + ab_im * li) / den
    ci = (ab_im * lr - (ab_re - 1.0) * li) / den
    br, bi = b_re.astype(F32), b_im.astype(F32)
    bb_re = cr[..., None] * br - ci[..., None] * bi
    bb_im = cr[..., None] * bi + ci[..., None] * br
    eye = jnp.eye(S5_GROUPS, dtype=F32)
    wb_re = jnp.einsum("gpc,gh->gchp", bb_re, eye).reshape(S5_WIDTH, S5_FLAT)
    wb_im = jnp.einsum("gpc,gh->gchp", bb_im, eye).reshape(S5_WIDTH, S5_FLAT)
    wc_re = jnp.einsum("gcp,gh->gphc", c_re.astype(F32), eye).reshape(S5_FLAT, S5_WIDTH).astype(MXU_DTYPE)
    wc_im = jnp.einsum("gcp,gh->gphc", c_im.astype(F32), eye).reshape(S5_FLAT, S5_WIDTH).astype(MXU_DTYPE)
    return ab_re.reshape(1, S5_FLAT), ab_im.reshape(1, S5_FLAT), wb_re, wb_im, wc_re, wc_im


def _s5(z3, h0r, h0i, sp, d, wg, bg, *, bt, tc, exact_in):
    b, seq, _ = z3.shape
    ar, ai, wbr, wbi, wcr, wci = sp
    if not exact_in:
        wbr, wbi = wbr.astype(MXU_DTYPE), wbi.astype(MXU_DTYPE)
    kern = functools.partial(_s5_kernel, bt=bt, tc=tc, exact_in=exact_in)
    const = lambda shape: pl.BlockSpec(shape, lambda i, c: (0,) * len(shape))
    return pl.pallas_call(
        kern,
        grid=(b // bt, seq // tc),
        in_specs=[
            pl.BlockSpec((bt, tc, S5_WIDTH), lambda i, c: (i, c, COL_U // S5_WIDTH)),
            pl.BlockSpec((bt, S5_FLAT), lambda i, c: (i, 0)),
            pl.BlockSpec((bt, S5_FLAT), lambda i, c: (i, 0)),
            const((1, S5_FLAT)), const((1, S5_FLAT)),
            const((S5_WIDTH, S5_FLAT)), const((S5_WIDTH, S5_FLAT)),
            const((S5_FLAT, S5_WIDTH)), const((S5_FLAT, S5_WIDTH)),
            const((1, S5_WIDTH)), const((S5_WIDTH, S5_WIDTH)), const((1, S5_WIDTH)),
        ],
        out_specs=[
            pl.BlockSpec((bt, tc, S5_WIDTH), lambda i, c: (i, c, 0)),
            pl.BlockSpec((bt, S5_FLAT), lambda i, c: (i, 0)),
            pl.BlockSpec((bt, S5_FLAT), lambda i, c: (i, 0)),
        ],
        out_shape=[
            jax.ShapeDtypeStruct((b, seq, S5_WIDTH), F32),
            jax.ShapeDtypeStruct((b, S5_FLAT), F32),
            jax.ShapeDtypeStruct((b, S5_FLAT), F32),
        ],
        scratch_shapes=[
            pltpu.VMEM((bt * tc, S5_FLAT), F32), pltpu.VMEM((bt * tc, S5_FLAT), F32),
            pltpu.VMEM((bt, S5_FLAT), F32), pltpu.VMEM((bt, S5_FLAT), F32),
        ],
        compiler_params=pltpu.CompilerParams(
            dimension_semantics=("parallel", "arbitrary"), vmem_limit_bytes=VMEM_LIMIT),
        name="s5",
    )(z3, h0r, h0i, ar, ai, wbr, wbi, wcr, wci, d, wg, bg)


def _sortable_key(score):
    bits = lax.bitcast_convert_type(score + 0.0, jnp.int32)
    return bits ^ ((bits >> 31) & jnp.int32(0x7FFFFFFF))


DSA_KT = 2 * Q_BLOCK
I16_MIN = -(2 ** 15)


def _fold_rows(x, rows):
    parts = [x[i:i + rows] for i in range(0, x.shape[0], rows)]
    while len(parts) > 1:
        parts = [a + b for a, b in zip(parts[0::2], parts[1::2])] + parts[len(parts) & ~1:]
    return parts[0]


def _bisect16(count_ge, k):
    def body(it, res):
        cand = res | lax.shift_left(jnp.int32(1), (15 - it).astype(jnp.int32))
        cnt = count_ge((cand + I16_MIN).astype(jnp.int16))
        return jnp.where(cnt >= k, cand, res)

    return lax.fori_loop(0, 16, body, jnp.zeros((1, Q_BLOCK), jnp.int32))


def _dsa_kernel(cq_ref, ckv_ref, qi_ref, kiwq_ref, kiwk_ref, gq_ref, gk_ref, bias_ref,
                o_ref, ck_out, cv_out, cki_out, kn_s, vt_s, ki_s, keys_s, hi_s, lo_s, m_s, l_s, acc_s, *, seq, topk):
    j = pl.program_id(1)
    nkt = j // 2 + 1

    @pl.when(j == 0)
    def _prep():
        kn = _rms(ckv_ref[0, :, 0:HEAD_DIM], gk_ref[...])
        cv = ckv_ref[0, :, HEAD_DIM:2 * HEAD_DIM]
        ck_out[0] = kn
        cv_out[0] = cv
        kn_s[...] = kn.astype(MXU_DTYPE)
        eye = (lax.broadcasted_iota(jnp.int32, (HEAD_DIM, HEAD_DIM), 0)
               == lax.broadcasted_iota(jnp.int32, (HEAD_DIM, HEAD_DIM), 1)).astype(MXU_DTYPE)
        vt_s[...] = _nt_dot(eye, cv.astype(MXU_DTYPE)).astype(MXU_DTYPE)
        ki = kiwk_ref[0, :, 0:IDX_DIM]
        cki_out[0] = ki
        ki_s[...] = ki.astype(MXU_DTYPE)

    cq = cq_ref[0]
    q_all = jnp.concatenate(
        [(_rms(cq[:, h * HEAD_DIM:(h + 1) * HEAD_DIM], gq_ref[...]) * HEAD_DIM ** -0.5).astype(MXU_DTYPE)
         for h in range(C_HEADS)], axis=0)
    qi = qi_ref[0].astype(MXU_DTYPE)
    qi_all = jnp.concatenate([qi[:, h * IDX_DIM:(h + 1) * IDX_DIM] for h in range(IDX_HEADS)], axis=0)
    w_t = kiwq_ref[0].T
    w_rows = [w_t[IDX_DIM + h:IDX_DIM + h + 1, :] * INDEX_SCALE for h in range(IDX_HEADS)]
    q_chunk = (j * Q_BLOCK + lax.broadcasted_iota(jnp.int32, (1, Q_BLOCK), 1)) // CHUNK
    kt_iota = lax.broadcasted_iota(jnp.int32, (DSA_KT, 1), 0)

    def trip_rows(kt):
        return pl.multiple_of(kt * DSA_KT, DSA_KT)

    def idx_body(kt, _):
        off = trip_rows(kt)
        dots = _nt_dot(ki_s[pl.ds(off, DSA_KT), :], qi_all)
        score = w_rows[0] * jnp.maximum(dots[:, 0:Q_BLOCK], 0.0)
        for h in range(1, IDX_HEADS):
            score = score + w_rows[h] * jnp.maximum(dots[:, h * Q_BLOCK:(h + 1) * Q_BLOCK], 0.0)
        adm = ((off + kt_iota) // CHUNK) <= q_chunk
        key = jnp.where(adm, _sortable_key(score), jnp.int32(INT_MIN))
        keys_s[pl.ds(off, DSA_KT), :] = key
        hi_s[pl.ds(off, DSA_KT), :] = (key >> 16).astype(jnp.int16)
        lo_s[pl.ds(off, DSA_KT), :] = ((key & 0xFFFF) + I16_MIN).astype(jnp.int16)
        return 0

    lax.fori_loop(0, nkt, idx_body, 0)

    def count16(ref, pred):
        def body(kt, acc):
            hit = jnp.where(pred(ref[pl.ds(trip_rows(kt), DSA_KT), :]), jnp.int16(1), jnp.int16(0))
            return acc + _fold_rows(hit, 16)

        acc = lax.fori_loop(0, nkt, body, jnp.zeros((16, Q_BLOCK), jnp.int16))
        return jnp.sum(acc.astype(jnp.int32), axis=0, keepdims=True)

    def count(pred):
        def body(kt, acc):
            off = trip_rows(kt)
            hit = jnp.where(pred(keys_s[pl.ds(off, DSA_KT), :], off + kt_iota), 1, 0)
            return acc + jnp.sum(hit.reshape(DSA_KT // 8, 8, Q_BLOCK), axis=0)

        acc = lax.fori_loop(0, nkt, body, jnp.zeros((8, Q_BLOCK), jnp.int32))
        return jnp.sum(acc, axis=0, keepdims=True)

    hi_u = _bisect16(lambda c: count16(hi_s, lambda v: v >= c), topk)
    thr_hi = (hi_u + I16_MIN).astype(jnp.int16)
    need_lo = topk - count16(hi_s, lambda v: v > thr_hi)

    def mask_lo(kt, _):
        rows = pl.ds(trip_rows(kt), DSA_KT)
        lo_s[rows, :] = jnp.where(hi_s[rows, :] == thr_hi, lo_s[rows, :], jnp.int16(I16_MIN))
        return 0

    lax.fori_loop(0, nkt, mask_lo, 0)
    lo_u = _bisect16(lambda c: count16(lo_s, lambda v: v >= c), need_lo)
    thr = lax.shift_left(hi_u + I16_MIN, 16) | lo_u
    real_thr = thr != jnp.int32(INT_MIN)
    n_tied = count16(lo_s, lambda v: v >= (lo_u + I16_MIN).astype(jnp.int16))
    pos_bits = (seq - 1).bit_length()

    def _tie_search():
        need = topk - count(lambda k, pos: k > thr)

        def body(it, v):
            cand = v | lax.shift_left(jnp.int32(1), (pos_bits - 1 - it).astype(jnp.int32))
            return jnp.where(count(lambda k, pos: (k == thr) & (pos < cand)) < need, cand, v)

        return lax.fori_loop(0, pos_bits, body, jnp.zeros((1, Q_BLOCK), jnp.int32))

    has_ties = jnp.max(jnp.where(real_thr & (n_tied > need_lo), 1, 0)) > 0
    tie_last = lax.cond(has_ties, _tie_search, lambda: jnp.full((1, Q_BLOCK), 2 ** pos_bits - 1, jnp.int32))
    tie_last = jnp.where(real_thr, tie_last, -1)

    m_s[...] = jnp.full(m_s.shape, NEG_INF, F32)
    l_s[...] = jnp.zeros(l_s.shape, F32)
    acc_s[...] = jnp.zeros(acc_s.shape, F32)

    def att_body(kt, _):
        off = trip_rows(kt)
        s_all = _nt_dot(kn_s[pl.ds(off, DSA_KT), :], q_all)
        keyc = keys_s[pl.ds(off, DSA_KT), :]
        sel = (keyc > thr) | ((keyc == thr) & (off + kt_iota <= tie_last))
        b_near = jnp.clip(j - 2 * kt, 0, 2)
        b_far = jnp.clip(j - 2 * kt - 1, 0, 2)
        vtc = vt_s[:, pl.ds(off, DSA_KT)]
        for h in range(C_HEADS):
            ls = slice(h * Q_BLOCK, (h + 1) * Q_BLOCK)
            bias = jnp.concatenate([bias_ref[b_near, :, ls], bias_ref[b_far, :, ls]], axis=0)
            s = jnp.where(sel, s_all[:, ls] + bias, NEG_INF)
            m_old = m_s[h:h + 1, :]
            m_new = jnp.maximum(m_old, jnp.max(s, axis=0, keepdims=True))
            alpha = jnp.exp(m_old - m_new)
            p = jnp.exp(s - m_new)
            l_s[h:h + 1, :] = alpha * l_s[h:h + 1, :] + jnp.sum(p, axis=0, keepdims=True)
            rs = slice(h * HEAD_DIM, (h + 1) * HEAD_DIM)
            acc_s[rs, :] = alpha * acc_s[rs, :] + _dot(vtc, p.astype(MXU_DTYPE))
            m_s[h:h + 1, :] = m_new
        return 0

    lax.fori_loop(0, nkt, att_body, 0)
    out_t = jnp.concatenate(
        [acc_s[h * HEAD_DIM:(h + 1) * HEAD_DIM, :] / l_s[h:h + 1, :] for h in range(C_HEADS)], axis=0)
    o_ref[0] = out_t.T


def _t5_bucket(rel):
    half = T5_BUCKETS // 2
    max_exact = half // 2
    n = jnp.abs(rel)
    log_val = jnp.log(jnp.maximum(n, 1).astype(F32) / max_exact) / math.log(T5_MAX_DIST / max_exact)
    large = jnp.minimum(max_exact + (log_val * (half - max_exact)).astype(jnp.int32), half - 1)
    return jnp.where(rel > 0, half, 0) + jnp.where(n < max_exact, n, large)


def _dsa_bias_tiles(t5_table):
    s = jnp.arange(Q_BLOCK)[:, None]
    t = jnp.arange(Q_BLOCK)[None, :]
    tiles = []
    for d in range(3):
        rel = s - d * Q_BLOCK - t
        tile = t5_table.astype(F32)[_t5_bucket(rel)]
        tiles.append(tile.transpose(0, 2, 1).reshape(Q_BLOCK, C_HEADS * Q_BLOCK))
    return jnp.stack(tiles)


def _dsa(z3, bias, gq, gk):
    b, seq, _ = z3.shape
    assert seq % DSA_KT == 0
    topk = min(TOPK_MAX, seq // 4)
    kern = functools.partial(_dsa_kernel, seq=seq, topk=topk)
    return pl.pallas_call(
        kern,
        grid=(b, seq // Q_BLOCK),
        in_specs=[
            pl.BlockSpec((1, Q_BLOCK, C_WIDTH), lambda i, j: (i, j, COL_CQ // C_WIDTH)),
            pl.BlockSpec((1, seq, 128), lambda i, j: (i, 0, COL_CKV // 128)),
            pl.BlockSpec((1, Q_BLOCK, 256), lambda i, j: (i, j, COL_QI // 256)),
            pl.BlockSpec((1, Q_BLOCK, 128), lambda i, j: (i, j, COL_KIW // 128)),
            pl.BlockSpec((1, seq, 128), lambda i, j: (i, 0, COL_KIW // 128)),
            pl.BlockSpec((1, HEAD_DIM), lambda i, j: (0, 0)),
            pl.BlockSpec((1, HEAD_DIM), lambda i, j: (0, 0)),
            pl.BlockSpec((3, Q_BLOCK, C_HEADS * Q_BLOCK), lambda i, j: (0, 0, 0)),
        ],
        out_specs=[
            pl.BlockSpec((1, Q_BLOCK, C_WIDTH), lambda i, j: (i, j, 0)),
            pl.BlockSpec((1, seq, HEAD_DIM), lambda i, j: (i, 0, 0)),
            pl.BlockSpec((1, seq, HEAD_DIM), lambda i, j: (i, 0, 0)),
            pl.BlockSpec((1, seq, IDX_DIM), lambda i, j: (i, 0, 0)),
        ],
        out_shape=[
            jax.ShapeDtypeStruct((b, seq, C_WIDTH), F32),
            jax.ShapeDtypeStruct((b, seq, HEAD_DIM), F32),
            jax.ShapeDtypeStruct((b, seq, HEAD_DIM), F32),
            jax.ShapeDtypeStruct((b, seq, IDX_DIM), F32),
        ],
        scratch_shapes=[
            pltpu.VMEM((seq, HEAD_DIM), MXU_DTYPE),
            pltpu.VMEM((HEAD_DIM, seq), MXU_DTYPE),
            pltpu.VMEM((seq, IDX_DIM), MXU_DTYPE),
            pltpu.VMEM((seq, Q_BLOCK), jnp.int32),
            pltpu.VMEM((seq, Q_BLOCK), jnp.int16),
            pltpu.VMEM((seq, Q_BLOCK), jnp.int16),
            pltpu.VMEM((8, Q_BLOCK), F32),
            pltpu.VMEM((8, Q_BLOCK), F32),
            pltpu.VMEM((C_WIDTH, Q_BLOCK), F32),
        ],
        compiler_params=pltpu.CompilerParams(
            dimension_semantics=("parallel", "arbitrary"), vmem_limit_bytes=VMEM_LIMIT),
        name="dsa",
    )(z3, z3, z3, z3, z3, gq, gk, bias)


def _mix_residual(x_ref, oa_ref, ob_ref, oc_ref, wo_ref):
    mix = (_dot(oa_ref[...].astype(MXU_DTYPE), wo_ref[0:A_WIDTH, :])
           + _dot(ob_ref[...].astype(MXU_DTYPE), wo_ref[A_WIDTH:A_WIDTH + S5_WIDTH, :])
           + _dot(oc_ref[...].astype(MXU_DTYPE), wo_ref[A_WIDTH + S5_WIDTH:, :]))
    return x_ref[...] + mix


def _ffn_chunk(xn, g, g1, g2, sl, wu_ref, cw_ref, cb_ref, wd_ref):
    up = _dot(xn, wu_ref[:, sl])
    gc = cw_ref[0:1, sl] * g2 + cw_ref[1:2, sl] * g1 + cw_ref[2:3, sl] * g + cb_ref[:, sl]
    return _dot((jax.nn.silu(gc) * up).astype(MXU_DTYPE), wd_ref[sl, :])


def _out_ffn_kernel(x_ref, oa_ref, ob_ref, oc_ref, wo_ref, g2_ref, wg_ref, wu_ref, cw_ref, cb_ref, wd_ref,
                    y_ref, conv_ref, carry_s, *, tiles_per_seq, tf):
    i = pl.program_id(0)
    tm = x_ref.shape[0]
    x1 = _mix_residual(x_ref, oa_ref, ob_ref, oc_ref, wo_ref)
    xn = _rms(x1, g2_ref[...]).astype(MXU_DTYPE)

    @pl.when(i % tiles_per_seq == 0)
    def _seq_start():
        carry_s[...] = jnp.zeros(carry_s.shape, F32)

    row = lax.broadcasted_iota(jnp.int32, (tm, 1), 0)
    acc = x1
    for f in range(FFN_DIM // tf):
        sl = slice(f * tf, (f + 1) * tf)
        g = _dot(xn, wg_ref[:, sl])
        prev2, prev1 = carry_s[0:1, sl], carry_s[1:2, sl]
        g1 = jnp.where(row == 0, prev1, pltpu.roll(g, 1, 0))
        g2 = jnp.where(row == 0, prev2, jnp.where(row == 1, prev1, pltpu.roll(g, 2, 0)))
        acc = acc + _ffn_chunk(xn, g, g1, g2, sl, wu_ref, cw_ref, cb_ref, wd_ref)
        carry_s[0:2, sl] = g[tm - 2:tm, :]
    y_ref[...] = acc
    conv_ref[0] = carry_s[0:2, :]


def _out_ffn(x2d, oa, ob, oc, wo, g2, wg, wu, cw, cb, wd, *, seq, tm, tf):
    m = x2d.shape[0]
    tiles_per_seq = seq // tm
    kern = functools.partial(_out_ffn_kernel, tiles_per_seq=tiles_per_seq, tf=tf)
    row = lambda w: pl.BlockSpec((tm, w), lambda i: (i, 0))
    const = lambda shape: pl.BlockSpec(shape, lambda i: (0,) * len(shape), pipeline_mode=pl.Buffered(1))
    return pl.pallas_call(
        kern,
        grid=(m // tm,),
        in_specs=[
            row(D_MODEL), row(A_WIDTH), row(S5_WIDTH), row(C_WIDTH),
            const((D_MODEL, D_MODEL)), const((1, D_MODEL)),
            const((D_MODEL, FFN_DIM)), const((D_MODEL, FFN_DIM)),
            const((CONV_W, FFN_DIM)), const((1, FFN_DIM)), const((FFN_DIM, D_MODEL)),
        ],
        out_specs=[
            row(D_MODEL),
            pl.BlockSpec((1, CONV_W - 1, FFN_DIM), lambda i: (i // tiles_per_seq, 0, 0)),
        ],
        out_shape=[
            jax.ShapeDtypeStruct((m, D_MODEL), F32),
            jax.ShapeDtypeStruct((m // seq, CONV_W - 1, FFN_DIM), F32),
        ],
        scratch_shapes=[pltpu.VMEM((8, FFN_DIM), F32)],
        compiler_params=pltpu.CompilerParams(dimension_semantics=("arbitrary",), vmem_limit_bytes=VMEM_LIMIT),
        name="out_ffn",
    )(x2d, oa, ob, oc, wo, g2, wg, wu, cw, cb, wd)


def _layer_params(l, t5_bias, norm1_g, w_in, a_q_gain, a_k_gain, a_rel_bias, s5_lam_re, s5_lam_im, s5_log_dt,
                  s5_b_re, s5_b_im, s5_c_re, s5_c_im, s5_d, s5_w_glu, s5_b_glu, c_q_gain, c_k_gain, w_out,
                  norm2_g, ffn_w_gate, ffn_w_up, ffn_conv_w, ffn_conv_b, ffn_w_down):
    w = w_in[l]
    sizes = (A_WIDTH, A_WIDTH, A_WIDTH, S5_WIDTH, C_WIDTH, HEAD_DIM, HEAD_DIM, IDX_HEADS * IDX_DIM, IDX_DIM, IDX_HEADS)
    cuts = [0]
    for s in sizes:
        cuts.append(cuts[-1] + s)
    aq, ak, av, u, cq, ck, cv, qi, ki, wi = [w[:, cuts[n]:cuts[n + 1]] for n in range(len(sizes))]
    pad = jnp.zeros((D_MODEL, Z_WIDTH - COL_KIW - IDX_DIM - IDX_HEADS), w.dtype)
    w_r = jnp.concatenate([aq, ak, av, cq, u, qi, ck, cv, ki, wi, pad], axis=1).astype(MXU_DTYPE)
    return dict(
        norm1=norm1_g[l].reshape(1, D_MODEL), w_in=w_r,
        a_gq=a_q_gain[l].reshape(1, HEAD_DIM), a_gk=a_k_gain[l].reshape(1, HEAD_DIM), a_rel=a_rel_bias[l],
        s5=_s5_params(s5_lam_re[l], s5_lam_im[l], s5_log_dt[l], s5_b_re[l], s5_b_im[l], s5_c_re[l], s5_c_im[l]),
        s5_d=s5_d[l].reshape(1, S5_WIDTH), s5_wg=s5_w_glu[l].astype(MXU_DTYPE), s5_bg=s5_b_glu[l].reshape(1, S5_WIDTH),
        c_gq=c_q_gain[l].reshape(1, HEAD_DIM), c_gk=c_k_gain[l].reshape(1, HEAD_DIM),
        w_out=w_out[l].astype(MXU_DTYPE), norm2=norm2_g[l].reshape(1, D_MODEL),
        wg=ffn_w_gate[l].astype(MXU_DTYPE), wu=ffn_w_up[l].astype(MXU_DTYPE), cw=ffn_conv_w[l],
        cb=ffn_conv_b[l].reshape(1, FFN_DIM), wd=ffn_w_down[l].astype(MXU_DTYPE),
    )


def _prompt_layer(x, p, dsa_bias, *, tm, tf, s5_bt, s5_tc):
    b, seq, _ = x.shape
    x2d = x.reshape(b * seq, D_MODEL)
    z3 = _in_proj(x2d, p["norm1"], p["w_in"], tm).reshape(b, seq, Z_WIDTH)
    oa, a_k, a_v = _attn_a(z3, _attn_a_bias(p["a_rel"]), p["a_gq"], p["a_gk"])
    zero_h = jnp.zeros((b, S5_FLAT), F32)
    ob, h_re, h_im = _s5(z3, zero_h, zero_h, p["s5"], p["s5_d"], p["s5_wg"], p["s5_bg"],
                         bt=s5_bt, tc=s5_tc, exact_in=False)
    oc, c_k, c_v, c_ki = _dsa(z3, dsa_bias, p["c_gq"], p["c_gk"])
    y, conv = _out_ffn(x2d, oa.reshape(b * seq, A_WIDTH), ob.reshape(b * seq, S5_WIDTH),
                       oc.reshape(b * seq, C_WIDTH), p["w_out"], p["norm2"], p["wg"], p["wu"], p["cw"], p["cb"],
                       p["wd"], seq=seq, tm=tm, tf=tf)
    keep = a_k.shape[1]
    states = (a_k.reshape(b, keep, A_HEADS, HEAD_DIM), a_v.reshape(b, keep, A_HEADS, HEAD_DIM),
              h_re.reshape(b, S5_GROUPS, S5_STATE), h_im.reshape(b, S5_GROUPS, S5_STATE), c_k, c_v, c_ki, conv)
    return y.reshape(b, seq, D_MODEL), states


def _attn_a_step_kernel(q_ref, k_ref, v_ref, ck_ref, cv_ref, bc_ref, bn_ref, gq_ref, gk_ref, o_ref, ak_ref, av_ref):
    v = v_ref[0]
    av_ref[0] = v
    for h in range(A_HEADS):
        sl = slice(h * HEAD_DIM, (h + 1) * HEAD_DIM)
        qh = (_rms(q_ref[0, :, sl], gq_ref[...]) * HEAD_DIM ** -0.5).astype(MXU_DTYPE)
        khn = _rms(k_ref[0, :, sl], gk_ref[...])
        ak_ref[0, :, sl] = khn
        s_c = _nt_dot(qh, ck_ref[0, :, sl].astype(MXU_DTYPE)) + bc_ref[h]
        s_n = _nt_dot(qh, khn.astype(MXU_DTYPE)) + bn_ref[h]
        m = jnp.maximum(jnp.max(s_c, axis=-1, keepdims=True), jnp.max(s_n, axis=-1, keepdims=True))
        p_c = jnp.exp(s_c - m)
        p_n = jnp.exp(s_n - m)
        denom = jnp.sum(p_c, axis=-1, keepdims=True) + jnp.sum(p_n, axis=-1, keepdims=True)
        oh = (_dot(p_c.astype(MXU_DTYPE), cv_ref[0, :, sl].astype(MXU_DTYPE))
              + _dot(p_n.astype(MXU_DTYPE), v[:, sl].astype(MXU_DTYPE)))
        o_ref[0, :, sl] = oh / denom


def _attn_a_step_bias(rel_table, past, t_new):
    t = jnp.arange(t_new)[:, None]
    rel = jnp.concatenate([jnp.arange(past) - past, jnp.arange(t_new)])[None, :] - t
    bias = rel_table.astype(F32)[:, jnp.clip(rel, -A_MAX_REL, A_MAX_REL) + A_MAX_REL]
    return bias[:, :, :past], bias[:, :, past:]


def _attn_a_step(z3, cache_k, cache_v, bias_c, bias_n, gq, gk):
    b, t_new, _ = z3.shape
    past = cache_k.shape[1]
    new = lambda col: pl.BlockSpec((1, t_new, A_WIDTH), lambda i: (i, 0, col // A_WIDTH))
    const = lambda shape: pl.BlockSpec(shape, lambda i: (0,) * len(shape))
    return pl.pallas_call(
        _attn_a_step_kernel,
        grid=(b,),
        in_specs=[
            new(COL_AQ), new(COL_AK), new(COL_AV),
            pl.BlockSpec((1, past, A_WIDTH), lambda i: (i, 0, 0)),
            pl.BlockSpec((1, past, A_WIDTH), lambda i: (i, 0, 0)),
            const((A_HEADS, t_new, past)), const((A_HEADS, t_new, t_new)),
            const((1, HEAD_DIM)), const((1, HEAD_DIM)),
        ],
        out_specs=[pl.BlockSpec((1, t_new, A_WIDTH), lambda i: (i, 0, 0))] * 3,
        out_shape=[jax.ShapeDtypeStruct((b, t_new, A_WIDTH), F32)] * 3,
        compiler_params=pltpu.CompilerParams(dimension_semantics=("parallel",), vmem_limit_bytes=VMEM_LIMIT),
        name="attn_a_step",
    )(z3, z3, z3, cache_k, cache_v, bias_c, bias_n, gq, gk)


def _dsa_step_kernel(cq_ref, ckv_ref, qi_ref, kiw_ref, ck_ref, cv_ref, cki_ref, gq_ref, gk_ref, bias_ref,
                     o_ref, ck_out, cv_out, cki_out, *, past, t_new, topk):
    n_keys = past + t_new
    kn_new = _rms(ckv_ref[0, :, 0:HEAD_DIM], gk_ref[...])
    cv_new = ckv_ref[0, :, HEAD_DIM:2 * HEAD_DIM]
    ki_new = kiw_ref[0, :, 0:IDX_DIM]
    ck_out[0] = kn_new
    cv_out[0] = cv_new
    cki_out[0] = ki_new
    k_all = jnp.concatenate([ck_ref[0], kn_new], axis=0).astype(MXU_DTYPE)
    v_all = jnp.concatenate([cv_ref[0], cv_new], axis=0).astype(MXU_DTYPE)
    ki_all = jnp.concatenate([cki_ref[0], ki_new], axis=0).astype(MXU_DTYPE)
    eye = (lax.broadcasted_iota(jnp.int32, (HEAD_DIM, HEAD_DIM), 0)
           == lax.broadcasted_iota(jnp.int32, (HEAD_DIM, HEAD_DIM), 1)).astype(MXU_DTYPE)
    v_t = _nt_dot(eye, v_all).astype(MXU_DTYPE)

    qi = qi_ref[0].astype(MXU_DTYPE)
    kiw_pad = jnp.concatenate([kiw_ref[0], jnp.zeros((128 - t_new, 128), F32)], axis=0)
    w_t = kiw_pad.T[:, 0:t_new]
    score = jnp.zeros((n_keys, t_new), F32)
    for h in range(IDX_HEADS):
        dots = _nt_dot(ki_all, qi[:, h * IDX_DIM:(h + 1) * IDX_DIM])
        score = score + (w_t[IDX_DIM + h:IDX_DIM + h + 1, :] * INDEX_SCALE) * jnp.maximum(dots, 0.0)
    pos = lax.broadcasted_iota(jnp.int32, (n_keys, 1), 0)
    q_chunk = (past + lax.broadcasted_iota(jnp.int32, (1, t_new), 1)) // CHUNK
    keys = jnp.where((pos // CHUNK) <= q_chunk, _sortable_key(score), jnp.int32(INT_MIN))

    def count(pred):
        return jnp.sum(jnp.where(pred(keys, pos), 1, 0), axis=0, keepdims=True)

    def kth_body(it, res):
        cand = res | lax.shift_left(jnp.int32(1), (31 - it).astype(jnp.int32))
        thr_c = cand ^ jnp.int32(INT_MIN)
        return jnp.where(count(lambda k, p: k >= thr_c) >= topk, cand, res)

    thr = lax.fori_loop(0, 32, kth_body, jnp.zeros((1, t_new), jnp.int32)) ^ jnp.int32(INT_MIN)
    real_thr = thr != jnp.int32(INT_MIN)
    need = topk - count(lambda k, p: k > thr)
    pos_bits = (n_keys - 1).bit_length()

    def tie_body(it, v):
        cand = v | lax.shift_left(jnp.int32(1), (pos_bits - 1 - it).astype(jnp.int32))
        return jnp.where(count(lambda k, p: (k == thr) & (p < cand)) < need, cand, v)

    tie_last = lax.fori_loop(0, pos_bits, tie_body, jnp.zeros((1, t_new), jnp.int32))
    tie_last = jnp.where(real_thr, tie_last, -1)
    sel = (keys > thr) | ((keys == thr) & (pos <= tie_last))

    cq = cq_ref[0]
    outs = []
    for h in range(C_HEADS):
        qh = (_rms(cq[:, h * HEAD_DIM:(h + 1) * HEAD_DIM], gq_ref[...]) * HEAD_DIM ** -0.5).astype(MXU_DTYPE)
        s = jnp.where(sel, _nt_dot(k_all, qh) + bias_ref[h], NEG_INF)
        p = jnp.where(sel, jnp.exp(s - jnp.max(s, axis=0, keepdims=True)), 0.0)
        outs.append(_dot(v_t, p.astype(MXU_DTYPE)) / jnp.sum(p, axis=0, keepdims=True))
    out_t = jnp.concatenate(outs, axis=0)
    out_pad = jnp.concatenate([out_t, jnp.zeros((C_WIDTH, 128 - t_new), F32)], axis=1)
    o_ref[0] = out_pad.T[0:t_new, :]


def _dsa_step_bias(t5_table, past, t_new):
    s = jnp.arange(past + t_new)[:, None]
    t = jnp.arange(t_new)[None, :]
    return t5_table.astype(F32)[_t5_bucket(s - (past + t))].transpose(2, 0, 1)


def _dsa_step(z3, cache_k, cache_v, cache_ki, bias, gq, gk):
    b, t_new, _ = z3.shape
    past = cache_k.shape[1]
    topk = min(TOPK_MAX, (past + t_new) // 4)
    kern = functools.partial(_dsa_step_kernel, past=past, t_new=t_new, topk=topk)
    new = lambda w, col: pl.BlockSpec((1, t_new, w), lambda i: (i, 0, col // w))
    per_seq = lambda n, w: pl.BlockSpec((1, n, w), lambda i: (i, 0, 0))
    const = lambda shape: pl.BlockSpec(shape, lambda i: (0,) * len(shape))
    return pl.pallas_call(
        kern,
        grid=(b,),
        in_specs=[
            new(C_WIDTH, COL_CQ), new(128, COL_CKV), new(256, COL_QI), new(128, COL_KIW),
            per_seq(past, HEAD_DIM), per_seq(past, HEAD_DIM), per_seq(past, IDX_DIM),
            const((1, HEAD_DIM)), const((1, HEAD_DIM)), const((C_HEADS, past + t_new, t_new)),
        ],
        out_specs=[per_seq(t_new, C_WIDTH), per_seq(t_new, HEAD_DIM), per_seq(t_new, HEAD_DIM),
                   per_seq(t_new, IDX_DIM)],
        out_shape=[
            jax.ShapeDtypeStruct((b, t_new, C_WIDTH), F32),
            jax.ShapeDtypeStruct((b, t_new, HEAD_DIM), F32),
            jax.ShapeDtypeStruct((b, t_new, HEAD_DIM), F32),
            jax.ShapeDtypeStruct((b, t_new, IDX_DIM), F32),
        ],
        compiler_params=pltpu.CompilerParams(dimension_semantics=("parallel",), vmem_limit_bytes=VMEM_LIMIT),
        name="dsa_step",
    )(z3, z3, z3, z3, cache_k, cache_v, cache_ki, gq, gk, bias)


def _out_ffn_step_kernel(x_ref, oa_ref, ob_ref, oc_ref, wo_ref, g2_ref, wg_ref, wu_ref, cw_ref, cb_ref, wd_ref,
                         e1_ref, e2_ref, y_ref, gate_ref, *, t_new, tf):
    tm = x_ref.shape[0]
    x1 = _mix_residual(x_ref, oa_ref, ob_ref, oc_ref, wo_ref)
    xn = _rms(x1, g2_ref[...]).astype(MXU_DTYPE)
    t = lax.broadcasted_iota(jnp.int32, (tm, 1), 0) % t_new
    acc = x1
    for f in range(FFN_DIM // tf):
        sl = slice(f * tf, (f + 1) * tf)
        g = _dot(xn, wg_ref[:, sl])
        gate_ref[:, sl] = g
        g1 = jnp.where(t == 0, e1_ref[:, sl], pltpu.roll(g, 1, 0))
        g2 = jnp.where(t <= 1, e2_ref[:, sl], pltpu.roll(g, 2, 0))
        acc = acc + _ffn_chunk(xn, g, g1, g2, sl, wu_ref, cw_ref, cb_ref, wd_ref)
    y_ref[...] = acc


def _out_ffn_step(x2d, oa, ob, oc, wo, g2, wg, wu, cw, cb, wd, conv_prev, *, t_new, tf):
    m = x2d.shape[0]
    b = m // t_new
    e1 = jnp.zeros((b, t_new, FFN_DIM), F32).at[:, 0].set(conv_prev[:, 1]).reshape(m, FFN_DIM)
    e2 = jnp.zeros((b, t_new, FFN_DIM), F32).at[:, 0].set(conv_prev[:, 0]).at[:, 1].set(conv_prev[:, 1])
    e2 = e2.reshape(m, FFN_DIM)
    kern = functools.partial(_out_ffn_step_kernel, t_new=t_new, tf=tf)
    full = lambda shape: pl.BlockSpec(shape, lambda i: (0,) * len(shape))
    y, gate = pl.pallas_call(
        kern,
        grid=(1,),
        in_specs=[
            full((m, D_MODEL)), full((m, A_WIDTH)), full((m, S5_WIDTH)), full((m, C_WIDTH)),
            full((D_MODEL, D_MODEL)), full((1, D_MODEL)),
            full((D_MODEL, FFN_DIM)), full((D_MODEL, FFN_DIM)),
            full((CONV_W, FFN_DIM)), full((1, FFN_DIM)), full((FFN_DIM, D_MODEL)),
            full((m, FFN_DIM)), full((m, FFN_DIM)),
        ],
        out_specs=[full((m, D_MODEL)), full((m, FFN_DIM))],
        out_shape=[jax.ShapeDtypeStruct((m, D_MODEL), F32), jax.ShapeDtypeStruct((m, FFN_DIM), F32)],
        compiler_params=pltpu.CompilerParams(dimension_semantics=("arbitrary",), vmem_limit_bytes=VMEM_LIMIT),
        name="out_ffn_step",
    )(x2d, oa, ob, oc, wo, g2, wg, wu, cw, cb, wd, e1, e2)
    return y, gate.reshape(b, t_new, FFN_DIM)[:, t_new - (CONV_W - 1):]


def _sample_layer(x, p, dsa_bias, ca_k, ca_v, h_re0, h_im0, cc_k, cc_v, cc_ki, conv_prev, *, tf):
    b, t_new, _ = x.shape
    m = b * t_new
    x2d = x.reshape(m, D_MODEL)
    z3 = _in_proj(x2d, p["norm1"], p["w_in"], m).reshape(b, t_new, Z_WIDTH)
    a_past = ca_k.shape[1]
    bias_c, bias_n = _attn_a_step_bias(p["a_rel"], a_past, t_new)
    oa, a_k, a_v = _attn_a_step(z3, ca_k.reshape(b, a_past, A_WIDTH), ca_v.reshape(b, a_past, A_WIDTH),
                                bias_c, bias_n, p["a_gq"], p["a_gk"])
    ob, h_re, h_im = _s5(z3, h_re0.reshape(b, S5_FLAT), h_im0.reshape(b, S5_FLAT), p["s5"], p["s5_d"], p["s5_wg"],
                         p["s5_bg"], bt=8, tc=t_new, exact_in=True)
    oc, c_k, c_v, c_ki = _dsa_step(z3, cc_k, cc_v, cc_ki, dsa_bias, p["c_gq"], p["c_gk"])
    y, conv = _out_ffn_step(x2d, oa.reshape(m, A_WIDTH), ob.reshape(m, S5_WIDTH), oc.reshape(m, C_WIDTH),
                            p["w_out"], p["norm2"], p["wg"], p["wu"], p["cw"], p["cb"], p["wd"], conv_prev,
                            t_new=t_new, tf=tf)
    states = (a_k.reshape(b, t_new, A_HEADS, HEAD_DIM), a_v.reshape(b, t_new, A_HEADS, HEAD_DIM),
              h_re.reshape(b, S5_GROUPS, S5_STATE), h_im.reshape(b, S5_GROUPS, S5_STATE), c_k, c_v, c_ki, conv)
    return y.reshape(b, t_new, D_MODEL), states


def kernel(x_prompt, x_sample, cache_a_k, cache_a_v, state_s5_re, state_s5_im, cache_c_k, cache_c_v, cache_c_idx_k,
           state_ffn_conv, t5_bias, norm1_g, w_in, a_q_gain, a_k_gain, a_rel_bias, s5_lam_re, s5_lam_im, s5_log_dt,
           s5_b_re, s5_b_im, s5_c_re, s5_c_im, s5_d, s5_w_glu, s5_b_glu, c_q_gain, c_k_gain, w_out, norm2_g,
           ffn_w_gate, ffn_w_up, ffn_conv_w, ffn_conv_b, ffn_w_down):
    depth = w_in.shape[0]
    dsa_bias = _dsa_bias_tiles(t5_bias)
    dsa_step_bias = _dsa_step_bias(t5_bias, cache_c_k.shape[2], x_sample.shape[1])
    yp, ys = x_prompt, x_sample
    prompt_states, sample_states = [], []
    for l in range(depth):
        p = _layer_params(l, t5_bias, norm1_g, w_in, a_q_gain, a_k_gain, a_rel_bias, s5_lam_re, s5_lam_im, s5_log_dt,
                          s5_b_re, s5_b_im, s5_c_re, s5_c_im, s5_d, s5_w_glu, s5_b_glu, c_q_gain, c_k_gain, w_out,
                          norm2_g, ffn_w_gate, ffn_w_up, ffn_conv_w, ffn_conv_b, ffn_w_down)
        yp, st_p = _prompt_layer(yp, p, dsa_bias, tm=512, tf=FFN_DIM, s5_bt=8, s5_tc=256)
        ys, st_s = _sample_layer(ys, p, dsa_step_bias, cache_a_k[l], cache_a_v[l], state_s5_re[l], state_s5_im[l],
                                 cache_c_k[l], cache_c_v[l], cache_c_idx_k[l], state_ffn_conv[l], tf=FFN_DIM)
        prompt_states.append(st_p)
        sample_states.append(st_s)
    (a_k_p, a_v_p, s5_re_p, s5_im_p, c_k_p, c_v_p, c_ki_p, conv_p) = [jnp.stack(z) for z in zip(*prompt_states)]
    (a_k_s, a_v_s, s5_re_s, s5_im_s, c_k_s, c_v_s, c_ki_s, conv_s) = [jnp.stack(z) for z in zip(*sample_states)]
    return (yp, ys, a_k_p, a_v_p, a_k_s, a_v_s, s5_re_p, s5_im_p, s5_re_s, s5_im_s,
            c_k_p, c_v_p, c_ki_p, c_k_s, c_v_s, c_ki_s, conv_p, conv_s)
```

```python
import functools
import math

import jax
import jax.numpy as jnp
from jax import lax
from jax.experimental import pallas as pl
from jax.experimental.pallas import tpu as pltpu

F32 = jnp.float32
MXU_DTYPE = jnp.bfloat16

D_MODEL = 1024
CHUNK = 64
HEAD_DIM = 64
A_HEADS = 6
A_WIDTH = A_HEADS * HEAD_DIM
A_BAND_PAST = 8 * CHUNK
A_MAX_REL = 128
S5_GROUPS = 16
S5_GROUP_CH = 16
S5_WIDTH = S5_GROUPS * S5_GROUP_CH
S5_STATE = 64
S5_FLAT = S5_GROUPS * S5_STATE
C_HEADS = 6
C_WIDTH = C_HEADS * HEAD_DIM
IDX_HEADS = 8
IDX_DIM = 32
INDEX_SCALE = (IDX_HEADS * IDX_DIM) ** -0.5
TOPK_MAX = 256
Q_BLOCK = 128
T5_BUCKETS = 32
T5_MAX_DIST = 128
FFN_DIM = 2816
CONV_W = 3
EPS = 1e-6
NEG_INF = -1e30
INT_MIN = -(2 ** 31)

COL_AQ, COL_AK, COL_AV, COL_CQ = 0, 384, 768, 1152
COL_U, COL_QI, COL_CKV, COL_KIW = 1536, 1792, 2048, 2176
Z_WIDTH = 2304
VMEM_LIMIT = 56 * 1024 * 1024


def _nt_dot(a, b):
    return lax.dot_general(a, b, (((1,), (1,)), ((), ())), preferred_element_type=F32)


def _dot(a, b):
    return jnp.dot(a, b, preferred_element_type=F32)


def _rms(x, g):
    return x * lax.rsqrt(jnp.mean(x * x, axis=-1, keepdims=True) + EPS) * g


def _in_proj_kernel(x_ref, g_ref, w_ref, z_ref):
    xn = _rms(x_ref[...], g_ref[...]).astype(MXU_DTYPE)
    z_ref[...] = _dot(xn, w_ref[...])


def _in_proj(x2d, g, w_r, tm):
    m = x2d.shape[0]
    return pl.pallas_call(
        _in_proj_kernel,
        grid=(m // tm,),
        in_specs=[
            pl.BlockSpec((tm, D_MODEL), lambda i: (i, 0)),
            pl.BlockSpec((1, D_MODEL), lambda i: (0, 0)),
            pl.BlockSpec((D_MODEL, Z_WIDTH), lambda i: (0, 0)),
        ],
        out_specs=pl.BlockSpec((tm, Z_WIDTH), lambda i: (i, 0)),
        out_shape=jax.ShapeDtypeStruct((m, Z_WIDTH), F32),
        compiler_params=pltpu.CompilerParams(dimension_semantics=("parallel",), vmem_limit_bytes=VMEM_LIMIT),
        name="in_proj",
    )(x2d, g, w_r)


A_QB = 2 * CHUNK
A_BAND = A_BAND_PAST + A_QB


def _eye(n, dtype):
    return (lax.broadcasted_iota(jnp.int32, (n, n), 0) == lax.broadcasted_iota(jnp.int32, (n, n), 1)).astype(dtype)


def _attn_a_kernel(q_ref, k_ref, v_ref, bias_ref, gq_ref, gk_ref, o_ref, ak_ref, av_ref, kn_s, vt_s, s_s, p_s,
                   *, seq, keep):
    j = pl.program_id(1)

    @pl.when(j == 0)
    def _prep():
        for h in range(A_HEADS):
            sl = slice(h * HEAD_DIM, (h + 1) * HEAD_DIM)
            khn = _rms(k_ref[0, :, sl], gk_ref[...])
            kn_s[h, 0:A_BAND_PAST, :] = jnp.zeros((A_BAND_PAST, HEAD_DIM), MXU_DTYPE)
            kn_s[h, A_BAND_PAST:A_BAND_PAST + seq, :] = khn.astype(MXU_DTYPE)
            ak_ref[0, :, sl] = khn[seq - keep:, :]
        v = v_ref[0]
        av_ref[0] = v[seq - keep:, :]
        vt_s[:, 0:A_BAND_PAST] = jnp.zeros((A_WIDTH, A_BAND_PAST), MXU_DTYPE)
        vt_s[:, A_BAND_PAST:A_BAND_PAST + seq] = _nt_dot(_eye(A_WIDTH, MXU_DTYPE), v.astype(MXU_DTYPE)).astype(MXU_DTYPE)

    start = pl.multiple_of(j * A_QB, A_QB)
    variant = jnp.minimum(j, A_LEAD_STEPS)
    for h in range(A_HEADS):
        sl = slice(h * HEAD_DIM, (h + 1) * HEAD_DIM)
        qh = (_rms(q_ref[0, :, sl], gq_ref[...]) * HEAD_DIM ** -0.5).astype(MXU_DTYPE)
        s_s[h] = _nt_dot(kn_s[h, pl.ds(start, A_BAND), :], qh) + bias_ref[variant, h]
    denoms = []
    for h in range(A_HEADS):
        s = s_s[h]
        p = jnp.exp(s - jnp.max(s, axis=0, keepdims=True))
        denoms.append(jnp.sum(p, axis=0, keepdims=True))
        p_s[h] = p.astype(MXU_DTYPE)
    outs = [_dot(vt_s[h * HEAD_DIM:(h + 1) * HEAD_DIM, pl.ds(start, A_BAND)], p_s[h]) / denoms[h]
            for h in range(A_HEADS)]
    o_ref[0] = jnp.concatenate(outs, axis=0).T


A_LEAD_STEPS = A_BAND_PAST // A_QB


def _attn_a_bias(rel_table):
    n_ext = A_BAND + A_QB - 1
    m = jnp.arange(n_ext)
    ext = rel_table.astype(F32)[:, jnp.clip(m - (A_QB - 1) - A_BAND_PAST, -A_MAX_REL, A_MAX_REL) + A_MAX_REL]
    rows = jnp.pad(jnp.broadcast_to(ext[:, None, :], (A_HEADS, A_QB, n_ext)), ((0, 0), (0, 0), (0, 1)))
    skew = rows.reshape(A_HEADS, A_QB * (n_ext + 1))[:, :A_QB * n_ext].reshape(A_HEADS, A_QB, n_ext)
    bias = skew[:, :, A_QB - 1:].transpose(0, 2, 1)
    c = jnp.arange(A_BAND)[:, None]
    lo = (jnp.arange(A_QB)[None, :] // CHUNK) * CHUNK
    in_band = (c >= lo) & (c < lo + A_BAND_PAST + CHUNK)
    first_live = (A_LEAD_STEPS - jnp.arange(A_LEAD_STEPS + 1)) * A_QB
    live = in_band[None] & (c[None] >= first_live[:, None, None])
    return jnp.where(live[:, None], bias[None], NEG_INF)


def _attn_a(z3, bias, gq, gk):
    b, seq, _ = z3.shape
    keep = min(A_BAND_PAST, seq)
    kern = functools.partial(_attn_a_kernel, seq=seq, keep=keep)
    return pl.pallas_call(
        kern,
        grid=(b, seq // A_QB),
        in_specs=[
            pl.BlockSpec((1, A_QB, A_WIDTH), lambda i, j: (i, j, COL_AQ // A_WIDTH)),
            pl.BlockSpec((1, seq, A_WIDTH), lambda i, j: (i, 0, COL_AK // A_WIDTH)),
            pl.BlockSpec((1, seq, A_WIDTH), lambda i, j: (i, 0, COL_AV // A_WIDTH)),
            pl.BlockSpec((A_LEAD_STEPS + 1, A_HEADS, A_BAND, A_QB), lambda i, j: (0, 0, 0, 0),
                         pipeline_mode=pl.Buffered(1)),
            pl.BlockSpec((1, HEAD_DIM), lambda i, j: (0, 0)),
            pl.BlockSpec((1, HEAD_DIM), lambda i, j: (0, 0)),
        ],
        out_specs=[
            pl.BlockSpec((1, A_QB, A_WIDTH), lambda i, j: (i, j, 0)),
            pl.BlockSpec((1, keep, A_WIDTH), lambda i, j: (i, 0, 0)),
            pl.BlockSpec((1, keep, A_WIDTH), lambda i, j: (i, 0, 0)),
        ],
        out_shape=[
            jax.ShapeDtypeStruct((b, seq, A_WIDTH), F32),
            jax.ShapeDtypeStruct((b, keep, A_WIDTH), F32),
            jax.ShapeDtypeStruct((b, keep, A_WIDTH), F32),
        ],
        scratch_shapes=[
            pltpu.VMEM((A_HEADS, A_BAND_PAST + seq, HEAD_DIM), MXU_DTYPE),
            pltpu.VMEM((A_WIDTH, A_BAND_PAST + seq), MXU_DTYPE),
            pltpu.VMEM((A_HEADS, A_BAND, A_QB), F32),
            pltpu.VMEM((A_HEADS, A_BAND, A_QB), MXU_DTYPE),
        ],
        compiler_params=pltpu.CompilerParams(
            dimension_semantics=("parallel", "arbitrary"), vmem_limit_bytes=VMEM_LIMIT),
        name="attn_a",
    )(z3, z3, z3, bias, gq, gk)


def _s5_kernel(u_ref, h0r_ref, h0i_ref, ar_ref, ai_ref, wbr_ref, wbi_ref, wcr_ref, wci_ref, d_ref, wg_ref, bg_ref,
               o_ref, hr_out, hi_out, xr_s, xi_s, hr_c, hi_c, *, bt, tc, exact_in):
    c = pl.program_id(1)

    @pl.when(c == 0)
    def _init():
        hr_c[...] = h0r_ref[...]
        hi_c[...] = h0i_ref[...]

    u = jnp.swapaxes(u_ref[...], 0, 1).reshape(tc * bt, S5_WIDTH)
    if exact_in:
        xr_s[...] = jnp.dot(u, wbr_ref[...], preferred_element_type=F32, precision=lax.Precision.HIGHEST)
        xi_s[...] = jnp.dot(u, wbi_ref[...], preferred_element_type=F32, precision=lax.Precision.HIGHEST)
    else:
        ub = u.astype(MXU_DTYPE)
        xr_s[...] = _dot(ub, wbr_ref[...])
        xi_s[...] = _dot(ub, wbi_ref[...])
    ar = ar_ref[...]
    ai = ai_ref[...]

    def step(t, carry):
        hr, hi = carry
        rows = pl.ds(pl.multiple_of(t * bt, bt), bt)
        nhr = ar * hr - ai * hi + xr_s[rows, :]
        nhi = ar * hi + ai * hr + xi_s[rows, :]
        xr_s[rows, :] = nhr
        xi_s[rows, :] = nhi
        return nhr, nhi

    hr, hi = lax.fori_loop(0, tc, step, (hr_c[...], hi_c[...]))
    hr_c[...] = hr
    hi_c[...] = hi
    hr_out[...] = hr
    hi_out[...] = hi
    y = (_dot(xr_s[...].astype(MXU_DTYPE), wcr_ref[...]) - _dot(xi_s[...].astype(MXU_DTYPE), wci_ref[...])
         + d_ref[...] * u)
    g = jax.nn.gelu(y)
    out = g * jax.nn.sigmoid(_dot(g.astype(MXU_DTYPE), wg_ref[...]) + bg_ref[...])
    o_ref[...] = jnp.swapaxes(out.reshape(tc, bt, S5_WIDTH), 0, 1)


def _s5_params(lam_re, lam_im, log_dt, b_re, b_im, c_re, c_im):
    dt = jnp.exp(log_dt.astype(F32))[:, None]
    lr, li = lam_re.astype(F32), lam_im.astype(F32)
    mag = jnp.exp(lr * dt)
    ab_re, ab_im = mag * jnp.cos(li * dt), mag * jnp.sin(li * dt)
    den = lr * lr + li * li
    cr = ((ab_re - 1.0) * lr + ab_im * li) / den
    ci = (ab_im * lr - (ab_re - 1.0) * li) / den
    br, bi = b_re.astype(F32), b_im.astype(F32)
    bb_re = cr[..., None] * br - ci[..., None] * bi
    bb_im = cr[..., None] * bi + ci[..., None] * br
    eye = jnp.eye(S5_GROUPS, dtype=F32)
    wb_re = jnp.einsum("gpc,gh->gchp", bb_re, eye).reshape(S5_WIDTH, S5_FLAT)
    wb_im = jnp.einsum("gpc,gh->gchp", bb_im, eye).reshape(S5_WIDTH, S5_FLAT)
    wc_re = jnp.einsum("gcp,gh->gphc", c_re.astype(F32), eye).reshape(S5_FLAT, S5_WIDTH).astype(MXU_DTYPE)
    wc_im = jnp.einsum("gcp,gh->gphc", c_im.astype(F32), eye).reshape(S5_FLAT, S5_WIDTH).astype(MXU_DTYPE)
    return ab_re.reshape(1, S5_FLAT), ab_im.reshape(1, S5_FLAT), wb_re, wb_im, wc_re, wc_im


def _s5(z3, h0r, h0i, sp, d, wg, bg, *, bt, tc, exact_in):
    b, seq, _ = z3.shape
    ar, ai, wbr, wbi, wcr, wci = sp
    if not exact_in:
        wbr, wbi = wbr.astype(MXU_DTYPE), wbi.astype(MXU_DTYPE)
    kern = functools.partial(_s5_kernel, bt=bt, tc=tc, exact_in=exact_in)
    const = lambda shape: pl.BlockSpec(shape, lambda i, c: (0,) * len(shape))
    return pl.pallas_call(
        kern,
        grid=(b // bt, seq // tc),
        in_specs=[
            pl.BlockSpec((bt, tc, S5_WIDTH), lambda i, c: (i, c, COL_U // S5_WIDTH)),
            pl.BlockSpec((bt, S5_FLAT), lambda i, c: (i, 0)),
            pl.BlockSpec((bt, S5_FLAT), lambda i, c: (i, 0)),
            const((1, S5_FLAT)), const((1, S5_FLAT)),
            const((S5_WIDTH, S5_FLAT)), const((S5_WIDTH, S5_FLAT)),
            const((S5_FLAT, S5_WIDTH)), const((S5_FLAT, S5_WIDTH)),
            const((1, S5_WIDTH)), const((S5_WIDTH, S5_WIDTH)), const((1, S5_WIDTH)),
        ],
        out_specs=[
            pl.BlockSpec((bt, tc, S5_WIDTH), lambda i, c: (i, c, 0)),
            pl.BlockSpec((bt, S5_FLAT), lambda i, c: (i, 0)),
            pl.BlockSpec((bt, S5_FLAT), lambda i, c: (i, 0)),
        ],
        out_shape=[
            jax.ShapeDtypeStruct((b, seq, S5_WIDTH), F32),
            jax.ShapeDtypeStruct((b, S5_FLAT), F32),
            jax.ShapeDtypeStruct((b, S5_FLAT), F32),
        ],
        scratch_shapes=[
            pltpu.VMEM((bt * tc, S5_FLAT), F32), pltpu.VMEM((bt * tc, S5_FLAT), F32),
            pltpu.VMEM((bt, S5_FLAT), F32), pltpu.VMEM((bt, S5_FLAT), F32),
        ],
        compiler_params=pltpu.CompilerParams(
            dimension_semantics=("parallel", "arbitrary"), vmem_limit_bytes=VMEM_LIMIT),
        name="s5",
    )(z3, h0r, h0i, ar, ai, wbr, wbi, wcr, wci, d, wg, bg)


def _sortable_key(score):
    bits = lax.bitcast_convert_type(score + 0.0, jnp.int32)
    return bits ^ ((bits >> 31) & jnp.int32(0x7FFFFFFF))


DSA_TRIP_BLOCKS = 4
DSA_KT = DSA_TRIP_BLOCKS * Q_BLOCK
I16_MIN = -(2 ** 15)


def _fold_rows(x, rows):
    parts = [x[i:i + rows] for i in range(0, x.shape[0], rows)]
    while len(parts) > 1:
        parts = [a + b for a, b in zip(parts[0::2], parts[1::2])] + parts[len(parts) & ~1:]
    return parts[0]


def _bisect16(count_ge, k):
    def body(it, res):
        cand = res | lax.shift_left(jnp.int32(1), jnp.int32(15) - it)
        cnt = count_ge((cand + I16_MIN).astype(jnp.int16))
        return jnp.where(cnt >= k, cand, res)

    return lax.fori_loop(0, 16, body, jnp.zeros((1, Q_BLOCK), jnp.int32))


def _dsa_kernel(cq_ref, ckv_ref, qi_ref, kiwq_ref, kiwk_ref, gq_ref, gk_ref, bias_ref,
                o_ref, ck_out, cv_out, cki_out, kn_s, vt_s, ki_s, keys_s, hi_s, lo_s, m_s, l_s, acc_s, *, seq, topk):
    j = pl.program_id(1)
    nkt = j // DSA_TRIP_BLOCKS + 1

    @pl.when(j == 0)
    def _prep():
        kn = _rms(ckv_ref[0, :, 0:HEAD_DIM], gk_ref[...])
        cv = ckv_ref[0, :, HEAD_DIM:2 * HEAD_DIM]
        ck_out[0] = kn
        cv_out[0] = cv
        kn_s[...] = kn.astype(MXU_DTYPE)
        eye = (lax.broadcasted_iota(jnp.int32, (HEAD_DIM, HEAD_DIM), 0)
               == lax.broadcasted_iota(jnp.int32, (HEAD_DIM, HEAD_DIM), 1)).astype(MXU_DTYPE)
        vt_s[...] = _nt_dot(eye, cv.astype(MXU_DTYPE)).astype(MXU_DTYPE)
        ki = kiwk_ref[0, :, 0:IDX_DIM]
        cki_out[0] = ki
        ki_s[...] = ki.astype(MXU_DTYPE)

    cq = cq_ref[0]
    q_all = jnp.concatenate(
        [(_rms(cq[:, h * HEAD_DIM:(h + 1) * HEAD_DIM], gq_ref[...]) * HEAD_DIM ** -0.5).astype(MXU_DTYPE)
         for h in range(C_HEADS)], axis=0)
    qi = qi_ref[0].astype(MXU_DTYPE)
    qi_all = jnp.concatenate([qi[:, h * IDX_DIM:(h + 1) * IDX_DIM] for h in range(IDX_HEADS)], axis=0)
    w_t = kiwq_ref[0].T
    w_rows = [w_t[IDX_DIM + h:IDX_DIM + h + 1, :] * INDEX_SCALE for h in range(IDX_HEADS)]
    q_chunk = (j * Q_BLOCK + lax.broadcasted_iota(jnp.int32, (1, Q_BLOCK), 1)) // CHUNK
    kt_iota = lax.broadcasted_iota(jnp.int32, (DSA_KT, 1), 0)

    def trip_rows(kt):
        return pl.multiple_of(kt * DSA_KT, DSA_KT)

    def idx_body(kt, _):
        off = trip_rows(kt)
        dots = _nt_dot(ki_s[pl.ds(off, DSA_KT), :], qi_all)
        score = w_rows[0] * jnp.maximum(dots[:, 0:Q_BLOCK], 0.0)
        for h in range(1, IDX_HEADS):
            score = score + w_rows[h] * jnp.maximum(dots[:, h * Q_BLOCK:(h + 1) * Q_BLOCK], 0.0)
        adm = ((off + kt_iota) // CHUNK) <= q_chunk
        key = jnp.where(adm, _sortable_key(score), jnp.int32(INT_MIN))
        keys_s[pl.ds(off, DSA_KT), :] = key
        hi_s[pl.ds(off, DSA_KT), :] = (key >> 16).astype(jnp.int16)
        lo_s[pl.ds(off, DSA_KT), :] = ((key & 0xFFFF) + I16_MIN).astype(jnp.int16)
        return 0

    lax.fori_loop(0, nkt, idx_body, 0)

    def count16(ref, pred):
        def body(kt, acc):
            hit = jnp.where(pred(ref[pl.ds(trip_rows(kt), DSA_KT), :]), jnp.int16(1), jnp.int16(0))
            return acc + _fold_rows(hit, 16)

        acc = lax.fori_loop(0, nkt, body, jnp.zeros((16, Q_BLOCK), jnp.int16))
        return jnp.sum(acc.astype(jnp.int32), axis=0, keepdims=True)

    def count(pred):
        def body(kt, acc):
            off = trip_rows(kt)
            hit = jnp.where(pred(keys_s[pl.ds(off, DSA_KT), :], off + kt_iota), 1, 0)
            return acc + jnp.sum(hit.reshape(DSA_KT // 8, 8, Q_BLOCK), axis=0)

        acc = lax.fori_loop(0, nkt, body, jnp.zeros((8, Q_BLOCK), jnp.int32))
        return jnp.sum(acc, axis=0, keepdims=True)

    hi_u = _bisect16(lambda c: count16(hi_s, lambda v: v >= c), topk)
    thr_hi = (hi_u + I16_MIN).astype(jnp.int16)
    need_lo = topk - count16(hi_s, lambda v: v > thr_hi)

    def mask_lo(kt, _):
        rows = pl.ds(trip_rows(kt), DSA_KT)
        lo_s[rows, :] = jnp.where(hi_s[rows, :] == thr_hi, lo_s[rows, :], jnp.int16(I16_MIN))
        return 0

    lax.fori_loop(0, nkt, mask_lo, 0)
    lo_u = _bisect16(lambda c: count16(lo_s, lambda v: v >= c), need_lo)
    thr = lax.shift_left(hi_u + I16_MIN, 16) | lo_u
    real_thr = thr != jnp.int32(INT_MIN)
    n_tied = count16(lo_s, lambda v: v >= (lo_u + I16_MIN).astype(jnp.int16))
    pos_bits = (seq - 1).bit_length()

    def _tie_search():
        need = topk - count(lambda k, pos: k > thr)

        def body(it, v):
            cand = v | lax.shift_left(jnp.int32(1), jnp.int32(pos_bits - 1) - it)
            return jnp.where(count(lambda k, pos: (k == thr) & (pos < cand)) < need, cand, v)

        return lax.fori_loop(0, pos_bits, body, jnp.zeros((1, Q_BLOCK), jnp.int32))

    has_ties = jnp.max(jnp.where(real_thr & (n_tied > need_lo), 1, 0)) > 0
    tie_last = lax.cond(has_ties, _tie_search, lambda: jnp.full((1, Q_BLOCK), 2 ** pos_bits - 1, jnp.int32))
    tie_last = jnp.where(real_thr, tie_last, -1)

    m_s[...] = jnp.full(m_s.shape, NEG_INF, F32)
    l_s[...] = jnp.zeros(l_s.shape, F32)
    acc_s[...] = jnp.zeros(acc_s.shape, F32)

    def att_body(kt, _):
        off = trip_rows(kt)
        s_all = _nt_dot(kn_s[pl.ds(off, DSA_KT), :], q_all)
        keyc = keys_s[pl.ds(off, DSA_KT), :]
        sel = (keyc > thr) | ((keyc == thr) & (off + kt_iota <= tie_last))
        bidx = [jnp.clip(j - (DSA_TRIP_BLOCKS * kt + t), 0, 2) for t in range(DSA_TRIP_BLOCKS)]
        vtc = vt_s[:, pl.ds(off, DSA_KT)]
        for h in range(C_HEADS):
            ls = slice(h * Q_BLOCK, (h + 1) * Q_BLOCK)
            bias = jnp.concatenate([bias_ref[bi, :, ls] for bi in bidx], axis=0)
            s = jnp.where(sel, s_all[:, ls] + bias, NEG_INF)
            m_old = m_s[h:h + 1, :]
            m_new = jnp.maximum(m_old, jnp.max(s, axis=0, keepdims=True))
            alpha = jnp.exp(m_old - m_new)
            p = jnp.exp(s - m_new)
            l_s[h:h + 1, :] = alpha * l_s[h:h + 1, :] + jnp.sum(p, axis=0, keepdims=True)
            rs = slice(h * HEAD_DIM, (h + 1) * HEAD_DIM)
            acc_s[rs, :] = alpha * acc_s[rs, :] + _dot(vtc, p.astype(MXU_DTYPE))
            m_s[h:h + 1, :] = m_new
        return 0

    lax.fori_loop(0, nkt, att_body, 0)
    out_t = jnp.concatenate(
        [acc_s[h * HEAD_DIM:(h + 1) * HEAD_DIM, :] / l_s[h:h + 1, :] for h in range(C_HEADS)], axis=0)
    o_ref[0] = out_t.T


def _t5_bucket(rel):
    half = T5_BUCKETS // 2
    max_exact = half // 2
    n = jnp.abs(rel)
    log_val = jnp.log(jnp.maximum(n, 1).astype(F32) / max_exact) / math.log(T5_MAX_DIST / max_exact)
    large = jnp.minimum(max_exact + (log_val * (half - max_exact)).astype(jnp.int32), half - 1)
    return jnp.where(rel > 0, half, 0) + jnp.where(n < max_exact, n, large)


def _dsa_bias_tiles(t5_table):
    s = jnp.arange(Q_BLOCK)[:, None]
    t = jnp.arange(Q_BLOCK)[None, :]
    tiles = []
    for d in range(3):
        rel = s - d * Q_BLOCK - t
        tile = t5_table.astype(F32)[_t5_bucket(rel)]
        tiles.append(tile.transpose(0, 2, 1).reshape(Q_BLOCK, C_HEADS * Q_BLOCK))
    return jnp.stack(tiles)


def _dsa(z3, bias, gq, gk):
    b, seq, _ = z3.shape
    assert seq % DSA_KT == 0
    topk = min(TOPK_MAX, seq // 4)
    kern = functools.partial(_dsa_kernel, seq=seq, topk=topk)
    return pl.pallas_call(
        kern,
        grid=(b, seq // Q_BLOCK),
        in_specs=[
            pl.BlockSpec((1, Q_BLOCK, C_WIDTH), lambda i, j: (i, j, COL_CQ // C_WIDTH)),
            pl.BlockSpec((1, seq, 128), lambda i, j: (i, 0, COL_CKV // 128)),
            pl.BlockSpec((1, Q_BLOCK, 256), lambda i, j: (i, j, COL_QI // 256)),
            pl.BlockSpec((1, Q_BLOCK, 128), lambda i, j: (i, j, COL_KIW // 128)),
            pl.BlockSpec((1, seq, 128), lambda i, j: (i, 0, COL_KIW // 128)),
            pl.BlockSpec((1, HEAD_DIM), lambda i, j: (0, 0)),
            pl.BlockSpec((1, HEAD_DIM), lambda i, j: (0, 0)),
            pl.BlockSpec((3, Q_BLOCK, C_HEADS * Q_BLOCK), lambda i, j: (0, 0, 0)),
        ],
        out_specs=[
            pl.BlockSpec((1, Q_BLOCK, C_WIDTH), lambda i, j: (i, j, 0)),
            pl.BlockSpec((1, seq, HEAD_DIM), lambda i, j: (i, 0, 0)),
            pl.BlockSpec((1, seq, HEAD_DIM), lambda i, j: (i, 0, 0)),
            pl.BlockSpec((1, seq, IDX_DIM), lambda i, j: (i, 0, 0)),
        ],
        out_shape=[
            jax.ShapeDtypeStruct((b, seq, C_WIDTH), F32),
            jax.ShapeDtypeStruct((b, seq, HEAD_DIM), F32),
            jax.ShapeDtypeStruct((b, seq, HEAD_DIM), F32),
            jax.ShapeDtypeStruct((b, seq, IDX_DIM), F32),
        ],
        scratch_shapes=[
            pltpu.VMEM((seq, HEAD_DIM), MXU_DTYPE),
            pltpu.VMEM((HEAD_DIM, seq), MXU_DTYPE),
            pltpu.VMEM((seq, IDX_DIM), MXU_DTYPE),
            pltpu.VMEM((seq, Q_BLOCK), jnp.int32),
            pltpu.VMEM((seq, Q_BLOCK), jnp.int16),
            pltpu.VMEM((seq, Q_BLOCK), jnp.int16),
            pltpu.VMEM((8, Q_BLOCK), F32),
            pltpu.VMEM((8, Q_BLOCK), F32),
            pltpu.VMEM((C_WIDTH, Q_BLOCK), F32),
        ],
        compiler_params=pltpu.CompilerParams(
            dimension_semantics=("parallel", "arbitrary"), vmem_limit_bytes=VMEM_LIMIT),
        name="dsa",
    )(z3, z3, z3, z3, z3, gq, gk, bias)


def _mix_residual(x_ref, oa_ref, ob_ref, oc_ref, wo_ref):
    mix = (_dot(oa_ref[...].astype(MXU_DTYPE), wo_ref[0:A_WIDTH, :])
           + _dot(ob_ref[...].astype(MXU_DTYPE), wo_ref[A_WIDTH:A_WIDTH + S5_WIDTH, :])
           + _dot(oc_ref[...].astype(MXU_DTYPE), wo_ref[A_WIDTH + S5_WIDTH:, :]))
    return x_ref[...] + mix


def _ffn_chunk(xn, g, g1, g2, sl, wu_ref, cw_ref, cb_ref, wd_ref):
    up = _dot(xn, wu_ref[:, sl])
    gc = cw_ref[0:1, sl] * g2 + cw_ref[1:2, sl] * g1 + cw_ref[2:3, sl] * g + cb_ref[:, sl]
    return _dot((jax.nn.silu(gc) * up).astype(MXU_DTYPE), wd_ref[sl, :])


def _out_ffn_kernel(x_ref, oa_ref, ob_ref, oc_ref, wo_ref, g2_ref, wg_ref, wu_ref, cw_ref, cb_ref, wd_ref,
                    y_ref, conv_ref, carry_s, *, tiles_per_seq, tf):
    i = pl.program_id(0)
    tm = x_ref.shape[0]
    x1 = _mix_residual(x_ref, oa_ref, ob_ref, oc_ref, wo_ref)
    xn = _rms(x1, g2_ref[...]).astype(MXU_DTYPE)

    @pl.when(i % tiles_per_seq == 0)
    def _seq_start():
        carry_s[...] = jnp.zeros(carry_s.shape, F32)

    row = lax.broadcasted_iota(jnp.int32, (tm, 1), 0)
    acc = x1
    for f in range(FFN_DIM // tf):
        sl = slice(f * tf, (f + 1) * tf)
        g = _dot(xn, wg_ref[:, sl])
        prev2, prev1 = carry_s[0:1, sl], carry_s[1:2, sl]
        g1 = jnp.where(row == 0, prev1, pltpu.roll(g, 1, 0))
        g2 = jnp.where(row == 0, prev2, jnp.where(row == 1, prev1, pltpu.roll(g, 2, 0)))
        acc = acc + _ffn_chunk(xn, g, g1, g2, sl, wu_ref, cw_ref, cb_ref, wd_ref)
        carry_s[0:2, sl] = g[tm - 2:tm, :]
    y_ref[...] = acc
    conv_ref[0] = carry_s[0:2, :]


def _out_ffn(x2d, oa, ob, oc, wo, g2, wg, wu, cw, cb, wd, *, seq, tm, tf):
    m = x2d.shape[0]
    tiles_per_seq = seq // tm
    kern = functools.partial(_out_ffn_kernel, tiles_per_seq=tiles_per_seq, tf=tf)
    row = lambda w: pl.BlockSpec((tm, w), lambda i: (i, 0))
    const = lambda shape: pl.BlockSpec(shape, lambda i: (0,) * len(shape), pipeline_mode=pl.Buffered(1))
    return pl.pallas_call(
        kern,
        grid=(m // tm,),
        in_specs=[
            row(D_MODEL), row(A_WIDTH), row(S5_WIDTH), row(C_WIDTH),
            const((D_MODEL, D_MODEL)), const((1, D_MODEL)),
            const((D_MODEL, FFN_DIM)), const((D_MODEL, FFN_DIM)),
            const((CONV_W, FFN_DIM)), const((1, FFN_DIM)), const((FFN_DIM, D_MODEL)),
        ],
        out_specs=[
            row(D_MODEL),
            pl.BlockSpec((1, CONV_W - 1, FFN_DIM), lambda i: (i // tiles_per_seq, 0, 0)),
        ],
        out_shape=[
            jax.ShapeDtypeStruct((m, D_MODEL), F32),
            jax.ShapeDtypeStruct((m // seq, CONV_W - 1, FFN_DIM), F32),
        ],
        scratch_shapes=[pltpu.VMEM((8, FFN_DIM), F32)],
        compiler_params=pltpu.CompilerParams(dimension_semantics=("arbitrary",), vmem_limit_bytes=VMEM_LIMIT),
        name="out_ffn",
    )(x2d, oa, ob, oc, wo, g2, wg, wu, cw, cb, wd)


def _layer_params(l, t5_bias, norm1_g, w_in, a_q_gain, a_k_gain, a_rel_bias, s5_lam_re, s5_lam_im, s5_log_dt,
                  s5_b_re, s5_b_im, s5_c_re, s5_c_im, s5_d, s5_w_glu, s5_b_glu, c_q_gain, c_k_gain, w_out,
                  norm2_g, ffn_w_gate, ffn_w_up, ffn_conv_w, ffn_conv_b, ffn_w_down):
    w = w_in[l]
    sizes = (A_WIDTH, A_WIDTH, A_WIDTH, S5_WIDTH, C_WIDTH, HEAD_DIM, HEAD_DIM, IDX_HEADS * IDX_DIM, IDX_DIM, IDX_HEADS)
    cuts = [0]
    for s in sizes:
        cuts.append(cuts[-1] + s)
    aq, ak, av, u, cq, ck, cv, qi, ki, wi = [w[:, cuts[n]:cuts[n + 1]] for n in range(len(sizes))]
    pad = jnp.zeros((D_MODEL, Z_WIDTH - COL_KIW - IDX_DIM - IDX_HEADS), w.dtype)
    w_r = jnp.concatenate([aq, ak, av, cq, u, qi, ck, cv, ki, wi, pad], axis=1).astype(MXU_DTYPE)
    return dict(
        norm1=norm1_g[l].reshape(1, D_MODEL), w_in=w_r,
        a_gq=a_q_gain[l].reshape(1, HEAD_DIM), a_gk=a_k_gain[l].reshape(1, HEAD_DIM), a_rel=a_rel_bias[l],
        s5=_s5_params(s5_lam_re[l], s5_lam_im[l], s5_log_dt[l], s5_b_re[l], s5_b_im[l], s5_c_re[l], s5_c_im[l]),
        s5_d=s5_d[l].reshape(1, S5_WIDTH), s5_wg=s5_w_glu[l].astype(MXU_DTYPE), s5_bg=s5_b_glu[l].reshape(1, S5_WIDTH),
        c_gq=c_q_gain[l].reshape(1, HEAD_DIM), c_gk=c_k_gain[l].reshape(1, HEAD_DIM),
        w_out=w_out[l].astype(MXU_DTYPE), norm2=norm2_g[l].reshape(1, D_MODEL),
        wg=ffn_w_gate[l].astype(MXU_DTYPE), wu=ffn_w_up[l].astype(MXU_DTYPE), cw=ffn_conv_w[l],
        cb=ffn_conv_b[l].reshape(1, FFN_DIM), wd=ffn_w_down[l].astype(MXU_DTYPE),
    )


def _prompt_layer(x, p, dsa_bias, *, tm, tf, s5_bt, s5_tc):
    b, seq, _ = x.shape
    x2d = x.reshape(b * seq, D_MODEL)
    z3 = _in_proj(x2d, p["norm1"], p["w_in"], tm).reshape(b, seq, Z_WIDTH)
    oa, a_k, a_v = _attn_a(z3, _attn_a_bias(p["a_rel"]), p["a_gq"], p["a_gk"])
    zero_h = jnp.zeros((b, S5_FLAT), F32)
    ob, h_re, h_im = _s5(z3, zero_h, zero_h, p["s5"], p["s5_d"], p["s5_wg"], p["s5_bg"],
                         bt=s5_bt, tc=s5_tc, exact_in=False)
    oc, c_k, c_v, c_ki = _dsa(z3, dsa_bias, p["c_gq"], p["c_gk"])
    y, conv = _out_ffn(x2d, oa.reshape(b * seq, A_WIDTH), ob.reshape(b * seq, S5_WIDTH),
                       oc.reshape(b * seq, C_WIDTH), p["w_out"], p["norm2"], p["wg"], p["wu"], p["cw"], p["cb"],
                       p["wd"], seq=seq, tm=tm, tf=tf)
    keep = a_k.shape[1]
    states = (a_k.reshape(b, keep, A_HEADS, HEAD_DIM), a_v.reshape(b, keep, A_HEADS, HEAD_DIM),
              h_re.reshape(b, S5_GROUPS, S5_STATE), h_im.reshape(b, S5_GROUPS, S5_STATE), c_k, c_v, c_ki, conv)
    return y.reshape(b, seq, D_MODEL), states


def _attn_a_step_kernel(q_ref, k_ref, v_ref, ck_ref, cv_ref, bc_ref, bn_ref, gq_ref, gk_ref, o_ref, ak_ref, av_ref):
    v = v_ref[0]
    av_ref[0] = v
    for h in range(A_HEADS):
        sl = slice(h * HEAD_DIM, (h + 1) * HEAD_DIM)
        qh = (_rms(q_ref[0, :, sl], gq_ref[...]) * HEAD_DIM ** -0.5).astype(MXU_DTYPE)
        khn = _rms(k_ref[0, :, sl], gk_ref[...])
        ak_ref[0, :, sl] = khn
        s_c = _nt_dot(qh, ck_ref[0, :, sl].astype(MXU_DTYPE)) + bc_ref[h]
        s_n = _nt_dot(qh, khn.astype(MXU_DTYPE)) + bn_ref[h]
        m = jnp.maximum(jnp.max(s_c, axis=-1, keepdims=True), jnp.max(s_n, axis=-1, keepdims=True))
        p_c = jnp.exp(s_c - m)
        p_n = jnp.exp(s_n - m)
        denom = jnp.sum(p_c, axis=-1, keepdims=True) + jnp.sum(p_n, axis=-1, keepdims=True)
        oh = (_dot(p_c.astype(MXU_DTYPE), cv_ref[0, :, sl].astype(MXU_DTYPE))
              + _dot(p_n.astype(MXU_DTYPE), v[:, sl].astype(MXU_DTYPE)))
        o_ref[0, :, sl] = oh / denom


def _attn_a_step_bias(rel_table, past, t_new):
    t = jnp.arange(t_new)[:, None]
    rel = jnp.concatenate([jnp.arange(past) - past, jnp.arange(t_new)])[None, :] - t
    bias = rel_table.astype(F32)[:, jnp.clip(rel, -A_MAX_REL, A_MAX_REL) + A_MAX_REL]
    return bias[:, :, :past], bias[:, :, past:]


def _attn_a_step(z3, cache_k, cache_v, layer, bias_c, bias_n, gq, gk):
    b, t_new, _ = z3.shape
    past = cache_k.shape[2]
    new = lambda col: pl.BlockSpec((1, t_new, A_WIDTH), lambda i: (i, 0, col // A_WIDTH))
    const = lambda shape: pl.BlockSpec(shape, lambda i: (0,) * len(shape))
    return pl.pallas_call(
        _attn_a_step_kernel,
        grid=(b,),
        in_specs=[
            new(COL_AQ), new(COL_AK), new(COL_AV),
            pl.BlockSpec((None, 1, past, A_WIDTH), lambda i: (layer, i, 0, 0)),
            pl.BlockSpec((None, 1, past, A_WIDTH), lambda i: (layer, i, 0, 0)),
            const((A_HEADS, t_new, past)), const((A_HEADS, t_new, t_new)),
            const((1, HEAD_DIM)), const((1, HEAD_DIM)),
        ],
        out_specs=[pl.BlockSpec((1, t_new, A_WIDTH), lambda i: (i, 0, 0))] * 3,
        out_shape=[jax.ShapeDtypeStruct((b, t_new, A_WIDTH), F32)] * 3,
        compiler_params=pltpu.CompilerParams(dimension_semantics=("parallel",), vmem_limit_bytes=VMEM_LIMIT),
        name="attn_a_step",
    )(z3, z3, z3, cache_k, cache_v, bias_c, bias_n, gq, gk)


def _dsa_step_kernel(cq_ref, ckv_ref, qi_ref, kiw_ref, ck_ref, cv_ref, cki_ref, gq_ref, gk_ref, bias_ref,
                     o_ref, ck_out, cv_out, cki_out, *, past, t_new, topk):
    n_keys = past + t_new
    kn_new = _rms(ckv_ref[0, :, 0:HEAD_DIM], gk_ref[...])
    cv_new = ckv_ref[0, :, HEAD_DIM:2 * HEAD_DIM]
    ki_new = kiw_ref[0, :, 0:IDX_DIM]
    ck_out[0] = kn_new
    cv_out[0] = cv_new
    cki_out[0] = ki_new
    k_all = jnp.concatenate([ck_ref[0], kn_new], axis=0).astype(MXU_DTYPE)
    v_all = jnp.concatenate([cv_ref[0], cv_new], axis=0).astype(MXU_DTYPE)
    ki_all = jnp.concatenate([cki_ref[0], ki_new], axis=0).astype(MXU_DTYPE)
    eye = (lax.broadcasted_iota(jnp.int32, (HEAD_DIM, HEAD_DIM), 0)
           == lax.broadcasted_iota(jnp.int32, (HEAD_DIM, HEAD_DIM), 1)).astype(MXU_DTYPE)
    v_t = _nt_dot(eye, v_all).astype(MXU_DTYPE)

    qi = qi_ref[0].astype(MXU_DTYPE)
    kiw_pad = jnp.concatenate([kiw_ref[0], jnp.zeros((128 - t_new, 128), F32)], axis=0)
    w_t = kiw_pad.T[:, 0:t_new]
    score = jnp.zeros((n_keys, t_new), F32)
    for h in range(IDX_HEADS):
        dots = _nt_dot(ki_all, qi[:, h * IDX_DIM:(h + 1) * IDX_DIM])
        score = score + (w_t[IDX_DIM + h:IDX_DIM + h + 1, :] * INDEX_SCALE) * jnp.maximum(dots, 0.0)
    pos = lax.broadcasted_iota(jnp.int32, (n_keys, 1), 0)
    q_chunk = (past + lax.broadcasted_iota(jnp.int32, (1, t_new), 1)) // CHUNK
    keys = jnp.where((pos // CHUNK) <= q_chunk, _sortable_key(score), jnp.int32(INT_MIN))

    def count(pred):
        return jnp.sum(jnp.where(pred(keys, pos), 1, 0), axis=0, keepdims=True)

    def kth_body(it, res):
        cand = res | lax.shift_left(jnp.int32(1), jnp.int32(31) - it)
        thr_c = cand ^ jnp.int32(INT_MIN)
        return jnp.where(count(lambda k, p: k >= thr_c) >= topk, cand, res)

    thr = lax.fori_loop(0, 32, kth_body, jnp.zeros((1, t_new), jnp.int32)) ^ jnp.int32(INT_MIN)
    real_thr = thr != jnp.int32(INT_MIN)
    need = topk - count(lambda k, p: k > thr)
    pos_bits = (n_keys - 1).bit_length()

    def tie_body(it, v):
        cand = v | lax.shift_left(jnp.int32(1), jnp.int32(pos_bits - 1) - it)
        return jnp.where(count(lambda k, p: (k == thr) & (p < cand)) < need, cand, v)

    tie_last = lax.fori_loop(0, pos_bits, tie_body, jnp.zeros((1, t_new), jnp.int32))
    tie_last = jnp.where(real_thr, tie_last, -1)
    sel = (keys > thr) | ((keys == thr) & (pos <= tie_last))

    cq = cq_ref[0]
    outs = []
    for h in range(C_HEADS):
        qh = (_rms(cq[:, h * HEAD_DIM:(h + 1) * HEAD_DIM], gq_ref[...]) * HEAD_DIM ** -0.5).astype(MXU_DTYPE)
        s = jnp.where(sel, _nt_dot(k_all, qh) + bias_ref[h], NEG_INF)
        p = jnp.where(sel, jnp.exp(s - jnp.max(s, axis=0, keepdims=True)), 0.0)
        outs.append(_dot(v_t, p.astype(MXU_DTYPE)) / jnp.sum(p, axis=0, keepdims=True))
    out_t = jnp.concatenate(outs, axis=0)
    out_pad = jnp.concatenate([out_t, jnp.zeros((C_WIDTH, 128 - t_new), F32)], axis=1)
    o_ref[0] = out_pad.T[0:t_new, :]


def _dsa_step_bias(t5_table, past, t_new):
    s = jnp.arange(past + t_new)[:, None]
    t = jnp.arange(t_new)[None, :]
    return t5_table.astype(F32)[_t5_bucket(s - (past + t))].transpose(2, 0, 1)


def _dsa_step(z3, cache_k, cache_v, cache_ki, layer, bias, gq, gk):
    b, t_new, _ = z3.shape
    past = cache_k.shape[2]
    topk = min(TOPK_MAX, (past + t_new) // 4)
    kern = functools.partial(_dsa_step_kernel, past=past, t_new=t_new, topk=topk)
    new = lambda w, col: pl.BlockSpec((1, t_new, w), lambda i: (i, 0, col // w))
    per_seq = lambda n, w: pl.BlockSpec((1, n, w), lambda i: (i, 0, 0))
    cached = lambda w: pl.BlockSpec((None, 1, past, w), lambda i: (layer, i, 0, 0))
    const = lambda shape: pl.BlockSpec(shape, lambda i: (0,) * len(shape))
    return pl.pallas_call(
        kern,
        grid=(b,),
        in_specs=[
            new(C_WIDTH, COL_CQ), new(128, COL_CKV), new(256, COL_QI), new(128, COL_KIW),
            cached(HEAD_DIM), cached(HEAD_DIM), cached(IDX_DIM),
            const((1, HEAD_DIM)), const((1, HEAD_DIM)), const((C_HEADS, past + t_new, t_new)),
        ],
        out_specs=[per_seq(t_new, C_WIDTH), per_seq(t_new, HEAD_DIM), per_seq(t_new, HEAD_DIM),
                   per_seq(t_new, IDX_DIM)],
        out_shape=[
            jax.ShapeDtypeStruct((b, t_new, C_WIDTH), F32),
            jax.ShapeDtypeStruct((b, t_new, HEAD_DIM), F32),
            jax.ShapeDtypeStruct((b, t_new, HEAD_DIM), F32),
            jax.ShapeDtypeStruct((b, t_new, IDX_DIM), F32),
        ],
        compiler_params=pltpu.CompilerParams(dimension_semantics=("parallel",), vmem_limit_bytes=VMEM_LIMIT),
        name="dsa_step",
    )(z3, z3, z3, z3, cache_k, cache_v, cache_ki, gq, gk, bias)


def _out_ffn_step_kernel(x_ref, oa_ref, ob_ref, oc_ref, wo_ref, g2_ref, wg_ref, wu_ref, cw_ref, cb_ref, wd_ref,
                         e1_ref, e2_ref, y_ref, gate_ref, *, t_new, tf):
    tm = x_ref.shape[0]
    x1 = _mix_residual(x_ref, oa_ref, ob_ref, oc_ref, wo_ref)
    xn = _rms(x1, g2_ref[...]).astype(MXU_DTYPE)
    t = lax.broadcasted_iota(jnp.int32, (tm, 1), 0) % t_new
    acc = x1
    for f in range(FFN_DIM // tf):
        sl = slice(f * tf, (f + 1) * tf)
        g = _dot(xn, wg_ref[:, sl])
        gate_ref[:, sl] = g
        g1 = jnp.where(t == 0, e1_ref[:, sl], pltpu.roll(g, 1, 0))
        g2 = jnp.where(t <= 1, e2_ref[:, sl], pltpu.roll(g, 2, 0))
        acc = acc + _ffn_chunk(xn, g, g1, g2, sl, wu_ref, cw_ref, cb_ref, wd_ref)
    y_ref[...] = acc


def _out_ffn_step(x2d, oa, ob, oc, wo, g2, wg, wu, cw, cb, wd, conv_prev, *, t_new, tf):
    m = x2d.shape[0]
    b = m // t_new
    e1 = jnp.zeros((b, t_new, FFN_DIM), F32).at[:, 0].set(conv_prev[:, 1]).reshape(m, FFN_DIM)
    e2 = jnp.zeros((b, t_new, FFN_DIM), F32).at[:, 0].set(conv_prev[:, 0]).at[:, 1].set(conv_prev[:, 1])
    e2 = e2.reshape(m, FFN_DIM)
    kern = functools.partial(_out_ffn_step_kernel, t_new=t_new, tf=tf)
    full = lambda shape: pl.BlockSpec(shape, lambda i: (0,) * len(shape))
    y, gate = pl.pallas_call(
        kern,
        grid=(1,),
        in_specs=[
            full((m, D_MODEL)), full((m, A_WIDTH)), full((m, S5_WIDTH)), full((m, C_WIDTH)),
            full((D_MODEL, D_MODEL)), full((1, D_MODEL)),
            full((D_MODEL, FFN_DIM)), full((D_MODEL, FFN_DIM)),
            full((CONV_W, FFN_DIM)), full((1, FFN_DIM)), full((FFN_DIM, D_MODEL)),
            full((m, FFN_DIM)), full((m, FFN_DIM)),
        ],
        out_specs=[full((m, D_MODEL)), full((m, FFN_DIM))],
        out_shape=[jax.ShapeDtypeStruct((m, D_MODEL), F32), jax.ShapeDtypeStruct((m, FFN_DIM), F32)],
        compiler_params=pltpu.CompilerParams(dimension_semantics=("arbitrary",), vmem_limit_bytes=VMEM_LIMIT),
        name="out_ffn_step",
    )(x2d, oa, ob, oc, wo, g2, wg, wu, cw, cb, wd, e1, e2)
    return y, gate.reshape(b, t_new, FFN_DIM)[:, t_new - (CONV_W - 1):]


def _sample_layer(x, p, dsa_bias, layer, ca_k, ca_v, h_re0, h_im0, cc_k, cc_v, cc_ki, conv_prev, *, tf):
    b, t_new, _ = x.shape
    m = b * t_new
    x2d = x.reshape(m, D_MODEL)
    z3 = _in_proj(x2d, p["norm1"], p["w_in"], m).reshape(b, t_new, Z_WIDTH)
    depth, _, a_past = ca_k.shape[:3]
    bias_c, bias_n = _attn_a_step_bias(p["a_rel"], a_past, t_new)
    oa, a_k, a_v = _attn_a_step(z3, ca_k.reshape(depth, b, a_past, A_WIDTH), ca_v.reshape(depth, b, a_past, A_WIDTH),
                                layer, bias_c, bias_n, p["a_gq"], p["a_gk"])
    ob, h_re, h_im = _s5(z3, h_re0.reshape(b, S5_FLAT), h_im0.reshape(b, S5_FLAT), p["s5"], p["s5_d"], p["s5_wg"],
                         p["s5_bg"], bt=8, tc=t_new, exact_in=True)
    oc, c_k, c_v, c_ki = _dsa_step(z3, cc_k, cc_v, cc_ki, layer, dsa_bias, p["c_gq"], p["c_gk"])
    y, conv = _out_ffn_step(x2d, oa.reshape(m, A_WIDTH), ob.reshape(m, S5_WIDTH), oc.reshape(m, C_WIDTH),
                            p["w_out"], p["norm2"], p["wg"], p["wu"], p["cw"], p["cb"], p["wd"], conv_prev,
                            t_new=t_new, tf=tf)
    states = (a_k.reshape(b, t_new, A_HEADS, HEAD_DIM), a_v.reshape(b, t_new, A_HEADS, HEAD_DIM),
              h_re.reshape(b, S5_GROUPS, S5_STATE), h_im.reshape(b, S5_GROUPS, S5_STATE), c_k, c_v, c_ki, conv)
    return y.reshape(b, t_new, D_MODEL), states


def kernel(x_prompt, x_sample, cache_a_k, cache_a_v, state_s5_re, state_s5_im, cache_c_k, cache_c_v, cache_c_idx_k,
           state_ffn_conv, t5_bias, norm1_g, w_in, a_q_gain, a_k_gain, a_rel_bias, s5_lam_re, s5_lam_im, s5_log_dt,
           s5_b_re, s5_b_im, s5_c_re, s5_c_im, s5_d, s5_w_glu, s5_b_glu, c_q_gain, c_k_gain, w_out, norm2_g,
           ffn_w_gate, ffn_w_up, ffn_conv_w, ffn_conv_b, ffn_w_down):
    depth = w_in.shape[0]
    dsa_bias = _dsa_bias_tiles(t5_bias)
    dsa_step_bias = _dsa_step_bias(t5_bias, cache_c_k.shape[2], x_sample.shape[1])
    yp, ys = x_prompt, x_sample
    prompt_states, sample_states = [], []
    for l in range(depth):
        p = _layer_params(l, t5_bias, norm1_g, w_in, a_q_gain, a_k_gain, a_rel_bias, s5_lam_re, s5_lam_im, s5_log_dt,
                          s5_b_re, s5_b_im, s5_c_re, s5_c_im, s5_d, s5_w_glu, s5_b_glu, c_q_gain, c_k_gain, w_out,
                          norm2_g, ffn_w_gate, ffn_w_up, ffn_conv_w, ffn_conv_b, ffn_w_down)
        yp, st_p = _prompt_layer(yp, p, dsa_bias, tm=512, tf=FFN_DIM, s5_bt=8, s5_tc=256)
        ys, st_s = _sample_layer(ys, p, dsa_step_bias, l, cache_a_k, cache_a_v, state_s5_re[l], state_s5_im[l],
                                 cache_c_k, cache_c_v, cache_c_idx_k, state_ffn_conv[l], tf=FFN_DIM)
        prompt_states.append(st_p)
        sample_states.append(st_s)
    (a_k_p, a_v_p, s5_re_p, s5_im_p, c_k_p, c_v_p, c_ki_p, conv_p) = [jnp.stack(z) for z in zip(*prompt_states)]
    (a_k_s, a_v_s, s5_re_s, s5_im_s, c_k_s, c_v_s, c_ki_s, conv_s) = [jnp.stack(z) for z in zip(*sample_states)]
    return (yp, ys, a_k_p, a_v_p, a_k_s, a_v_s, s5_re_p, s5_im_p, s5_re_s, s5_im_s,
            c_k_p, c_v_p, c_ki_p, c_k_s, c_v_s, c_ki_s, conv_p, conv_s)
```

```python
import functools
import math

import jax
import jax.numpy as jnp
from jax import lax
from jax.experimental import pallas as pl
from jax.experimental.pallas import tpu as pltpu

F32 = jnp.float32
MXU_DTYPE = jnp.bfloat16

D_MODEL = 1024
CHUNK = 64
HEAD_DIM = 64
A_HEADS = 6
A_WIDTH = A_HEADS * HEAD_DIM
A_BAND_PAST = 8 * CHUNK
A_MAX_REL = 128
S5_GROUPS = 16
S5_GROUP_CH = 16
S5_WIDTH = S5_GROUPS * S5_GROUP_CH
S5_STATE = 64
S5_FLAT = S5_GROUPS * S5_STATE
C_HEADS = 6
C_WIDTH = C_HEADS * HEAD_DIM
IDX_HEADS = 8
IDX_DIM = 32
INDEX_SCALE = (IDX_HEADS * IDX_DIM) ** -0.5
TOPK_MAX = 256
Q_BLOCK = 128
T5_BUCKETS = 32
T5_MAX_DIST = 128
FFN_DIM = 2816
CONV_W = 3
EPS = 1e-6
NEG_INF = -1e30
INT_MIN = -(2 ** 31)

COL_AQ, COL_AK, COL_AV, COL_CQ = 0, 384, 768, 1152
COL_U, COL_QI, COL_CKV, COL_KIW = 1536, 1792, 2048, 2176
Z_WIDTH = 2304
VMEM_LIMIT = 56 * 1024 * 1024


def _nt_dot(a, b):
    return lax.dot_general(a, b, (((1,), (1,)), ((), ())), preferred_element_type=F32)


def _dot(a, b):
    return jnp.dot(a, b, preferred_element_type=F32)


def _rms(x, g):
    return x * lax.rsqrt(jnp.mean(x * x, axis=-1, keepdims=True) + EPS) * g


def _in_proj_kernel(x_ref, g_ref, w_ref, z_ref):
    xn = _rms(x_ref[...], g_ref[...]).astype(MXU_DTYPE)
    z_ref[...] = _dot(xn, w_ref[...])


def _in_proj(x2d, g, w_r, tm):
    m = x2d.shape[0]
    return pl.pallas_call(
        _in_proj_kernel,
        grid=(m // tm,),
        in_specs=[
            pl.BlockSpec((tm, D_MODEL), lambda i: (i, 0)),
            pl.BlockSpec((1, D_MODEL), lambda i: (0, 0)),
            pl.BlockSpec((D_MODEL, Z_WIDTH), lambda i: (0, 0)),
        ],
        out_specs=pl.BlockSpec((tm, Z_WIDTH), lambda i: (i, 0)),
        out_shape=jax.ShapeDtypeStruct((m, Z_WIDTH), F32),
        compiler_params=pltpu.CompilerParams(dimension_semantics=("parallel",), vmem_limit_bytes=VMEM_LIMIT),
        name="in_proj",
    )(x2d, g, w_r)


A_QB = 2 * CHUNK
A_BAND = A_BAND_PAST + A_QB


def _eye(n, dtype):
    return (lax.broadcasted_iota(jnp.int32, (n, n), 0) == lax.broadcasted_iota(jnp.int32, (n, n), 1)).astype(dtype)


def _attn_a_kernel(q_ref, k_ref, v_ref, bias_ref, gq_ref, gk_ref, o_ref, ak_ref, av_ref, kn_s, vt_s, s_s, p_s,
                   *, seq, keep):
    j = pl.program_id(1)

    @pl.when(j == 0)
    def _prep():
        for h in range(A_HEADS):
            sl = slice(h * HEAD_DIM, (h + 1) * HEAD_DIM)
            khn = _rms(k_ref[0, :, sl], gk_ref[...])
            kn_s[h, 0:A_BAND_PAST, :] = jnp.zeros((A_BAND_PAST, HEAD_DIM), MXU_DTYPE)
            kn_s[h, A_BAND_PAST:A_BAND_PAST + seq, :] = khn.astype(MXU_DTYPE)
            ak_ref[0, :, sl] = khn[seq - keep:, :]
        v = v_ref[0]
        av_ref[0] = v[seq - keep:, :]
        vt_s[:, 0:A_BAND_PAST] = jnp.zeros((A_WIDTH, A_BAND_PAST), MXU_DTYPE)
        vt_s[:, A_BAND_PAST:A_BAND_PAST + seq] = _nt_dot(_eye(A_WIDTH, MXU_DTYPE), v.astype(MXU_DTYPE)).astype(MXU_DTYPE)

    start = pl.multiple_of(j * A_QB, A_QB)
    variant = jnp.minimum(j, A_LEAD_STEPS)
    for h in range(A_HEADS):
        sl = slice(h * HEAD_DIM, (h + 1) * HEAD_DIM)
        qh = (_rms(q_ref[0, :, sl], gq_ref[...]) * HEAD_DIM ** -0.5).astype(MXU_DTYPE)
        s_s[h] = _nt_dot(kn_s[h, pl.ds(start, A_BAND), :], qh) + bias_ref[variant, h]
    denoms = []
    for h in range(A_HEADS):
        s = s_s[h]
        p = jnp.exp(s - jnp.max(s, axis=0, keepdims=True))
        denoms.append(jnp.sum(p, axis=0, keepdims=True))
        p_s[h] = p.astype(MXU_DTYPE)
    outs = [_dot(vt_s[h * HEAD_DIM:(h + 1) * HEAD_DIM, pl.ds(start, A_BAND)], p_s[h]) / denoms[h]
            for h in range(A_HEADS)]
    o_ref[0] = jnp.concatenate(outs, axis=0).T


A_LEAD_STEPS = A_BAND_PAST // A_QB


def _attn_a_bias(rel_table):
    n_ext = A_BAND + A_QB - 1
    m = jnp.arange(n_ext)
    ext = rel_table.astype(F32)[:, jnp.clip(m - (A_QB - 1) - A_BAND_PAST, -A_MAX_REL, A_MAX_REL) + A_MAX_REL]
    rows = jnp.pad(jnp.broadcast_to(ext[:, None, :], (A_HEADS, A_QB, n_ext)), ((0, 0), (0, 0), (0, 1)))
    skew = rows.reshape(A_HEADS, A_QB * (n_ext + 1))[:, :A_QB * n_ext].reshape(A_HEADS, A_QB, n_ext)
    bias = skew[:, :, A_QB - 1:].transpose(0, 2, 1)
    c = jnp.arange(A_BAND)[:, None]
    lo = (jnp.arange(A_QB)[None, :] // CHUNK) * CHUNK
    in_band = (c >= lo) & (c < lo + A_BAND_PAST + CHUNK)
    first_live = (A_LEAD_STEPS - jnp.arange(A_LEAD_STEPS + 1)) * A_QB
    live = in_band[None] & (c[None] >= first_live[:, None, None])
    return jnp.where(live[:, None], bias[None], NEG_INF)


def _attn_a(z3, bias, gq, gk):
    b, seq, _ = z3.shape
    keep = min(A_BAND_PAST, seq)
    kern = functools.partial(_attn_a_kernel, seq=seq, keep=keep)
    return pl.pallas_call(
        kern,
        grid=(b, seq // A_QB),
        in_specs=[
            pl.BlockSpec((1, A_QB, A_WIDTH), lambda i, j: (i, j, COL_AQ // A_WIDTH)),
            pl.BlockSpec((1, seq, A_WIDTH), lambda i, j: (i, 0, COL_AK // A_WIDTH)),
            pl.BlockSpec((1, seq, A_WIDTH), lambda i, j: (i, 0, COL_AV // A_WIDTH)),
            pl.BlockSpec((A_LEAD_STEPS + 1, A_HEADS, A_BAND, A_QB), lambda i, j: (0, 0, 0, 0),
                         pipeline_mode=pl.Buffered(1)),
            pl.BlockSpec((1, HEAD_DIM), lambda i, j: (0, 0)),
            pl.BlockSpec((1, HEAD_DIM), lambda i, j: (0, 0)),
        ],
        out_specs=[
            pl.BlockSpec((1, A_QB, A_WIDTH), lambda i, j: (i, j, 0)),
            pl.BlockSpec((1, keep, A_WIDTH), lambda i, j: (i, 0, 0)),
            pl.BlockSpec((1, keep, A_WIDTH), lambda i, j: (i, 0, 0)),
        ],
        out_shape=[
            jax.ShapeDtypeStruct((b, seq, A_WIDTH), F32),
            jax.ShapeDtypeStruct((b, keep, A_WIDTH), F32),
            jax.ShapeDtypeStruct((b, keep, A_WIDTH), F32),
        ],
        scratch_shapes=[
            pltpu.VMEM((A_HEADS, A_BAND_PAST + seq, HEAD_DIM), MXU_DTYPE),
            pltpu.VMEM((A_WIDTH, A_BAND_PAST + seq), MXU_DTYPE),
            pltpu.VMEM((A_HEADS, A_BAND, A_QB), F32),
            pltpu.VMEM((A_HEADS, A_BAND, A_QB), MXU_DTYPE),
        ],
        compiler_params=pltpu.CompilerParams(
            dimension_semantics=("parallel", "arbitrary"), vmem_limit_bytes=VMEM_LIMIT),
        name="attn_a",
    )(z3, z3, z3, bias, gq, gk)


def _s5_kernel(u_ref, h0r_ref, h0i_ref, ar_ref, ai_ref, wbr_ref, wbi_ref, wcr_ref, wci_ref, d_ref, wg_ref, bg_ref,
               o_ref, hr_out, hi_out, xr_s, xi_s, hr_c, hi_c, *, bt, tc, exact_in):
    c = pl.program_id(1)

    @pl.when(c == 0)
    def _init():
        hr_c[...] = h0r_ref[...]
        hi_c[...] = h0i_ref[...]

    u = jnp.swapaxes(u_ref[...], 0, 1).reshape(tc * bt, S5_WIDTH)
    if exact_in:
        xr_s[...] = jnp.dot(u, wbr_ref[...], preferred_element_type=F32, precision=lax.Precision.HIGHEST)
        xi_s[...] = jnp.dot(u, wbi_ref[...], preferred_element_type=F32, precision=lax.Precision.HIGHEST)
    else:
        ub = u.astype(MXU_DTYPE)
        xr_s[...] = _dot(ub, wbr_ref[...])
        xi_s[...] = _dot(ub, wbi_ref[...])
    ar = ar_ref[...]
    ai = ai_ref[...]

    def step(t, carry):
        hr, hi = carry
        rows = pl.ds(pl.multiple_of(t * bt, bt), bt)
        nhr = ar * hr - ai * hi + xr_s[rows, :]
        nhi = ar * hi + ai * hr + xi_s[rows, :]
        xr_s[rows, :] = nhr
        xi_s[rows, :] = nhi
        return nhr, nhi

    hr, hi = lax.fori_loop(0, tc, step, (hr_c[...], hi_c[...]))
    hr_c[...] = hr
    hi_c[...] = hi
    hr_out[...] = hr
    hi_out[...] = hi
    y = (_dot(xr_s[...].astype(MXU_DTYPE), wcr_ref[...]) - _dot(xi_s[...].astype(MXU_DTYPE), wci_ref[...])
         + d_ref[...] * u)
    g = jax.nn.gelu(y)
    out = g * jax.nn.sigmoid(_dot(g.astype(MXU_DTYPE), wg_ref[...]) + bg_ref[...])
    o_ref[...] = jnp.swapaxes(out.reshape(tc, bt, S5_WIDTH), 0, 1)


def _s5_params(lam_re, lam_im, log_dt, b_re, b_im, c_re, c_im):
    dt = jnp.exp(log_dt.astype(F32))[:, None]
    lr, li = lam_re.astype(F32), lam_im.astype(F32)
    mag = jnp.exp(lr * dt)
    ab_re, ab_im = mag * jnp.cos(li * dt), mag * jnp.sin(li * dt)
    den = lr * lr + li * li
    cr = ((ab_re - 1.0) * lr + ab_im * li) / den
    ci = (ab_im * lr - (ab_re - 1.0) * li) / den
    br, bi = b_re.astype(F32), b_im.astype(F32)
    bb_re = cr[..., None] * br - ci[..., None] * bi
    bb_im = cr[..., None] * bi + ci[..., None] * br
    eye = jnp.eye(S5_GROUPS, dtype=F32)
    wb_re = jnp.einsum("gpc,gh->gchp", bb_re, eye).reshape(S5_WIDTH, S5_FLAT)
    wb_im = jnp.einsum("gpc,gh->gchp", bb_im, eye).reshape(S5_WIDTH, S5_FLAT)
    wc_re = jnp.einsum("gcp,gh->gphc", c_re.astype(F32), eye).reshape(S5_FLAT, S5_WIDTH).astype(MXU_DTYPE)
    wc_im = jnp.einsum("gcp,gh->gphc", c_im.astype(F32), eye).reshape(S5_FLAT, S5_WIDTH).astype(MXU_DTYPE)
    return ab_re.reshape(1, S5_FLAT), ab_im.reshape(1, S5_FLAT), wb_re, wb_im, wc_re, wc_im


def _s5(z3, h0r, h0i, sp, d, wg, bg, *, bt, tc, exact_in):
    b, seq, _ = z3.shape
    ar, ai, wbr, wbi, wcr, wci = sp
    if not exact_in:
        wbr, wbi = wbr.astype(MXU_DTYPE), wbi.astype(MXU_DTYPE)
    kern = functools.partial(_s5_kernel, bt=bt, tc=tc, exact_in=exact_in)
    const = lambda shape: pl.BlockSpec(shape, lambda i, c: (0,) * len(shape))
    return pl.pallas_call(
        kern,
        grid=(b // bt, seq // tc),
        in_specs=[
            pl.BlockSpec((bt, tc, S5_WIDTH), lambda i, c: (i, c, COL_U // S5_WIDTH)),
            pl.BlockSpec((bt, S5_FLAT), lambda i, c: (i, 0)),
            pl.BlockSpec((bt, S5_FLAT), lambda i, c: (i, 0)),
            const((1, S5_FLAT)), const((1, S5_FLAT)),
            const((S5_WIDTH, S5_FLAT)), const((S5_WIDTH, S5_FLAT)),
            const((S5_FLAT, S5_WIDTH)), const((S5_FLAT, S5_WIDTH)),
            const((1, S5_WIDTH)), const((S5_WIDTH, S5_WIDTH)), const((1, S5_WIDTH)),
        ],
        out_specs=[
            pl.BlockSpec((bt, tc, S5_WIDTH), lambda i, c: (i, c, 0)),
            pl.BlockSpec((bt, S5_FLAT), lambda i, c: (i, 0)),
            pl.BlockSpec((bt, S5_FLAT), lambda i, c: (i, 0)),
        ],
        out_shape=[
            jax.ShapeDtypeStruct((b, seq, S5_WIDTH), F32),
            jax.ShapeDtypeStruct((b, S5_FLAT), F32),
            jax.ShapeDtypeStruct((b, S5_FLAT), F32),
        ],
        scratch_shapes=[
            pltpu.VMEM((bt * tc, S5_FLAT), F32), pltpu.VMEM((bt * tc, S5_FLAT), F32),
            pltpu.VMEM((bt, S5_FLAT), F32), pltpu.VMEM((bt, S5_FLAT), F32),
        ],
        compiler_params=pltpu.CompilerParams(
            dimension_semantics=("parallel", "arbitrary"), vmem_limit_bytes=VMEM_LIMIT),
        name="s5",
    )(z3, h0r, h0i, ar, ai, wbr, wbi, wcr, wci, d, wg, bg)


def _sortable_key(score):
    bits = lax.bitcast_convert_type(score + 0.0, jnp.int32)
    return bits ^ ((bits >> 31) & jnp.int32(0x7FFFFFFF))


DSA_TRIP_BLOCKS = 4
DSA_KT = DSA_TRIP_BLOCKS * Q_BLOCK
I16_MIN = -(2 ** 15)


def _fold_rows(x, rows):
    parts = [x[i:i + rows] for i in range(0, x.shape[0], rows)]
    while len(parts) > 1:
        parts = [a + b for a, b in zip(parts[0::2], parts[1::2])] + parts[len(parts) & ~1:]
    return parts[0]


def _bisect16(count_ge, k):
    def body(it, res):
        cand = res | lax.shift_left(jnp.int32(1), jnp.int32(15) - it)
        cnt = count_ge((cand + I16_MIN).astype(jnp.int16))
        return jnp.where(cnt >= k, cand, res)

    return lax.fori_loop(0, 16, body, jnp.zeros((1, Q_BLOCK), jnp.int32))


def _dsa_kernel(cq_ref, ckv_ref, qi_ref, kiwq_ref, kiwk_ref, gq_ref, gk_ref, bias_ref,
                o_ref, ck_out, cv_out, cki_out, kn_s, vt_s, ki_s, keys_s, hi_s, lo_s, negm_s, m_s, l_s, acc_s,
                *, seq, topk):
    j = pl.program_id(1)
    nkt = j // DSA_TRIP_BLOCKS + 1

    @pl.when(j == 0)
    def _prep():
        kn = _rms(ckv_ref[0, :, 0:HEAD_DIM], gk_ref[...])
        cv = ckv_ref[0, :, HEAD_DIM:2 * HEAD_DIM]
        ck_out[0] = kn
        cv_out[0] = cv
        kn_s[...] = kn.astype(MXU_DTYPE)
        eye = (lax.broadcasted_iota(jnp.int32, (HEAD_DIM, HEAD_DIM), 0)
               == lax.broadcasted_iota(jnp.int32, (HEAD_DIM, HEAD_DIM), 1)).astype(MXU_DTYPE)
        vt_s[...] = _nt_dot(eye, cv.astype(MXU_DTYPE)).astype(MXU_DTYPE)
        ki = kiwk_ref[0, :, 0:IDX_DIM]
        cki_out[0] = ki
        ki_s[...] = ki.astype(MXU_DTYPE)

    cq = cq_ref[0]
    q_all = jnp.concatenate(
        [(_rms(cq[:, h * HEAD_DIM:(h + 1) * HEAD_DIM], gq_ref[...]) * HEAD_DIM ** -0.5).astype(MXU_DTYPE)
         for h in range(C_HEADS)], axis=0)
    qi = qi_ref[0].astype(MXU_DTYPE)
    qi_all = jnp.concatenate([qi[:, h * IDX_DIM:(h + 1) * IDX_DIM] for h in range(IDX_HEADS)], axis=0)
    w_t = kiwq_ref[0].T
    w_rows = [w_t[IDX_DIM + h:IDX_DIM + h + 1, :] * INDEX_SCALE for h in range(IDX_HEADS)]
    q_chunk = (j * Q_BLOCK + lax.broadcasted_iota(jnp.int32, (1, Q_BLOCK), 1)) // CHUNK
    kt_iota = lax.broadcasted_iota(jnp.int32, (DSA_KT, 1), 0)

    def trip_rows(kt):
        return pl.multiple_of(kt * DSA_KT, DSA_KT)

    def idx_body(kt, _):
        off = trip_rows(kt)
        dots = _nt_dot(ki_s[pl.ds(off, DSA_KT), :], qi_all)
        score = w_rows[0] * jnp.maximum(dots[:, 0:Q_BLOCK], 0.0)
        for h in range(1, IDX_HEADS):
            score = score + w_rows[h] * jnp.maximum(dots[:, h * Q_BLOCK:(h + 1) * Q_BLOCK], 0.0)
        adm = ((off + kt_iota) // CHUNK) <= q_chunk
        key = jnp.where(adm, _sortable_key(score), jnp.int32(INT_MIN))
        keys_s[pl.ds(off, DSA_KT), :] = key
        hi_s[pl.ds(off, DSA_KT), :] = (key >> 16).astype(jnp.int16)
        lo_s[pl.ds(off, DSA_KT), :] = ((key & 0xFFFF) + I16_MIN).astype(jnp.int16)
        return 0

    lax.fori_loop(0, nkt, idx_body, 0)

    def count16(ref, pred):
        def body(kt, acc):
            hit = jnp.where(pred(ref[pl.ds(trip_rows(kt), DSA_KT), :]), jnp.int16(1), jnp.int16(0))
            return acc + _fold_rows(hit, 16)

        acc = lax.fori_loop(0, nkt, body, jnp.zeros((16, Q_BLOCK), jnp.int16))
        return jnp.sum(acc.astype(jnp.int32), axis=0, keepdims=True)

    def count(pred):
        def body(kt, acc):
            off = trip_rows(kt)
            hit = jnp.where(pred(keys_s[pl.ds(off, DSA_KT), :], off + kt_iota), 1, 0)
            return acc + jnp.sum(hit.reshape(DSA_KT // 8, 8, Q_BLOCK), axis=0)

        acc = lax.fori_loop(0, nkt, body, jnp.zeros((8, Q_BLOCK), jnp.int32))
        return jnp.sum(acc, axis=0, keepdims=True)

    hi_u = _bisect16(lambda c: count16(hi_s, lambda v: v >= c), topk)
    thr_hi = (hi_u + I16_MIN).astype(jnp.int16)
    need_lo = topk - count16(hi_s, lambda v: v > thr_hi)

    def mask_lo(kt, _):
        rows = pl.ds(trip_rows(kt), DSA_KT)
        lo_s[rows, :] = jnp.where(hi_s[rows, :] == thr_hi, lo_s[rows, :], jnp.int16(I16_MIN))
        return 0

    lax.fori_loop(0, nkt, mask_lo, 0)
    lo_u = _bisect16(lambda c: count16(lo_s, lambda v: v >= c), need_lo)
    thr = lax.shift_left(hi_u + I16_MIN, 16) | lo_u
    real_thr = thr != jnp.int32(INT_MIN)
    n_tied = count16(lo_s, lambda v: v >= (lo_u + I16_MIN).astype(jnp.int16))
    pos_bits = (seq - 1).bit_length()

    def _tie_search():
        need = topk - count(lambda k, pos: k > thr)

        def body(it, v):
            cand = v | lax.shift_left(jnp.int32(1), jnp.int32(pos_bits - 1) - it)
            return jnp.where(count(lambda k, pos: (k == thr) & (pos < cand)) < need, cand, v)

        return lax.fori_loop(0, pos_bits, body, jnp.zeros((1, Q_BLOCK), jnp.int32))

    has_ties = jnp.max(jnp.where(real_thr & (n_tied > need_lo), 1, 0)) > 0
    tie_last = lax.cond(has_ties, _tie_search, lambda: jnp.full((1, Q_BLOCK), 2 ** pos_bits - 1, jnp.int32))
    tie_last = jnp.where(real_thr, tie_last, -1)

    m_s[...] = jnp.full(m_s.shape, NEG_INF, F32)
    l_s[...] = jnp.zeros(l_s.shape, F32)
    acc_s[...] = jnp.zeros(acc_s.shape, F32)

    def mask_body(kt, _):
        off = trip_rows(kt)
        keyc = keys_s[pl.ds(off, DSA_KT), :]
        sel = (keyc > thr) | ((keyc == thr) & (off + kt_iota <= tie_last))
        negm_s[pl.ds(off, DSA_KT), :] = jnp.where(sel, 0.0, NEG_INF)
        return 0

    lax.fori_loop(0, nkt, mask_body, 0)

    def att_body(kt, _):
        off = trip_rows(kt)
        s_all = _nt_dot(kn_s[pl.ds(off, DSA_KT), :], q_all)
        negm = negm_s[pl.ds(off, DSA_KT), :]
        bidx = [jnp.clip(j - (DSA_TRIP_BLOCKS * kt + t), 0, 2) for t in range(DSA_TRIP_BLOCKS)]
        vtc = vt_s[:, pl.ds(off, DSA_KT)]
        for h in range(C_HEADS):
            ls = slice(h * Q_BLOCK, (h + 1) * Q_BLOCK)
            bias = jnp.concatenate([bias_ref[bi, :, ls] for bi in bidx], axis=0)
            s = s_all[:, ls] + bias + negm
            m_old = m_s[h:h + 1, :]
            m_new = jnp.maximum(m_old, jnp.max(s, axis=0, keepdims=True))
            alpha = jnp.exp(m_old - m_new)
            p = jnp.exp(s - m_new)
            l_s[h:h + 1, :] = alpha * l_s[h:h + 1, :] + jnp.sum(p, axis=0, keepdims=True)
            rs = slice(h * HEAD_DIM, (h + 1) * HEAD_DIM)
            acc_s[rs, :] = alpha * acc_s[rs, :] + _dot(vtc, p.astype(MXU_DTYPE))
            m_s[h:h + 1, :] = m_new
        return 0

    lax.fori_loop(0, nkt, att_body, 0)
    out_t = jnp.concatenate(
        [acc_s[h * HEAD_DIM:(h + 1) * HEAD_DIM, :] / l_s[h:h + 1, :] for h in range(C_HEADS)], axis=0)
    o_ref[0] = out_t.T


def _t5_bucket(rel):
    half = T5_BUCKETS // 2
    max_exact = half // 2
    n = jnp.abs(rel)
    log_val = jnp.log(jnp.maximum(n, 1).astype(F32) / max_exact) / math.log(T5_MAX_DIST / max_exact)
    large = jnp.minimum(max_exact + (log_val * (half - max_exact)).astype(jnp.int32), half - 1)
    return jnp.where(rel > 0, half, 0) + jnp.where(n < max_exact, n, large)


def _dsa_bias_tiles(t5_table):
    s = jnp.arange(Q_BLOCK)[:, None]
    t = jnp.arange(Q_BLOCK)[None, :]
    tiles = []
    for d in range(3):
        rel = s - d * Q_BLOCK - t
        tile = t5_table.astype(F32)[_t5_bucket(rel)]
        tiles.append(tile.transpose(0, 2, 1).reshape(Q_BLOCK, C_HEADS * Q_BLOCK))
    return jnp.stack(tiles)


def _dsa(z3, bias, gq, gk):
    b, seq, _ = z3.shape
    assert seq % DSA_KT == 0
    topk = min(TOPK_MAX, seq // 4)
    kern = functools.partial(_dsa_kernel, seq=seq, topk=topk)
    return pl.pallas_call(
        kern,
        grid=(b, seq // Q_BLOCK),
        in_specs=[
            pl.BlockSpec((1, Q_BLOCK, C_WIDTH), lambda i, j: (i, j, COL_CQ // C_WIDTH)),
            pl.BlockSpec((1, seq, 128), lambda i, j: (i, 0, COL_CKV // 128)),
            pl.BlockSpec((1, Q_BLOCK, 256), lambda i, j: (i, j, COL_QI // 256)),
            pl.BlockSpec((1, Q_BLOCK, 128), lambda i, j: (i, j, COL_KIW // 128)),
            pl.BlockSpec((1, seq, 128), lambda i, j: (i, 0, COL_KIW // 128)),
            pl.BlockSpec((1, HEAD_DIM), lambda i, j: (0, 0)),
            pl.BlockSpec((1, HEAD_DIM), lambda i, j: (0, 0)),
            pl.BlockSpec((3, Q_BLOCK, C_HEADS * Q_BLOCK), lambda i, j: (0, 0, 0)),
        ],
        out_specs=[
            pl.BlockSpec((1, Q_BLOCK, C_WIDTH), lambda i, j: (i, j, 0)),
            pl.BlockSpec((1, seq, HEAD_DIM), lambda i, j: (i, 0, 0)),
            pl.BlockSpec((1, seq, HEAD_DIM), lambda i, j: (i, 0, 0)),
            pl.BlockSpec((1, seq, IDX_DIM), lambda i, j: (i, 0, 0)),
        ],
        out_shape=[
            jax.ShapeDtypeStruct((b, seq, C_WIDTH), F32),
            jax.ShapeDtypeStruct((b, seq, HEAD_DIM), F32),
            jax.ShapeDtypeStruct((b, seq, HEAD_DIM), F32),
            jax.ShapeDtypeStruct((b, seq, IDX_DIM), F32),
        ],
        scratch_shapes=[
            pltpu.VMEM((seq, HEAD_DIM), MXU_DTYPE),
            pltpu.VMEM((HEAD_DIM, seq), MXU_DTYPE),
            pltpu.VMEM((seq, IDX_DIM), MXU_DTYPE),
            pltpu.VMEM((seq, Q_BLOCK), jnp.int32),
            pltpu.VMEM((seq, Q_BLOCK), jnp.int16),
            pltpu.VMEM((seq, Q_BLOCK), jnp.int16),
            pltpu.VMEM((seq, Q_BLOCK), F32),
            pltpu.VMEM((8, Q_BLOCK), F32),
            pltpu.VMEM((8, Q_BLOCK), F32),
            pltpu.VMEM((C_WIDTH, Q_BLOCK), F32),
        ],
        compiler_params=pltpu.CompilerParams(
            dimension_semantics=("parallel", "arbitrary"), vmem_limit_bytes=VMEM_LIMIT),
        name="dsa",
    )(z3, z3, z3, z3, z3, gq, gk, bias)


def _mix_residual(x_ref, oa_ref, ob_ref, oc_ref, wo_ref):
    mix = (_dot(oa_ref[...].astype(MXU_DTYPE), wo_ref[0:A_WIDTH, :])
           + _dot(ob_ref[...].astype(MXU_DTYPE), wo_ref[A_WIDTH:A_WIDTH + S5_WIDTH, :])
           + _dot(oc_ref[...].astype(MXU_DTYPE), wo_ref[A_WIDTH + S5_WIDTH:, :]))
    return x_ref[...] + mix


def _ffn_chunk(xn, g, g1, g2, sl, wu_ref, cw_ref, cb_ref, wd_ref):
    up = _dot(xn, wu_ref[:, sl])
    gc = cw_ref[0:1, sl] * g2 + cw_ref[1:2, sl] * g1 + cw_ref[2:3, sl] * g + cb_ref[:, sl]
    return _dot((jax.nn.silu(gc) * up).astype(MXU_DTYPE), wd_ref[sl, :])


def _out_ffn_kernel(x_ref, oa_ref, ob_ref, oc_ref, wo_ref, g2_ref, wg_ref, wu_ref, cw_ref, cb_ref, wd_ref,
                    y_ref, conv_ref, carry_s, *, tiles_per_seq, tf):
    i = pl.program_id(0)
    tm = x_ref.shape[0]
    x1 = _mix_residual(x_ref, oa_ref, ob_ref, oc_ref, wo_ref)
    xn = _rms(x1, g2_ref[...]).astype(MXU_DTYPE)

    @pl.when(i % tiles_per_seq == 0)
    def _seq_start():
        carry_s[...] = jnp.zeros(carry_s.shape, F32)

    row = lax.broadcasted_iota(jnp.int32, (tm, 1), 0)
    acc = x1
    for f in range(FFN_DIM // tf):
        sl = slice(f * tf, (f + 1) * tf)
        g = _dot(xn, wg_ref[:, sl])
        prev2, prev1 = carry_s[0:1, sl], carry_s[1:2, sl]
        g1 = jnp.where(row == 0, prev1, pltpu.roll(g, 1, 0))
        g2 = jnp.where(row == 0, prev2, jnp.where(row == 1, prev1, pltpu.roll(g, 2, 0)))
        acc = acc + _ffn_chunk(xn, g, g1, g2, sl, wu_ref, cw_ref, cb_ref, wd_ref)
        carry_s[0:2, sl] = g[tm - 2:tm, :]
    y_ref[...] = acc
    conv_ref[0] = carry_s[0:2, :]


def _out_ffn(x2d, oa, ob, oc, wo, g2, wg, wu, cw, cb, wd, *, seq, tm, tf):
    m = x2d.shape[0]
    tiles_per_seq = seq // tm
    kern = functools.partial(_out_ffn_kernel, tiles_per_seq=tiles_per_seq, tf=tf)
    row = lambda w: pl.BlockSpec((tm, w), lambda i: (i, 0))
    const = lambda shape: pl.BlockSpec(shape, lambda i: (0,) * len(shape), pipeline_mode=pl.Buffered(1))
    return pl.pallas_call(
        kern,
        grid=(m // tm,),
        in_specs=[
            row(D_MODEL), row(A_WIDTH), row(S5_WIDTH), row(C_WIDTH),
            const((D_MODEL, D_MODEL)), const((1, D_MODEL)),
            const((D_MODEL, FFN_DIM)), const((D_MODEL, FFN_DIM)),
            const((CONV_W, FFN_DIM)), const((1, FFN_DIM)), const((FFN_DIM, D_MODEL)),
        ],
        out_specs=[
            row(D_MODEL),
            pl.BlockSpec((1, CONV_W - 1, FFN_DIM), lambda i: (i // tiles_per_seq, 0, 0)),
        ],
        out_shape=[
            jax.ShapeDtypeStruct((m, D_MODEL), F32),
            jax.ShapeDtypeStruct((m // seq, CONV_W - 1, FFN_DIM), F32),
        ],
        scratch_shapes=[pltpu.VMEM((8, FFN_DIM), F32)],
        compiler_params=pltpu.CompilerParams(dimension_semantics=("arbitrary",), vmem_limit_bytes=VMEM_LIMIT),
        name="out_ffn",
    )(x2d, oa, ob, oc, wo, g2, wg, wu, cw, cb, wd)


def _layer_params(l, t5_bias, norm1_g, w_in, a_q_gain, a_k_gain, a_rel_bias, s5_lam_re, s5_lam_im, s5_log_dt,
                  s5_b_re, s5_b_im, s5_c_re, s5_c_im, s5_d, s5_w_glu, s5_b_glu, c_q_gain, c_k_gain, w_out,
                  norm2_g, ffn_w_gate, ffn_w_up, ffn_conv_w, ffn_conv_b, ffn_w_down):
    w = w_in[l]
    sizes = (A_WIDTH, A_WIDTH, A_WIDTH, S5_WIDTH, C_WIDTH, HEAD_DIM, HEAD_DIM, IDX_HEADS * IDX_DIM, IDX_DIM, IDX_HEADS)
    cuts = [0]
    for s in sizes:
        cuts.append(cuts[-1] + s)
    aq, ak, av, u, cq, ck, cv, qi, ki, wi = [w[:, cuts[n]:cuts[n + 1]] for n in range(len(sizes))]
    pad = jnp.zeros((D_MODEL, Z_WIDTH - COL_KIW - IDX_DIM - IDX_HEADS), w.dtype)
    w_r = jnp.concatenate([aq, ak, av, cq, u, qi, ck, cv, ki, wi, pad], axis=1).astype(MXU_DTYPE)
    return dict(
        norm1=norm1_g[l].reshape(1, D_MODEL), w_in=w_r,
        a_gq=a_q_gain[l].reshape(1, HEAD_DIM), a_gk=a_k_gain[l].reshape(1, HEAD_DIM), a_rel=a_rel_bias[l],
        s5=_s5_params(s5_lam_re[l], s5_lam_im[l], s5_log_dt[l], s5_b_re[l], s5_b_im[l], s5_c_re[l], s5_c_im[l]),
        s5_d=s5_d[l].reshape(1, S5_WIDTH), s5_wg=s5_w_glu[l].astype(MXU_DTYPE), s5_bg=s5_b_glu[l].reshape(1, S5_WIDTH),
        c_gq=c_q_gain[l].reshape(1, HEAD_DIM), c_gk=c_k_gain[l].reshape(1, HEAD_DIM),
        w_out=w_out[l].astype(MXU_DTYPE), norm2=norm2_g[l].reshape(1, D_MODEL),
        wg=ffn_w_gate[l].astype(MXU_DTYPE), wu=ffn_w_up[l].astype(MXU_DTYPE), cw=ffn_conv_w[l],
        cb=ffn_conv_b[l].reshape(1, FFN_DIM), wd=ffn_w_down[l].astype(MXU_DTYPE),
    )


def _prompt_layer(x, p, dsa_bias, *, tm, tf, s5_bt, s5_tc):
    b, seq, _ = x.shape
    x2d = x.reshape(b * seq, D_MODEL)
    z3 = _in_proj(x2d, p["norm1"], p["w_in"], tm).reshape(b, seq, Z_WIDTH)
    oa, a_k, a_v = _attn_a(z3, _attn_a_bias(p["a_rel"]), p["a_gq"], p["a_gk"])
    zero_h = jnp.zeros((b, S5_FLAT), F32)
    ob, h_re, h_im = _s5(z3, zero_h, zero_h, p["s5"], p["s5_d"], p["s5_wg"], p["s5_bg"],
                         bt=s5_bt, tc=s5_tc, exact_in=False)
    oc, c_k, c_v, c_ki = _dsa(z3, dsa_bias, p["c_gq"], p["c_gk"])
    y, conv = _out_ffn(x2d, oa.reshape(b * seq, A_WIDTH), ob.reshape(b * seq, S5_WIDTH),
                       oc.reshape(b * seq, C_WIDTH), p["w_out"], p["norm2"], p["wg"], p["wu"], p["cw"], p["cb"],
                       p["wd"], seq=seq, tm=tm, tf=tf)
    keep = a_k.shape[1]
    states = (a_k.reshape(b, keep, A_HEADS, HEAD_DIM), a_v.reshape(b, keep, A_HEADS, HEAD_DIM),
              h_re.reshape(b, S5_GROUPS, S5_STATE), h_im.reshape(b, S5_GROUPS, S5_STATE), c_k, c_v, c_ki, conv)
    return y.reshape(b, seq, D_MODEL), states


def _attn_a_step_kernel(q_ref, k_ref, v_ref, ck_ref, cv_ref, bc_ref, bn_ref, gq_ref, gk_ref, o_ref, ak_ref, av_ref):
    v = v_ref[0]
    av_ref[0] = v
    for h in range(A_HEADS):
        sl = slice(h * HEAD_DIM, (h + 1) * HEAD_DIM)
        qh = (_rms(q_ref[0, :, sl], gq_ref[...]) * HEAD_DIM ** -0.5).astype(MXU_DTYPE)
        khn = _rms(k_ref[0, :, sl], gk_ref[...])
        ak_ref[0, :, sl] = khn
        s_c = _nt_dot(qh, ck_ref[0, :, sl].astype(MXU_DTYPE)) + bc_ref[h]
        s_n = _nt_dot(qh, khn.astype(MXU_DTYPE)) + bn_ref[h]
        m = jnp.maximum(jnp.max(s_c, axis=-1, keepdims=True), jnp.max(s_n, axis=-1, keepdims=True))
        p_c = jnp.exp(s_c - m)
        p_n = jnp.exp(s_n - m)
        denom = jnp.sum(p_c, axis=-1, keepdims=True) + jnp.sum(p_n, axis=-1, keepdims=True)
        oh = (_dot(p_c.astype(MXU_DTYPE), cv_ref[0, :, sl].astype(MXU_DTYPE))
              + _dot(p_n.astype(MXU_DTYPE), v[:, sl].astype(MXU_DTYPE)))
        o_ref[0, :, sl] = oh / denom


def _attn_a_step_bias(rel_table, past, t_new):
    t = jnp.arange(t_new)[:, None]
    rel = jnp.concatenate([jnp.arange(past) - past, jnp.arange(t_new)])[None, :] - t
    bias = rel_table.astype(F32)[:, jnp.clip(rel, -A_MAX_REL, A_MAX_REL) + A_MAX_REL]
    return bias[:, :, :past], bias[:, :, past:]


def _attn_a_step(z3, cache_k, cache_v, layer, bias_c, bias_n, gq, gk):
    b, t_new, _ = z3.shape
    past = cache_k.shape[2]
    new = lambda col: pl.BlockSpec((1, t_new, A_WIDTH), lambda i: (i, 0, col // A_WIDTH))
    const = lambda shape: pl.BlockSpec(shape, lambda i: (0,) * len(shape))
    return pl.pallas_call(
        _attn_a_step_kernel,
        grid=(b,),
        in_specs=[
            new(COL_AQ), new(COL_AK), new(COL_AV),
            pl.BlockSpec((None, 1, past, A_WIDTH), lambda i: (layer, i, 0, 0)),
            pl.BlockSpec((None, 1, past, A_WIDTH), lambda i: (layer, i, 0, 0)),
            const((A_HEADS, t_new, past)), const((A_HEADS, t_new, t_new)),
            const((1, HEAD_DIM)), const((1, HEAD_DIM)),
        ],
        out_specs=[pl.BlockSpec((1, t_new, A_WIDTH), lambda i: (i, 0, 0))] * 3,
        out_shape=[jax.ShapeDtypeStruct((b, t_new, A_WIDTH), F32)] * 3,
        compiler_params=pltpu.CompilerParams(dimension_semantics=("parallel",), vmem_limit_bytes=VMEM_LIMIT),
        name="attn_a_step",
    )(z3, z3, z3, cache_k, cache_v, bias_c, bias_n, gq, gk)


def _dsa_step_kernel(cq_ref, ckv_ref, qi_ref, kiw_ref, ck_ref, cv_ref, cki_ref, gq_ref, gk_ref, bias_ref,
                     o_ref, ck_out, cv_out, cki_out, *, past, t_new, topk):
    n_keys = past + t_new
    kn_new = _rms(ckv_ref[0, :, 0:HEAD_DIM], gk_ref[...])
    cv_new = ckv_ref[0, :, HEAD_DIM:2 * HEAD_DIM]
    ki_new = kiw_ref[0, :, 0:IDX_DIM]
    ck_out[0] = kn_new
    cv_out[0] = cv_new
    cki_out[0] = ki_new
    k_all = jnp.concatenate([ck_ref[0], kn_new], axis=0).astype(MXU_DTYPE)
    v_all = jnp.concatenate([cv_ref[0], cv_new], axis=0).astype(MXU_DTYPE)
    ki_all = jnp.concatenate([cki_ref[0], ki_new], axis=0).astype(MXU_DTYPE)
    eye = (lax.broadcasted_iota(jnp.int32, (HEAD_DIM, HEAD_DIM), 0)
           == lax.broadcasted_iota(jnp.int32, (HEAD_DIM, HEAD_DIM), 1)).astype(MXU_DTYPE)
    v_t = _nt_dot(eye, v_all).astype(MXU_DTYPE)

    qi = qi_ref[0].astype(MXU_DTYPE)
    kiw_pad = jnp.concatenate([kiw_ref[0], jnp.zeros((128 - t_new, 128), F32)], axis=0)
    w_t = kiw_pad.T[:, 0:t_new]
    score = jnp.zeros((n_keys, t_new), F32)
    for h in range(IDX_HEADS):
        dots = _nt_dot(ki_all, qi[:, h * IDX_DIM:(h + 1) * IDX_DIM])
        score = score + (w_t[IDX_DIM + h:IDX_DIM + h + 1, :] * INDEX_SCALE) * jnp.maximum(dots, 0.0)
    pos = lax.broadcasted_iota(jnp.int32, (n_keys, 1), 0)
    q_chunk = (past + lax.broadcasted_iota(jnp.int32, (1, t_new), 1)) // CHUNK
    keys = jnp.where((pos // CHUNK) <= q_chunk, _sortable_key(score), jnp.int32(INT_MIN))

    def count(pred):
        return jnp.sum(jnp.where(pred(keys, pos), 1, 0), axis=0, keepdims=True)

    def kth_body(it, res):
        cand = res | lax.shift_left(jnp.int32(1), jnp.int32(31) - it)
        thr_c = cand ^ jnp.int32(INT_MIN)
        return jnp.where(count(lambda k, p: k >= thr_c) >= topk, cand, res)

    thr = lax.fori_loop(0, 32, kth_body, jnp.zeros((1, t_new), jnp.int32)) ^ jnp.int32(INT_MIN)
    real_thr = thr != jnp.int32(INT_MIN)
    need = topk - count(lambda k, p: k > thr)
    pos_bits = (n_keys - 1).bit_length()

    def tie_body(it, v):
        cand = v | lax.shift_left(jnp.int32(1), jnp.int32(pos_bits - 1) - it)
        return jnp.where(count(lambda k, p: (k == thr) & (p < cand)) < need, cand, v)

    tie_last = lax.fori_loop(0, pos_bits, tie_body, jnp.zeros((1, t_new), jnp.int32))
    tie_last = jnp.where(real_thr, tie_last, -1)
    negm = jnp.where((keys > thr) | ((keys == thr) & (pos <= tie_last)), 0.0, NEG_INF)

    cq = cq_ref[0]
    outs = []
    for h in range(C_HEADS):
        qh = (_rms(cq[:, h * HEAD_DIM:(h + 1) * HEAD_DIM], gq_ref[...]) * HEAD_DIM ** -0.5).astype(MXU_DTYPE)
        s = _nt_dot(k_all, qh) + bias_ref[h] + negm
        p = jnp.exp(s - jnp.max(s, axis=0, keepdims=True))
        outs.append(_dot(v_t, p.astype(MXU_DTYPE)) / jnp.sum(p, axis=0, keepdims=True))
    out_t = jnp.concatenate(outs, axis=0)
    out_pad = jnp.concatenate([out_t, jnp.zeros((C_WIDTH, 128 - t_new), F32)], axis=1)
    o_ref[0] = out_pad.T[0:t_new, :]


def _dsa_step_bias(t5_table, past, t_new):
    s = jnp.arange(past + t_new)[:, None]
    t = jnp.arange(t_new)[None, :]
    return t5_table.astype(F32)[_t5_bucket(s - (past + t))].transpose(2, 0, 1)


def _dsa_step(z3, cache_k, cache_v, cache_ki, layer, bias, gq, gk):
    b, t_new, _ = z3.shape
    past = cache_k.shape[2]
    topk = min(TOPK_MAX, (past + t_new) // 4)
    kern = functools.partial(_dsa_step_kernel, past=past, t_new=t_new, topk=topk)
    new = lambda w, col: pl.BlockSpec((1, t_new, w), lambda i: (i, 0, col // w))
    per_seq = lambda n, w: pl.BlockSpec((1, n, w), lambda i: (i, 0, 0))
    cached = lambda w: pl.BlockSpec((None, 1, past, w), lambda i: (layer, i, 0, 0))
    const = lambda shape: pl.BlockSpec(shape, lambda i: (0,) * len(shape))
    return pl.pallas_call(
        kern,
        grid=(b,),
        in_specs=[
            new(C_WIDTH, COL_CQ), new(128, COL_CKV), new(256, COL_QI), new(128, COL_KIW),
            cached(HEAD_DIM), cached(HEAD_DIM), cached(IDX_DIM),
            const((1, HEAD_DIM)), const((1, HEAD_DIM)), const((C_HEADS, past + t_new, t_new)),
        ],
        out_specs=[per_seq(t_new, C_WIDTH), per_seq(t_new, HEAD_DIM), per_seq(t_new, HEAD_DIM),
                   per_seq(t_new, IDX_DIM)],
        out_shape=[
            jax.ShapeDtypeStruct((b, t_new, C_WIDTH), F32),
            jax.ShapeDtypeStruct((b, t_new, HEAD_DIM), F32),
            jax.ShapeDtypeStruct((b, t_new, HEAD_DIM), F32),
            jax.ShapeDtypeStruct((b, t_new, IDX_DIM), F32),
        ],
        compiler_params=pltpu.CompilerParams(dimension_semantics=("parallel",), vmem_limit_bytes=VMEM_LIMIT),
        name="dsa_step",
    )(z3, z3, z3, z3, cache_k, cache_v, cache_ki, gq, gk, bias)


def _out_ffn_step_kernel(x_ref, oa_ref, ob_ref, oc_ref, wo_ref, g2_ref, wg_ref, wu_ref, cw_ref, cb_ref, wd_ref,
                         e1_ref, e2_ref, y_ref, gate_ref, *, t_new, tf):
    tm = x_ref.shape[0]
    x1 = _mix_residual(x_ref, oa_ref, ob_ref, oc_ref, wo_ref)
    xn = _rms(x1, g2_ref[...]).astype(MXU_DTYPE)
    t = lax.broadcasted_iota(jnp.int32, (tm, 1), 0) % t_new
    acc = x1
    for f in range(FFN_DIM // tf):
        sl = slice(f * tf, (f + 1) * tf)
        g = _dot(xn, wg_ref[:, sl])
        gate_ref[:, sl] = g
        g1 = jnp.where(t == 0, e1_ref[:, sl], pltpu.roll(g, 1, 0))
        g2 = jnp.where(t <= 1, e2_ref[:, sl], pltpu.roll(g, 2, 0))
        acc = acc + _ffn_chunk(xn, g, g1, g2, sl, wu_ref, cw_ref, cb_ref, wd_ref)
    y_ref[...] = acc


def _out_ffn_step(x2d, oa, ob, oc, wo, g2, wg, wu, cw, cb, wd, conv_prev, *, t_new, tf):
    m = x2d.shape[0]
    b = m // t_new
    e1 = jnp.zeros((b, t_new, FFN_DIM), F32).at[:, 0].set(conv_prev[:, 1]).reshape(m, FFN_DIM)
    e2 = jnp.zeros((b, t_new, FFN_DIM), F32).at[:, 0].set(conv_prev[:, 0]).at[:, 1].set(conv_prev[:, 1])
    e2 = e2.reshape(m, FFN_DIM)
    kern = functools.partial(_out_ffn_step_kernel, t_new=t_new, tf=tf)
    full = lambda shape: pl.BlockSpec(shape, lambda i: (0,) * len(shape))
    y, gate = pl.pallas_call(
        kern,
        grid=(1,),
        in_specs=[
            full((m, D_MODEL)), full((m, A_WIDTH)), full((m, S5_WIDTH)), full((m, C_WIDTH)),
            full((D_MODEL, D_MODEL)), full((1, D_MODEL)),
            full((D_MODEL, FFN_DIM)), full((D_MODEL, FFN_DIM)),
            full((CONV_W, FFN_DIM)), full((1, FFN_DIM)), full((FFN_DIM, D_MODEL)),
            full((m, FFN_DIM)), full((m, FFN_DIM)),
        ],
        out_specs=[full((m, D_MODEL)), full((m, FFN_DIM))],
        out_shape=[jax.ShapeDtypeStruct((m, D_MODEL), F32), jax.ShapeDtypeStruct((m, FFN_DIM), F32)],
        compiler_params=pltpu.CompilerParams(dimension_semantics=("arbitrary",), vmem_limit_bytes=VMEM_LIMIT),
        name="out_ffn_step",
    )(x2d, oa, ob, oc, wo, g2, wg, wu, cw, cb, wd, e1, e2)
    return y, gate.reshape(b, t_new, FFN_DIM)[:, t_new - (CONV_W - 1):]


def _sample_layer(x, p, dsa_bias, layer, ca_k, ca_v, h_re0, h_im0, cc_k, cc_v, cc_ki, conv_prev, *, tf):
    b, t_new, _ = x.shape
    m = b * t_new
    x2d = x.reshape(m, D_MODEL)
    z3 = _in_proj(x2d, p["norm1"], p["w_in"], m).reshape(b, t_new, Z_WIDTH)
    depth, _, a_past = ca_k.shape[:3]
    bias_c, bias_n = _attn_a_step_bias(p["a_rel"], a_past, t_new)
    oa, a_k, a_v = _attn_a_step(z3, ca_k.reshape(depth, b, a_past, A_WIDTH), ca_v.reshape(depth, b, a_past, A_WIDTH),
                                layer, bias_c, bias_n, p["a_gq"], p["a_gk"])
    ob, h_re, h_im = _s5(z3, h_re0.reshape(b, S5_FLAT), h_im0.reshape(b, S5_FLAT), p["s5"], p["s5_d"], p["s5_wg"],
                         p["s5_bg"], bt=8, tc=t_new, exact_in=True)
    oc, c_k, c_v, c_ki = _dsa_step(z3, cc_k, cc_v, cc_ki, layer, dsa_bias, p["c_gq"], p["c_gk"])
    y, conv = _out_ffn_step(x2d, oa.reshape(m, A_WIDTH), ob.reshape(m, S5_WIDTH), oc.reshape(m, C_WIDTH),
                            p["w_out"], p["norm2"], p["wg"], p["wu"], p["cw"], p["cb"], p["wd"], conv_prev,
                            t_new=t_new, tf=tf)
    states = (a_k.reshape(b, t_new, A_HEADS, HEAD_DIM), a_v.reshape(b, t_new, A_HEADS, HEAD_DIM),
              h_re.reshape(b, S5_GROUPS, S5_STATE), h_im.reshape(b, S5_GROUPS, S5_STATE), c_k, c_v, c_ki, conv)
    return y.reshape(b, t_new, D_MODEL), states


def kernel(x_prompt, x_sample, cache_a_k, cache_a_v, state_s5_re, state_s5_im, cache_c_k, cache_c_v, cache_c_idx_k,
           state_ffn_conv, t5_bias, norm1_g, w_in, a_q_gain, a_k_gain, a_rel_bias, s5_lam_re, s5_lam_im, s5_log_dt,
           s5_b_re, s5_b_im, s5_c_re, s5_c_im, s5_d, s5_w_glu, s5_b_glu, c_q_gain, c_k_gain, w_out, norm2_g,
           ffn_w_gate, ffn_w_up, ffn_conv_w, ffn_conv_b, ffn_w_down):
    depth = w_in.shape[0]
    dsa_bias = _dsa_bias_tiles(t5_bias)
    dsa_step_bias = _dsa_step_bias(t5_bias, cache_c_k.shape[2], x_sample.shape[1])
    yp, ys = x_prompt, x_sample
    prompt_states, sample_states = [], []
    for l in range(depth):
        p = _layer_params(l, t5_bias, norm1_g, w_in, a_q_gain, a_k_gain, a_rel_bias, s5_lam_re, s5_lam_im, s5_log_dt,
                          s5_b_re, s5_b_im, s5_c_re, s5_c_im, s5_d, s5_w_glu, s5_b_glu, c_q_gain, c_k_gain, w_out,
                          norm2_g, ffn_w_gate, ffn_w_up, ffn_conv_w, ffn_conv_b, ffn_w_down)
        yp, st_p = _prompt_layer(yp, p, dsa_bias, tm=512, tf=FFN_DIM, s5_bt=8, s5_tc=256)
        ys, st_s = _sample_layer(ys, p, dsa_step_bias, l, cache_a_k, cache_a_v, state_s5_re[l], state_s5_im[l],
                                 cache_c_k, cache_c_v, cache_c_idx_k, state_ffn_conv[l], tf=FFN_DIM)
        prompt_states.append(st_p)
        sample_states.append(st_s)
    (a_k_p, a_v_p, s5_re_p, s5_im_p, c_k_p, c_v_p, c_ki_p, conv_p) = [jnp.stack(z) for z in zip(*prompt_states)]
    (a_k_s, a_v_s, s5_re_s, s5_im_s, c_k_s, c_v_s, c_ki_s, conv_s) = [jnp.stack(z) for z in zip(*sample_states)]
    return (yp, ys, a_k_p, a_v_p, a_k_s, a_v_s, s5_re_p, s5_im_p, s5_re_s, s5_im_s,
            c_k_p, c_v_p, c_ki_p, c_k_s, c_v_s, c_ki_s, conv_p, conv_s)
```

```python
import functools
import math

import jax
import jax.numpy as jnp
from jax import lax
from jax.experimental import pallas as pl
from jax.experimental.pallas import tpu as pltpu

F32 = jnp.float32
MXU_DTYPE = jnp.bfloat16

D_MODEL = 1024
CHUNK = 64
HEAD_DIM = 64
A_HEADS = 6
A_WIDTH = A_HEADS * HEAD_DIM
A_BAND_PAST = 8 * CHUNK
A_MAX_REL = 128
S5_GROUPS = 16
S5_GROUP_CH = 16
S5_WIDTH = S5_GROUPS * S5_GROUP_CH
S5_STATE = 64
S5_FLAT = S5_GROUPS * S5_STATE
C_HEADS = 6
C_WIDTH = C_HEADS * HEAD_DIM
IDX_HEADS = 8
IDX_DIM = 32
INDEX_SCALE = (IDX_HEADS * IDX_DIM) ** -0.5
TOPK_MAX = 256
Q_BLOCK = 128
T5_BUCKETS = 32
T5_MAX_DIST = 128
FFN_DIM = 2816
CONV_W = 3
EPS = 1e-6
NEG_INF = -1e30
INT_MIN = -(2 ** 31)

COL_AQ, COL_AK, COL_AV, COL_CQ = 0, 384, 768, 1152
COL_U, COL_QI, COL_CKV, COL_KIW = 1536, 1792, 2048, 2176
Z_WIDTH = 2304
VMEM_LIMIT = 56 * 1024 * 1024


def _nt_dot(a, b):
    return lax.dot_general(a, b, (((1,), (1,)), ((), ())), preferred_element_type=F32)


def _dot(a, b):
    return jnp.dot(a, b, preferred_element_type=F32)


def _rms(x, g):
    return x * lax.rsqrt(jnp.mean(x * x, axis=-1, keepdims=True) + EPS) * g


def _in_proj_kernel(x_ref, g_ref, w_ref, z_ref):
    xn = _rms(x_ref[...], g_ref[...]).astype(MXU_DTYPE)
    z_ref[...] = _dot(xn, w_ref[...])


def _in_proj(x2d, g, w_r, tm):
    m = x2d.shape[0]
    return pl.pallas_call(
        _in_proj_kernel,
        grid=(m // tm,),
        in_specs=[
            pl.BlockSpec((tm, D_MODEL), lambda i: (i, 0)),
            pl.BlockSpec((1, D_MODEL), lambda i: (0, 0)),
            pl.BlockSpec((D_MODEL, Z_WIDTH), lambda i: (0, 0)),
        ],
        out_specs=pl.BlockSpec((tm, Z_WIDTH), lambda i: (i, 0)),
        out_shape=jax.ShapeDtypeStruct((m, Z_WIDTH), F32),
        compiler_params=pltpu.CompilerParams(dimension_semantics=("parallel",), vmem_limit_bytes=VMEM_LIMIT),
        name="in_proj",
    )(x2d, g, w_r)


A_QB = 2 * CHUNK
A_BAND = A_BAND_PAST + A_QB


def _eye(n, dtype):
    return (lax.broadcasted_iota(jnp.int32, (n, n), 0) == lax.broadcasted_iota(jnp.int32, (n, n), 1)).astype(dtype)


def _attn_a_kernel(q_ref, k_ref, v_ref, bias_ref, gq_ref, gk_ref, o_ref, ak_ref, av_ref, kn_s, vt_s, s_s, p_s,
                   *, seq, keep):
    j = pl.program_id(1)

    @pl.when(j == 0)
    def _prep():
        for h in range(A_HEADS):
            sl = slice(h * HEAD_DIM, (h + 1) * HEAD_DIM)
            khn = _rms(k_ref[0, :, sl], gk_ref[...])
            kn_s[h, 0:A_BAND_PAST, :] = jnp.zeros((A_BAND_PAST, HEAD_DIM), MXU_DTYPE)
            kn_s[h, A_BAND_PAST:A_BAND_PAST + seq, :] = khn.astype(MXU_DTYPE)
            ak_ref[0, :, sl] = khn[seq - keep:, :]
        v = v_ref[0]
        av_ref[0] = v[seq - keep:, :]
        vt_s[:, 0:A_BAND_PAST] = jnp.zeros((A_WIDTH, A_BAND_PAST), MXU_DTYPE)
        vt_s[:, A_BAND_PAST:A_BAND_PAST + seq] = _nt_dot(_eye(A_WIDTH, MXU_DTYPE), v.astype(MXU_DTYPE)).astype(MXU_DTYPE)

    start = pl.multiple_of(j * A_QB, A_QB)
    variant = jnp.minimum(j, A_LEAD_STEPS)
    for h in range(A_HEADS):
        sl = slice(h * HEAD_DIM, (h + 1) * HEAD_DIM)
        qh = (_rms(q_ref[0, :, sl], gq_ref[...]) * HEAD_DIM ** -0.5).astype(MXU_DTYPE)
        s_s[h] = _nt_dot(kn_s[h, pl.ds(start, A_BAND), :], qh) + bias_ref[variant, h]
    denoms = []
    for h in range(A_HEADS):
        s = s_s[h]
        p = jnp.exp(s - jnp.max(s, axis=0, keepdims=True))
        denoms.append(jnp.sum(p, axis=0, keepdims=True))
        p_s[h] = p.astype(MXU_DTYPE)
    outs = [_dot(vt_s[h * HEAD_DIM:(h + 1) * HEAD_DIM, pl.ds(start, A_BAND)], p_s[h]) / denoms[h]
            for h in range(A_HEADS)]
    o_ref[0] = jnp.concatenate(outs, axis=0).T


A_LEAD_STEPS = A_BAND_PAST // A_QB


def _attn_a_bias(rel_table):
    n_ext = A_BAND + A_QB - 1
    m = jnp.arange(n_ext)
    ext = rel_table.astype(F32)[:, jnp.clip(m - (A_QB - 1) - A_BAND_PAST, -A_MAX_REL, A_MAX_REL) + A_MAX_REL]
    rows = jnp.pad(jnp.broadcast_to(ext[:, None, :], (A_HEADS, A_QB, n_ext)), ((0, 0), (0, 0), (0, 1)))
    skew = rows.reshape(A_HEADS, A_QB * (n_ext + 1))[:, :A_QB * n_ext].reshape(A_HEADS, A_QB, n_ext)
    bias = skew[:, :, A_QB - 1:].transpose(0, 2, 1)
    c = jnp.arange(A_BAND)[:, None]
    lo = (jnp.arange(A_QB)[None, :] // CHUNK) * CHUNK
    in_band = (c >= lo) & (c < lo + A_BAND_PAST + CHUNK)
    first_live = (A_LEAD_STEPS - jnp.arange(A_LEAD_STEPS + 1)) * A_QB
    live = in_band[None] & (c[None] >= first_live[:, None, None])
    return jnp.where(live[:, None], bias[None], NEG_INF)


def _attn_a(z3, bias, gq, gk):
    b, seq, _ = z3.shape
    keep = min(A_BAND_PAST, seq)
    kern = functools.partial(_attn_a_kernel, seq=seq, keep=keep)
    return pl.pallas_call(
        kern,
        grid=(b, seq // A_QB),
        in_specs=[
            pl.BlockSpec((1, A_QB, A_WIDTH), lambda i, j: (i, j, COL_AQ // A_WIDTH)),
            pl.BlockSpec((1, seq, A_WIDTH), lambda i, j: (i, 0, COL_AK // A_WIDTH)),
            pl.BlockSpec((1, seq, A_WIDTH), lambda i, j: (i, 0, COL_AV // A_WIDTH)),
            pl.BlockSpec((A_LEAD_STEPS + 1, A_HEADS, A_BAND, A_QB), lambda i, j: (0, 0, 0, 0),
                         pipeline_mode=pl.Buffered(1)),
            pl.BlockSpec((1, HEAD_DIM), lambda i, j: (0, 0)),
            pl.BlockSpec((1, HEAD_DIM), lambda i, j: (0, 0)),
        ],
        out_specs=[
            pl.BlockSpec((1, A_QB, A_WIDTH), lambda i, j: (i, j, 0)),
            pl.BlockSpec((1, keep, A_WIDTH), lambda i, j: (i, 0, 0)),
            pl.BlockSpec((1, keep, A_WIDTH), lambda i, j: (i, 0, 0)),
        ],
        out_shape=[
            jax.ShapeDtypeStruct((b, seq, A_WIDTH), F32),
            jax.ShapeDtypeStruct((b, keep, A_WIDTH), F32),
            jax.ShapeDtypeStruct((b, keep, A_WIDTH), F32),
        ],
        scratch_shapes=[
            pltpu.VMEM((A_HEADS, A_BAND_PAST + seq, HEAD_DIM), MXU_DTYPE),
            pltpu.VMEM((A_WIDTH, A_BAND_PAST + seq), MXU_DTYPE),
            pltpu.VMEM((A_HEADS, A_BAND, A_QB), F32),
            pltpu.VMEM((A_HEADS, A_BAND, A_QB), MXU_DTYPE),
        ],
        compiler_params=pltpu.CompilerParams(
            dimension_semantics=("parallel", "arbitrary"), vmem_limit_bytes=VMEM_LIMIT),
        name="attn_a",
    )(z3, z3, z3, bias, gq, gk)


def _s5_kernel(u_ref, h0r_ref, h0i_ref, ar_ref, ai_ref, wbr_ref, wbi_ref, wcr_ref, wci_ref, d_ref, wg_ref, bg_ref,
               o_ref, hr_out, hi_out, xr_s, xi_s, hr_c, hi_c, *, bt, tc, exact_in):
    c = pl.program_id(1)

    @pl.when(c == 0)
    def _init():
        hr_c[...] = h0r_ref[...]
        hi_c[...] = h0i_ref[...]

    u = jnp.swapaxes(u_ref[...], 0, 1).reshape(tc * bt, S5_WIDTH)
    if exact_in:
        xr_s[...] = jnp.dot(u, wbr_ref[...], preferred_element_type=F32, precision=lax.Precision.HIGHEST)
        xi_s[...] = jnp.dot(u, wbi_ref[...], preferred_element_type=F32, precision=lax.Precision.HIGHEST)
    else:
        ub = u.astype(MXU_DTYPE)
        xr_s[...] = _dot(ub, wbr_ref[...])
        xi_s[...] = _dot(ub, wbi_ref[...])
    ar = ar_ref[...]
    ai = ai_ref[...]

    def step(t, carry):
        hr, hi = carry
        rows = pl.ds(pl.multiple_of(t * bt, bt), bt)
        nhr = ar * hr - ai * hi + xr_s[rows, :]
        nhi = ar * hi + ai * hr + xi_s[rows, :]
        xr_s[rows, :] = nhr
        xi_s[rows, :] = nhi
        return nhr, nhi

    hr, hi = lax.fori_loop(0, tc, step, (hr_c[...], hi_c[...]))
    hr_c[...] = hr
    hi_c[...] = hi
    hr_out[...] = hr
    hi_out[...] = hi
    y = (_dot(xr_s[...].astype(MXU_DTYPE), wcr_ref[...]) - _dot(xi_s[...].astype(MXU_DTYPE), wci_ref[...])
         + d_ref[...] * u)
    g = jax.nn.gelu(y)
    out = g * jax.nn.sigmoid(_dot(g.astype(MXU_DTYPE), wg_ref[...]) + bg_ref[...])
    o_ref[...] = jnp.swapaxes(out.reshape(tc, bt, S5_WIDTH), 0, 1)


def _s5_params(lam_re, lam_im, log_dt, b_re, b_im, c_re, c_im):
    dt = jnp.exp(log_dt.astype(F32))[:, None]
    lr, li = lam_re.astype(F32), lam_im.astype(F32)
    mag = jnp.exp(lr * dt)
    ab_re, ab_im = mag * jnp.cos(li * dt), mag * jnp.sin(li * dt)
    den = lr * lr + li * li
    cr = ((ab_re - 1.0) * lr + ab_im * li) / den
    ci = (ab_im * lr - (ab_re - 1.0) * li) / den
    br, bi = b_re.astype(F32), b_im.astype(F32)
    bb_re = cr[..., None] * br - ci[..., None] * bi
    bb_im = cr[..., None] * bi + ci[..., None] * br
    eye = jnp.eye(S5_GROUPS, dtype=F32)
    wb_re = jnp.einsum("gpc,gh->gchp", bb_re, eye).reshape(S5_WIDTH, S5_FLAT)
    wb_im = jnp.einsum("gpc,gh->gchp", bb_im, eye).reshape(S5_WIDTH, S5_FLAT)
    wc_re = jnp.einsum("gcp,gh->gphc", c_re.astype(F32), eye).reshape(S5_FLAT, S5_WIDTH).astype(MXU_DTYPE)
    wc_im = jnp.einsum("gcp,gh->gphc", c_im.astype(F32), eye).reshape(S5_FLAT, S5_WIDTH).astype(MXU_DTYPE)
    return ab_re.reshape(1, S5_FLAT), ab_im.reshape(1, S5_FLAT), wb_re, wb_im, wc_re, wc_im


def _s5(z3, h0r, h0i, sp, d, wg, bg, *, bt, tc, exact_in):
    b, seq, _ = z3.shape
    ar, ai, wbr, wbi, wcr, wci = sp
    if not exact_in:
        wbr, wbi = wbr.astype(MXU_DTYPE), wbi.astype(MXU_DTYPE)
    kern = functools.partial(_s5_kernel, bt=bt, tc=tc, exact_in=exact_in)
    const = lambda shape: pl.BlockSpec(shape, lambda i, c: (0,) * len(shape))
    return pl.pallas_call(
        kern,
        grid=(b // bt, seq // tc),
        in_specs=[
            pl.BlockSpec((bt, tc, S5_WIDTH), lambda i, c: (i, c, COL_U // S5_WIDTH)),
            pl.BlockSpec((bt, S5_FLAT), lambda i, c: (i, 0)),
            pl.BlockSpec((bt, S5_FLAT), lambda i, c: (i, 0)),
            const((1, S5_FLAT)), const((1, S5_FLAT)),
            const((S5_WIDTH, S5_FLAT)), const((S5_WIDTH, S5_FLAT)),
            const((S5_FLAT, S5_WIDTH)), const((S5_FLAT, S5_WIDTH)),
            const((1, S5_WIDTH)), const((S5_WIDTH, S5_WIDTH)), const((1, S5_WIDTH)),
        ],
        out_specs=[
            pl.BlockSpec((bt, tc, S5_WIDTH), lambda i, c: (i, c, 0)),
            pl.BlockSpec((bt, S5_FLAT), lambda i, c: (i, 0)),
            pl.BlockSpec((bt, S5_FLAT), lambda i, c: (i, 0)),
        ],
        out_shape=[
            jax.ShapeDtypeStruct((b, seq, S5_WIDTH), F32),
            jax.ShapeDtypeStruct((b, S5_FLAT), F32),
            jax.ShapeDtypeStruct((b, S5_FLAT), F32),
        ],
        scratch_shapes=[
            pltpu.VMEM((bt * tc, S5_FLAT), F32), pltpu.VMEM((bt * tc, S5_FLAT), F32),
            pltpu.VMEM((bt, S5_FLAT), F32), pltpu.VMEM((bt, S5_FLAT), F32),
        ],
        compiler_params=pltpu.CompilerParams(
            dimension_semantics=("parallel", "arbitrary"), vmem_limit_bytes=VMEM_LIMIT),
        name="s5",
    )(z3, h0r, h0i, ar, ai, wbr, wbi, wcr, wci, d, wg, bg)


def _sortable_key(score):
    bits = lax.bitcast_convert_type(score + 0.0, jnp.int32)
    return bits ^ ((bits >> 31) & jnp.int32(0x7FFFFFFF))


DSA_TRIP_BLOCKS = 4
DSA_KT = DSA_TRIP_BLOCKS * Q_BLOCK
I16_MIN = -(2 ** 15)


def _fold_rows(x, rows, op=jnp.add):
    parts = [x[i:i + rows] for i in range(0, x.shape[0], rows)]
    while len(parts) > 1:
        parts = [op(a, b) for a, b in zip(parts[0::2], parts[1::2])] + parts[len(parts) & ~1:]
    return parts[0]


def _bisect16(count_ge, k):
    def body(it, res):
        cand = res | lax.shift_left(jnp.int32(1), jnp.int32(15) - it)
        cnt = count_ge((cand + I16_MIN).astype(jnp.int16))
        return jnp.where(cnt >= k, cand, res)

    return lax.fori_loop(0, 16, body, jnp.zeros((1, Q_BLOCK), jnp.int32))


def _dsa_kernel(cq_ref, ckv_ref, qi_ref, kiwq_ref, kiwk_ref, gq_ref, gk_ref, bias_ref,
                o_ref, ck_out, cv_out, cki_out, kn_s, vt_s, ki_s, keys_s, hi_s, lo_s, negm_s, s_s, p_s, acc_s,
                *, seq, topk):
    j = pl.program_id(1)
    nkt = j // DSA_TRIP_BLOCKS + 1

    @pl.when(j == 0)
    def _prep():
        kn = _rms(ckv_ref[0, :, 0:HEAD_DIM], gk_ref[...])
        cv = ckv_ref[0, :, HEAD_DIM:2 * HEAD_DIM]
        ck_out[0] = kn
        cv_out[0] = cv
        kn_s[...] = kn.astype(MXU_DTYPE)
        eye = (lax.broadcasted_iota(jnp.int32, (HEAD_DIM, HEAD_DIM), 0)
               == lax.broadcasted_iota(jnp.int32, (HEAD_DIM, HEAD_DIM), 1)).astype(MXU_DTYPE)
        vt_s[...] = _nt_dot(eye, cv.astype(MXU_DTYPE)).astype(MXU_DTYPE)
        ki = kiwk_ref[0, :, 0:IDX_DIM]
        cki_out[0] = ki
        ki_s[...] = ki.astype(MXU_DTYPE)

    cq = cq_ref[0]
    q_all = jnp.concatenate(
        [(_rms(cq[:, h * HEAD_DIM:(h + 1) * HEAD_DIM], gq_ref[...]) * HEAD_DIM ** -0.5).astype(MXU_DTYPE)
         for h in range(C_HEADS)], axis=0)
    qi = qi_ref[0].astype(MXU_DTYPE)
    qi_all = jnp.concatenate([qi[:, h * IDX_DIM:(h + 1) * IDX_DIM] for h in range(IDX_HEADS)], axis=0)
    w_t = kiwq_ref[0].T
    w_rows = [w_t[IDX_DIM + h:IDX_DIM + h + 1, :] * INDEX_SCALE for h in range(IDX_HEADS)]
    q_chunk = (j * Q_BLOCK + lax.broadcasted_iota(jnp.int32, (1, Q_BLOCK), 1)) // CHUNK
    kt_iota = lax.broadcasted_iota(jnp.int32, (DSA_KT, 1), 0)

    def trip_rows(kt):
        return pl.multiple_of(kt * DSA_KT, DSA_KT)

    def idx_body(kt, _):
        off = trip_rows(kt)
        dots = _nt_dot(ki_s[pl.ds(off, DSA_KT), :], qi_all)
        score = w_rows[0] * jnp.maximum(dots[:, 0:Q_BLOCK], 0.0)
        for h in range(1, IDX_HEADS):
            score = score + w_rows[h] * jnp.maximum(dots[:, h * Q_BLOCK:(h + 1) * Q_BLOCK], 0.0)
        adm = ((off + kt_iota) // CHUNK) <= q_chunk
        key = jnp.where(adm, _sortable_key(score), jnp.int32(INT_MIN))
        keys_s[pl.ds(off, DSA_KT), :] = key
        hi_s[pl.ds(off, DSA_KT), :] = (key >> 16).astype(jnp.int16)
        lo_s[pl.ds(off, DSA_KT), :] = ((key & 0xFFFF) + I16_MIN).astype(jnp.int16)
        return 0

    lax.fori_loop(0, nkt, idx_body, 0)

    def count16(ref, pred):
        def body(kt, acc):
            hit = jnp.where(pred(ref[pl.ds(trip_rows(kt), DSA_KT), :]), jnp.int16(1), jnp.int16(0))
            return acc + _fold_rows(hit, 16)

        acc = lax.fori_loop(0, nkt, body, jnp.zeros((16, Q_BLOCK), jnp.int16))
        return jnp.sum(acc.astype(jnp.int32), axis=0, keepdims=True)

    def count(pred):
        def body(kt, acc):
            off = trip_rows(kt)
            hit = jnp.where(pred(keys_s[pl.ds(off, DSA_KT), :], off + kt_iota), 1, 0)
            return acc + jnp.sum(hit.reshape(DSA_KT // 8, 8, Q_BLOCK), axis=0)

        acc = lax.fori_loop(0, nkt, body, jnp.zeros((8, Q_BLOCK), jnp.int32))
        return jnp.sum(acc, axis=0, keepdims=True)

    hi_u = _bisect16(lambda c: count16(hi_s, lambda v: v >= c), topk)
    thr_hi = (hi_u + I16_MIN).astype(jnp.int16)
    need_lo = topk - count16(hi_s, lambda v: v > thr_hi)

    def mask_lo(kt, _):
        rows = pl.ds(trip_rows(kt), DSA_KT)
        lo_s[rows, :] = jnp.where(hi_s[rows, :] == thr_hi, lo_s[rows, :], jnp.int16(I16_MIN))
        return 0

    lax.fori_loop(0, nkt, mask_lo, 0)
    lo_u = _bisect16(lambda c: count16(lo_s, lambda v: v >= c), need_lo)
    thr = lax.shift_left(hi_u + I16_MIN, 16) | lo_u
    real_thr = thr != jnp.int32(INT_MIN)
    n_tied = count16(lo_s, lambda v: v >= (lo_u + I16_MIN).astype(jnp.int16))
    pos_bits = (seq - 1).bit_length()

    def _tie_search():
        need = topk - count(lambda k, pos: k > thr)

        def body(it, v):
            cand = v | lax.shift_left(jnp.int32(1), jnp.int32(pos_bits - 1) - it)
            return jnp.where(count(lambda k, pos: (k == thr) & (pos < cand)) < need, cand, v)

        return lax.fori_loop(0, pos_bits, body, jnp.zeros((1, Q_BLOCK), jnp.int32))

    has_ties = jnp.max(jnp.where(real_thr & (n_tied > need_lo), 1, 0)) > 0
    tie_last = lax.cond(has_ties, _tie_search, lambda: jnp.full((1, Q_BLOCK), 2 ** pos_bits - 1, jnp.int32))
    tie_last = jnp.where(real_thr, tie_last, -1)

    def mask_body(kt, _):
        off = trip_rows(kt)
        keyc = keys_s[pl.ds(off, DSA_KT), :]
        sel = (keyc > thr) | ((keyc == thr) & (off + kt_iota <= tie_last))
        negm_s[pl.ds(off, DSA_KT), :] = jnp.where(sel, 0.0, NEG_INF)
        return 0

    lax.fori_loop(0, nkt, mask_body, 0)

    heads = [slice(h * Q_BLOCK, (h + 1) * Q_BLOCK) for h in range(C_HEADS)]

    def score_body(kt, m8):
        off = trip_rows(kt)
        rows = pl.ds(off, DSA_KT)
        s_all = _nt_dot(kn_s[rows, :], q_all)
        negm = negm_s[rows, :]
        bidx = [jnp.clip(j - (DSA_TRIP_BLOCKS * kt + t), 0, 2) for t in range(DSA_TRIP_BLOCKS)]
        tops = []
        for ls in heads:
            bias = jnp.concatenate([bias_ref[bi, :, ls] for bi in bidx], axis=0)
            s = s_all[:, ls] + bias + negm
            s_s[rows, ls] = s
            tops.append(_fold_rows(s, 8, jnp.maximum))
        return jnp.maximum(m8, jnp.concatenate(tops, axis=1))

    m8 = lax.fori_loop(0, nkt, score_body, jnp.full((8, C_HEADS * Q_BLOCK), NEG_INF, F32))
    m = jnp.max(m8, axis=0, keepdims=True)

    def prob_body(kt, l8):
        rows = pl.ds(trip_rows(kt), DSA_KT)
        p = jnp.exp(s_s[rows, :] - m)
        p_s[rows, :] = p.astype(MXU_DTYPE)
        return l8 + _fold_rows(p, 8)

    l8 = lax.fori_loop(0, nkt, prob_body, jnp.zeros((8, C_HEADS * Q_BLOCK), F32))
    denom = jnp.sum(l8, axis=0, keepdims=True)

    acc_s[...] = jnp.zeros(acc_s.shape, F32)

    def value_body(kt, _):
        rows = pl.ds(trip_rows(kt), DSA_KT)
        vtc = vt_s[:, rows]
        for h, ls in enumerate(heads):
            rs = slice(h * HEAD_DIM, (h + 1) * HEAD_DIM)
            acc_s[rs, :] = acc_s[rs, :] + _dot(vtc, p_s[rows, ls])
        return 0

    lax.fori_loop(0, nkt, value_body, 0)
    out_t = jnp.concatenate(
        [acc_s[h * HEAD_DIM:(h + 1) * HEAD_DIM, :] / denom[:, ls] for h, ls in enumerate(heads)], axis=0)
    o_ref[0] = out_t.T


def _t5_bucket(rel):
    half = T5_BUCKETS // 2
    max_exact = half // 2
    n = jnp.abs(rel)
    log_val = jnp.log(jnp.maximum(n, 1).astype(F32) / max_exact) / math.log(T5_MAX_DIST / max_exact)
    large = jnp.minimum(max_exact + (log_val * (half - max_exact)).astype(jnp.int32), half - 1)
    return jnp.where(rel > 0, half, 0) + jnp.where(n < max_exact, n, large)


def _dsa_bias_tiles(t5_table):
    s = jnp.arange(Q_BLOCK)[:, None]
    t = jnp.arange(Q_BLOCK)[None, :]
    tiles = []
    for d in range(3):
        rel = s - d * Q_BLOCK - t
        tile = t5_table.astype(F32)[_t5_bucket(rel)]
        tiles.append(tile.transpose(0, 2, 1).reshape(Q_BLOCK, C_HEADS * Q_BLOCK))
    return jnp.stack(tiles)


def _dsa(z3, bias, gq, gk):
    b, seq, _ = z3.shape
    assert seq % DSA_KT == 0
    topk = min(TOPK_MAX, seq // 4)
    kern = functools.partial(_dsa_kernel, seq=seq, topk=topk)
    return pl.pallas_call(
        kern,
        grid=(b, seq // Q_BLOCK),
        in_specs=[
            pl.BlockSpec((1, Q_BLOCK, C_WIDTH), lambda i, j: (i, j, COL_CQ // C_WIDTH)),
            pl.BlockSpec((1, seq, 128), lambda i, j: (i, 0, COL_CKV // 128)),
            pl.BlockSpec((1, Q_BLOCK, 256), lambda i, j: (i, j, COL_QI // 256)),
            pl.BlockSpec((1, Q_BLOCK, 128), lambda i, j: (i, j, COL_KIW // 128)),
            pl.BlockSpec((1, seq, 128), lambda i, j: (i, 0, COL_KIW // 128)),
            pl.BlockSpec((1, HEAD_DIM), lambda i, j: (0, 0)),
            pl.BlockSpec((1, HEAD_DIM), lambda i, j: (0, 0)),
            pl.BlockSpec((3, Q_BLOCK, C_HEADS * Q_BLOCK), lambda i, j: (0, 0, 0)),
        ],
        out_specs=[
            pl.BlockSpec((1, Q_BLOCK, C_WIDTH), lambda i, j: (i, j, 0)),
            pl.BlockSpec((1, seq, HEAD_DIM), lambda i, j: (i, 0, 0)),
            pl.BlockSpec((1, seq, HEAD_DIM), lambda i, j: (i, 0, 0)),
            pl.BlockSpec((1, seq, IDX_DIM), lambda i, j: (i, 0, 0)),
        ],
        out_shape=[
            jax.ShapeDtypeStruct((b, seq, C_WIDTH), F32),
            jax.ShapeDtypeStruct((b, seq, HEAD_DIM), F32),
            jax.ShapeDtypeStruct((b, seq, HEAD_DIM), F32),
            jax.ShapeDtypeStruct((b, seq, IDX_DIM), F32),
        ],
        scratch_shapes=[
            pltpu.VMEM((seq, HEAD_DIM), MXU_DTYPE),
            pltpu.VMEM((HEAD_DIM, seq), MXU_DTYPE),
            pltpu.VMEM((seq, IDX_DIM), MXU_DTYPE),
            pltpu.VMEM((seq, Q_BLOCK), jnp.int32),
            pltpu.VMEM((seq, Q_BLOCK), jnp.int16),
            pltpu.VMEM((seq, Q_BLOCK), jnp.int16),
            pltpu.VMEM((seq, Q_BLOCK), F32),
            pltpu.VMEM((seq, C_HEADS * Q_BLOCK), F32),
            pltpu.VMEM((seq, C_HEADS * Q_BLOCK), MXU_DTYPE),
            pltpu.VMEM((C_WIDTH, Q_BLOCK), F32),
        ],
        compiler_params=pltpu.CompilerParams(
            dimension_semantics=("parallel", "arbitrary"), vmem_limit_bytes=VMEM_LIMIT),
        name="dsa",
    )(z3, z3, z3, z3, z3, gq, gk, bias)


def _mix_residual(x_ref, oa_ref, ob_ref, oc_ref, wo_ref):
    mix = (_dot(oa_ref[...].astype(MXU_DTYPE), wo_ref[0:A_WIDTH, :])
           + _dot(ob_ref[...].astype(MXU_DTYPE), wo_ref[A_WIDTH:A_WIDTH + S5_WIDTH, :])
           + _dot(oc_ref[...].astype(MXU_DTYPE), wo_ref[A_WIDTH + S5_WIDTH:, :]))
    return x_ref[...] + mix


def _ffn_chunk(xn, g, g1, g2, sl, wu_ref, cw_ref, cb_ref, wd_ref):
    up = _dot(xn, wu_ref[:, sl])
    gc = cw_ref[0:1, sl] * g2 + cw_ref[1:2, sl] * g1 + cw_ref[2:3, sl] * g + cb_ref[:, sl]
    return _dot((jax.nn.silu(gc) * up).astype(MXU_DTYPE), wd_ref[sl, :])


def _out_ffn_kernel(x_ref, oa_ref, ob_ref, oc_ref, wo_ref, g2_ref, wg_ref, wu_ref, cw_ref, cb_ref, wd_ref,
                    y_ref, conv_ref, carry_s, *, tiles_per_seq, tf):
    i = pl.program_id(0)
    tm = x_ref.shape[0]
    x1 = _mix_residual(x_ref, oa_ref, ob_ref, oc_ref, wo_ref)
    xn = _rms(x1, g2_ref[...]).astype(MXU_DTYPE)

    @pl.when(i % tiles_per_seq == 0)
    def _seq_start():
        carry_s[...] = jnp.zeros(carry_s.shape, F32)

    row = lax.broadcasted_iota(jnp.int32, (tm, 1), 0)
    acc = x1
    for f in range(FFN_DIM // tf):
        sl = slice(f * tf, (f + 1) * tf)
        g = _dot(xn, wg_ref[:, sl])
        prev2, prev1 = carry_s[0:1, sl], carry_s[1:2, sl]
        g1 = jnp.where(row == 0, prev1, pltpu.roll(g, 1, 0))
        g2 = jnp.where(row == 0, prev2, jnp.where(row == 1, prev1, pltpu.roll(g, 2, 0)))
        acc = acc + _ffn_chunk(xn, g, g1, g2, sl, wu_ref, cw_ref, cb_ref, wd_ref)
        carry_s[0:2, sl] = g[tm - 2:tm, :]
    y_ref[...] = acc
    conv_ref[0] = carry_s[0:2, :]


def _out_ffn(x2d, oa, ob, oc, wo, g2, wg, wu, cw, cb, wd, *, seq, tm, tf):
    m = x2d.shape[0]
    tiles_per_seq = seq // tm
    kern = functools.partial(_out_ffn_kernel, tiles_per_seq=tiles_per_seq, tf=tf)
    row = lambda w: pl.BlockSpec((tm, w), lambda i: (i, 0))
    const = lambda shape: pl.BlockSpec(shape, lambda i: (0,) * len(shape), pipeline_mode=pl.Buffered(1))
    return pl.pallas_call(
        kern,
        grid=(m // tm,),
        in_specs=[
            row(D_MODEL), row(A_WIDTH), row(S5_WIDTH), row(C_WIDTH),
            const((D_MODEL, D_MODEL)), const((1, D_MODEL)),
            const((D_MODEL, FFN_DIM)), const((D_MODEL, FFN_DIM)),
            const((CONV_W, FFN_DIM)), const((1, FFN_DIM)), const((FFN_DIM, D_MODEL)),
        ],
        out_specs=[
            row(D_MODEL),
            pl.BlockSpec((1, CONV_W - 1, FFN_DIM), lambda i: (i // tiles_per_seq, 0, 0)),
        ],
        out_shape=[
            jax.ShapeDtypeStruct((m, D_MODEL), F32),
            jax.ShapeDtypeStruct((m // seq, CONV_W - 1, FFN_DIM), F32),
        ],
        scratch_shapes=[pltpu.VMEM((8, FFN_DIM), F32)],
        compiler_params=pltpu.CompilerParams(dimension_semantics=("arbitrary",), vmem_limit_bytes=VMEM_LIMIT),
        name="out_ffn",
    )(x2d, oa, ob, oc, wo, g2, wg, wu, cw, cb, wd)


def _layer_params(l, t5_bias, norm1_g, w_in, a_q_gain, a_k_gain, a_rel_bias, s5_lam_re, s5_lam_im, s5_log_dt,
                  s5_b_re, s5_b_im, s5_c_re, s5_c_im, s5_d, s5_w_glu, s5_b_glu, c_q_gain, c_k_gain, w_out,
                  norm2_g, ffn_w_gate, ffn_w_up, ffn_conv_w, ffn_conv_b, ffn_w_down):
    w = w_in[l]
    sizes = (A_WIDTH, A_WIDTH, A_WIDTH, S5_WIDTH, C_WIDTH, HEAD_DIM, HEAD_DIM, IDX_HEADS * IDX_DIM, IDX_DIM, IDX_HEADS)
    cuts = [0]
    for s in sizes:
        cuts.append(cuts[-1] + s)
    aq, ak, av, u, cq, ck, cv, qi, ki, wi = [w[:, cuts[n]:cuts[n + 1]] for n in range(len(sizes))]
    pad = jnp.zeros((D_MODEL, Z_WIDTH - COL_KIW - IDX_DIM - IDX_HEADS), w.dtype)
    w_r = jnp.concatenate([aq, ak, av, cq, u, qi, ck, cv, ki, wi, pad], axis=1).astype(MXU_DTYPE)
    return dict(
        norm1=norm1_g[l].reshape(1, D_MODEL), w_in=w_r,
        a_gq=a_q_gain[l].reshape(1, HEAD_DIM), a_gk=a_k_gain[l].reshape(1, HEAD_DIM), a_rel=a_rel_bias[l],
        s5=_s5_params(s5_lam_re[l], s5_lam_im[l], s5_log_dt[l], s5_b_re[l], s5_b_im[l], s5_c_re[l], s5_c_im[l]),
        s5_d=s5_d[l].reshape(1, S5_WIDTH), s5_wg=s5_w_glu[l].astype(MXU_DTYPE), s5_bg=s5_b_glu[l].reshape(1, S5_WIDTH),
        c_gq=c_q_gain[l].reshape(1, HEAD_DIM), c_gk=c_k_gain[l].reshape(1, HEAD_DIM),
        w_out=w_out[l].astype(MXU_DTYPE), norm2=norm2_g[l].reshape(1, D_MODEL),
        wg=ffn_w_gate[l].astype(MXU_DTYPE), wu=ffn_w_up[l].astype(MXU_DTYPE), cw=ffn_conv_w[l],
        cb=ffn_conv_b[l].reshape(1, FFN_DIM), wd=ffn_w_down[l].astype(MXU_DTYPE),
    )


def _prompt_layer(x, p, dsa_bias, *, tm, tf, s5_bt, s5_tc):
    b, seq, _ = x.shape
    x2d = x.reshape(b * seq, D_MODEL)
    z3 = _in_proj(x2d, p["norm1"], p["w_in"], tm).reshape(b, seq, Z_WIDTH)
    oa, a_k, a_v = _attn_a(z3, _attn_a_bias(p["a_rel"]), p["a_gq"], p["a_gk"])
    zero_h = jnp.zeros((b, S5_FLAT), F32)
    ob, h_re, h_im = _s5(z3, zero_h, zero_h, p["s5"], p["s5_d"], p["s5_wg"], p["s5_bg"],
                         bt=s5_bt, tc=s5_tc, exact_in=False)
    oc, c_k, c_v, c_ki = _dsa(z3, dsa_bias, p["c_gq"], p["c_gk"])
    y, conv = _out_ffn(x2d, oa.reshape(b * seq, A_WIDTH), ob.reshape(b * seq, S5_WIDTH),
                       oc.reshape(b * seq, C_WIDTH), p["w_out"], p["norm2"], p["wg"], p["wu"], p["cw"], p["cb"],
                       p["wd"], seq=seq, tm=tm, tf=tf)
    keep = a_k.shape[1]
    states = (a_k.reshape(b, keep, A_HEADS, HEAD_DIM), a_v.reshape(b, keep, A_HEADS, HEAD_DIM),
              h_re.reshape(b, S5_GROUPS, S5_STATE), h_im.reshape(b, S5_GROUPS, S5_STATE), c_k, c_v, c_ki, conv)
    return y.reshape(b, seq, D_MODEL), states


def _attn_a_step_kernel(q_ref, k_ref, v_ref, ck_ref, cv_ref, bc_ref, bn_ref, gq_ref, gk_ref, o_ref, ak_ref, av_ref):
    v = v_ref[0]
    av_ref[0] = v
    for h in range(A_HEADS):
        sl = slice(h * HEAD_DIM, (h + 1) * HEAD_DIM)
        qh = (_rms(q_ref[0, :, sl], gq_ref[...]) * HEAD_DIM ** -0.5).astype(MXU_DTYPE)
        khn = _rms(k_ref[0, :, sl], gk_ref[...])
        ak_ref[0, :, sl] = khn
        s_c = _nt_dot(qh, ck_ref[0, :, sl].astype(MXU_DTYPE)) + bc_ref[h]
        s_n = _nt_dot(qh, khn.astype(MXU_DTYPE)) + bn_ref[h]
        m = jnp.maximum(jnp.max(s_c, axis=-1, keepdims=True), jnp.max(s_n, axis=-1, keepdims=True))
        p_c = jnp.exp(s_c - m)
        p_n = jnp.exp(s_n - m)
        denom = jnp.sum(p_c, axis=-1, keepdims=True) + jnp.sum(p_n, axis=-1, keepdims=True)
        oh = (_dot(p_c.astype(MXU_DTYPE), cv_ref[0, :, sl].astype(MXU_DTYPE))
              + _dot(p_n.astype(MXU_DTYPE), v[:, sl].astype(MXU_DTYPE)))
        o_ref[0, :, sl] = oh / denom


def _attn_a_step_bias(rel_table, past, t_new):
    t = jnp.arange(t_new)[:, None]
    rel = jnp.concatenate([jnp.arange(past) - past, jnp.arange(t_new)])[None, :] - t
    bias = rel_table.astype(F32)[:, jnp.clip(rel, -A_MAX_REL, A_MAX_REL) + A_MAX_REL]
    return bias[:, :, :past], bias[:, :, past:]


def _attn_a_step(z3, cache_k, cache_v, layer, bias_c, bias_n, gq, gk):
    b, t_new, _ = z3.shape
    past = cache_k.shape[2]
    new = lambda col: pl.BlockSpec((1, t_new, A_WIDTH), lambda i: (i, 0, col // A_WIDTH))
    const = lambda shape: pl.BlockSpec(shape, lambda i: (0,) * len(shape))
    return pl.pallas_call(
        _attn_a_step_kernel,
        grid=(b,),
        in_specs=[
            new(COL_AQ), new(COL_AK), new(COL_AV),
            pl.BlockSpec((None, 1, past, A_WIDTH), lambda i: (layer, i, 0, 0)),
            pl.BlockSpec((None, 1, past, A_WIDTH), lambda i: (layer, i, 0, 0)),
            const((A_HEADS, t_new, past)), const((A_HEADS, t_new, t_new)),
            const((1, HEAD_DIM)), const((1, HEAD_DIM)),
        ],
        out_specs=[pl.BlockSpec((1, t_new, A_WIDTH), lambda i: (i, 0, 0))] * 3,
        out_shape=[jax.ShapeDtypeStruct((b, t_new, A_WIDTH), F32)] * 3,
        compiler_params=pltpu.CompilerParams(dimension_semantics=("parallel",), vmem_limit_bytes=VMEM_LIMIT),
        name="attn_a_step",
    )(z3, z3, z3, cache_k, cache_v, bias_c, bias_n, gq, gk)


def _dsa_step_kernel(cq_ref, ckv_ref, qi_ref, kiw_ref, ck_ref, cv_ref, cki_ref, gq_ref, gk_ref, bias_ref,
                     o_ref, ck_out, cv_out, cki_out, *, past, t_new, topk):
    n_keys = past + t_new
    kn_new = _rms(ckv_ref[0, :, 0:HEAD_DIM], gk_ref[...])
    cv_new = ckv_ref[0, :, HEAD_DIM:2 * HEAD_DIM]
    ki_new = kiw_ref[0, :, 0:IDX_DIM]
    ck_out[0] = kn_new
    cv_out[0] = cv_new
    cki_out[0] = ki_new
    k_all = jnp.concatenate([ck_ref[0], kn_new], axis=0).astype(MXU_DTYPE)
    v_all = jnp.concatenate([cv_ref[0], cv_new], axis=0).astype(MXU_DTYPE)
    ki_all = jnp.concatenate([cki_ref[0], ki_new], axis=0).astype(MXU_DTYPE)
    eye = (lax.broadcasted_iota(jnp.int32, (HEAD_DIM, HEAD_DIM), 0)
           == lax.broadcasted_iota(jnp.int32, (HEAD_DIM, HEAD_DIM), 1)).astype(MXU_DTYPE)
    v_t = _nt_dot(eye, v_all).astype(MXU_DTYPE)

    qi = qi_ref[0].astype(MXU_DTYPE)
    kiw_pad = jnp.concatenate([kiw_ref[0], jnp.zeros((128 - t_new, 128), F32)], axis=0)
    w_t = kiw_pad.T[:, 0:t_new]
    score = jnp.zeros((n_keys, t_new), F32)
    for h in range(IDX_HEADS):
        dots = _nt_dot(ki_all, qi[:, h * IDX_DIM:(h + 1) * IDX_DIM])
        score = score + (w_t[IDX_DIM + h:IDX_DIM + h + 1, :] * INDEX_SCALE) * jnp.maximum(dots, 0.0)
    pos = lax.broadcasted_iota(jnp.int32, (n_keys, 1), 0)
    q_chunk = (past + lax.broadcasted_iota(jnp.int32, (1, t_new), 1)) // CHUNK
    keys = jnp.where((pos // CHUNK) <= q_chunk, _sortable_key(score), jnp.int32(INT_MIN))

    def count(pred):
        return jnp.sum(jnp.where(pred(keys, pos), 1, 0), axis=0, keepdims=True)

    def kth_body(it, res):
        cand = res | lax.shift_left(jnp.int32(1), jnp.int32(31) - it)
        thr_c = cand ^ jnp.int32(INT_MIN)
        return jnp.where(count(lambda k, p: k >= thr_c) >= topk, cand, res)

    thr = lax.fori_loop(0, 32, kth_body, jnp.zeros((1, t_new), jnp.int32)) ^ jnp.int32(INT_MIN)
    real_thr = thr != jnp.int32(INT_MIN)
    need = topk - count(lambda k, p: k > thr)
    pos_bits = (n_keys - 1).bit_length()

    def tie_body(it, v):
        cand = v | lax.shift_left(jnp.int32(1), jnp.int32(pos_bits - 1) - it)
        return jnp.where(count(lambda k, p: (k == thr) & (p < cand)) < need, cand, v)

    tie_last = lax.fori_loop(0, pos_bits, tie_body, jnp.zeros((1, t_new), jnp.int32))
    tie_last = jnp.where(real_thr, tie_last, -1)
    negm = jnp.where((keys > thr) | ((keys == thr) & (pos <= tie_last)), 0.0, NEG_INF)

    cq = cq_ref[0]
    outs = []
    for h in range(C_HEADS):
        qh = (_rms(cq[:, h * HEAD_DIM:(h + 1) * HEAD_DIM], gq_ref[...]) * HEAD_DIM ** -0.5).astype(MXU_DTYPE)
        s = _nt_dot(k_all, qh) + bias_ref[h] + negm
        p = jnp.exp(s - jnp.max(s, axis=0, keepdims=True))
        outs.append(_dot(v_t, p.astype(MXU_DTYPE)) / jnp.sum(p, axis=0, keepdims=True))
    out_t = jnp.concatenate(outs, axis=0)
    out_pad = jnp.concatenate([out_t, jnp.zeros((C_WIDTH, 128 - t_new), F32)], axis=1)
    o_ref[0] = out_pad.T[0:t_new, :]


def _dsa_step_bias(t5_table, past, t_new):
    s = jnp.arange(past + t_new)[:, None]
    t = jnp.arange(t_new)[None, :]
    return t5_table.astype(F32)[_t5_bucket(s - (past + t))].transpose(2, 0, 1)


def _dsa_step(z3, cache_k, cache_v, cache_ki, layer, bias, gq, gk):
    b, t_new, _ = z3.shape
    past = cache_k.shape[2]
    topk = min(TOPK_MAX, (past + t_new) // 4)
    kern = functools.partial(_dsa_step_kernel, past=past, t_new=t_new, topk=topk)
    new = lambda w, col: pl.BlockSpec((1, t_new, w), lambda i: (i, 0, col // w))
    per_seq = lambda n, w: pl.BlockSpec((1, n, w), lambda i: (i, 0, 0))
    cached = lambda w: pl.BlockSpec((None, 1, past, w), lambda i: (layer, i, 0, 0))
    const = lambda shape: pl.BlockSpec(shape, lambda i: (0,) * len(shape))
    return pl.pallas_call(
        kern,
        grid=(b,),
        in_specs=[
            new(C_WIDTH, COL_CQ), new(128, COL_CKV), new(256, COL_QI), new(128, COL_KIW),
            cached(HEAD_DIM), cached(HEAD_DIM), cached(IDX_DIM),
            const((1, HEAD_DIM)), const((1, HEAD_DIM)), const((C_HEADS, past + t_new, t_new)),
        ],
        out_specs=[per_seq(t_new, C_WIDTH), per_seq(t_new, HEAD_DIM), per_seq(t_new, HEAD_DIM),
                   per_seq(t_new, IDX_DIM)],
        out_shape=[
            jax.ShapeDtypeStruct((b, t_new, C_WIDTH), F32),
            jax.ShapeDtypeStruct((b, t_new, HEAD_DIM), F32),
            jax.ShapeDtypeStruct((b, t_new, HEAD_DIM), F32),
            jax.ShapeDtypeStruct((b, t_new, IDX_DIM), F32),
        ],
        compiler_params=pltpu.CompilerParams(dimension_semantics=("parallel",), vmem_limit_bytes=VMEM_LIMIT),
        name="dsa_step",
    )(z3, z3, z3, z3, cache_k, cache_v, cache_ki, gq, gk, bias)


def _out_ffn_step_kernel(x_ref, oa_ref, ob_ref, oc_ref, wo_ref, g2_ref, wg_ref, wu_ref, cw_ref, cb_ref, wd_ref,
                         e1_ref, e2_ref, y_ref, gate_ref, *, t_new, tf):
    tm = x_ref.shape[0]
    x1 = _mix_residual(x_ref, oa_ref, ob_ref, oc_ref, wo_ref)
    xn = _rms(x1, g2_ref[...]).astype(MXU_DTYPE)
    t = lax.broadcasted_iota(jnp.int32, (tm, 1), 0) % t_new
    acc = x1
    for f in range(FFN_DIM // tf):
        sl = slice(f * tf, (f + 1) * tf)
        g = _dot(xn, wg_ref[:, sl])
        gate_ref[:, sl] = g
        g1 = jnp.where(t == 0, e1_ref[:, sl], pltpu.roll(g, 1, 0))
        g2 = jnp.where(t <= 1, e2_ref[:, sl], pltpu.roll(g, 2, 0))
        acc = acc + _ffn_chunk(xn, g, g1, g2, sl, wu_ref, cw_ref, cb_ref, wd_ref)
    y_ref[...] = acc


def _out_ffn_step(x2d, oa, ob, oc, wo, g2, wg, wu, cw, cb, wd, conv_prev, *, t_new, tf):
    m = x2d.shape[0]
    b = m // t_new
    e1 = jnp.zeros((b, t_new, FFN_DIM), F32).at[:, 0].set(conv_prev[:, 1]).reshape(m, FFN_DIM)
    e2 = jnp.zeros((b, t_new, FFN_DIM), F32).at[:, 0].set(conv_prev[:, 0]).at[:, 1].set(conv_prev[:, 1])
    e2 = e2.reshape(m, FFN_DIM)
    kern = functools.partial(_out_ffn_step_kernel, t_new=t_new, tf=tf)
    full = lambda shape: pl.BlockSpec(shape, lambda i: (0,) * len(shape))
    y, gate = pl.pallas_call(
        kern,
        grid=(1,),
        in_specs=[
            full((m, D_MODEL)), full((m, A_WIDTH)), full((m, S5_WIDTH)), full((m, C_WIDTH)),
            full((D_MODEL, D_MODEL)), full((1, D_MODEL)),
            full((D_MODEL, FFN_DIM)), full((D_MODEL, FFN_DIM)),
            full((CONV_W, FFN_DIM)), full((1, FFN_DIM)), full((FFN_DIM, D_MODEL)),
            full((m, FFN_DIM)), full((m, FFN_DIM)),
        ],
        out_specs=[full((m, D_MODEL)), full((m, FFN_DIM))],
        out_shape=[jax.ShapeDtypeStruct((m, D_MODEL), F32), jax.ShapeDtypeStruct((m, FFN_DIM), F32)],
        compiler_params=pltpu.CompilerParams(dimension_semantics=("arbitrary",), vmem_limit_bytes=VMEM_LIMIT),
        name="out_ffn_step",
    )(x2d, oa, ob, oc, wo, g2, wg, wu, cw, cb, wd, e1, e2)
    return y, gate.reshape(b, t_new, FFN_DIM)[:, t_new - (CONV_W - 1):]


def _sample_layer(x, p, dsa_bias, layer, ca_k, ca_v, h_re0, h_im0, cc_k, cc_v, cc_ki, conv_prev, *, tf):
    b, t_new, _ = x.shape
    m = b * t_new
    x2d = x.reshape(m, D_MODEL)
    z3 = _in_proj(x2d, p["norm1"], p["w_in"], m).reshape(b, t_new, Z_WIDTH)
    depth, _, a_past = ca_k.shape[:3]
    bias_c, bias_n = _attn_a_step_bias(p["a_rel"], a_past, t_new)
    oa, a_k, a_v = _attn_a_step(z3, ca_k.reshape(depth, b, a_past, A_WIDTH), ca_v.reshape(depth, b, a_past, A_WIDTH),
                                layer, bias_c, bias_n, p["a_gq"], p["a_gk"])
    ob, h_re, h_im = _s5(z3, h_re0.reshape(b, S5_FLAT), h_im0.reshape(b, S5_FLAT), p["s5"], p["s5_d"], p["s5_wg"],
                         p["s5_bg"], bt=8, tc=t_new, exact_in=True)
    oc, c_k, c_v, c_ki = _dsa_step(z3, cc_k, cc_v, cc_ki, layer, dsa_bias, p["c_gq"], p["c_gk"])
    y, conv = _out_ffn_step(x2d, oa.reshape(m, A_WIDTH), ob.reshape(m, S5_WIDTH), oc.reshape(m, C_WIDTH),
                            p["w_out"], p["norm2"], p["wg"], p["wu"], p["cw"], p["cb"], p["wd"], conv_prev,
                            t_new=t_new, tf=tf)
    states = (a_k.reshape(b, t_new, A_HEADS, HEAD_DIM), a_v.reshape(b, t_new, A_HEADS, HEAD_DIM),
              h_re.reshape(b, S5_GROUPS, S5_STATE), h_im.reshape(b, S5_GROUPS, S5_STATE), c_k, c_v, c_ki, conv)
    return y.reshape(b, t_new, D_MODEL), states


def kernel(x_prompt, x_sample, cache_a_k, cache_a_v, state_s5_re, state_s5_im, cache_c_k, cache_c_v, cache_c_idx_k,
           state_ffn_conv, t5_bias, norm1_g, w_in, a_q_gain, a_k_gain, a_rel_bias, s5_lam_re, s5_lam_im, s5_log_dt,
           s5_b_re, s5_b_im, s5_c_re, s5_c_im, s5_d, s5_w_glu, s5_b_glu, c_q_gain, c_k_gain, w_out, norm2_g,
           ffn_w_gate, ffn_w_up, ffn_conv_w, ffn_conv_b, ffn_w_down):
    depth = w_in.shape[0]
    dsa_bias = _dsa_bias_tiles(t5_bias)
    dsa_step_bias = _dsa_step_bias(t5_bias, cache_c_k.shape[2], x_sample.shape[1])
    yp, ys = x_prompt, x_sample
    prompt_states, sample_states = [], []
    for l in range(depth):
        p = _layer_params(l, t5_bias, norm1_g, w_in, a_q_gain, a_k_gain, a_rel_bias, s5_lam_re, s5_lam_im, s5_log_dt,
                          s5_b_re, s5_b_im, s5_c_re, s5_c_im, s5_d, s5_w_glu, s5_b_glu, c_q_gain, c_k_gain, w_out,
                          norm2_g, ffn_w_gate, ffn_w_up, ffn_conv_w, ffn_conv_b, ffn_w_down)
        yp, st_p = _prompt_layer(yp, p, dsa_bias, tm=512, tf=FFN_DIM, s5_bt=8, s5_tc=256)
        ys, st_s = _sample_layer(ys, p, dsa_step_bias, l, cache_a_k, cache_a_v, state_s5_re[l], state_s5_im[l],
                                 cache_c_k, cache_c_v, cache_c_idx_k, state_ffn_conv[l], tf=FFN_DIM)
        prompt_states.append(st_p)
        sample_states.append(st_s)
    (a_k_p, a_v_p, s5_re_p, s5_im_p, c_k_p, c_v_p, c_ki_p, conv_p) = [jnp.stack(z) for z in zip(*prompt_states)]
    (a_k_s, a_v_s, s5_re_s, s5_im_s, c_k_s, c_v_s, c_ki_s, conv_s) = [jnp.stack(z) for z in zip(*sample_states)]
    return (yp, ys, a_k_p, a_v_p, a_k_s, a_v_s, s5_re_p, s5_im_p, s5_re_s, s5_im_s,
            c_k_p, c_v_p, c_ki_p, c_k_s, c_v_s, c_ki_s, conv_p, conv_s)
```

```python
import functools
import math

import jax
import jax.numpy as jnp
from jax import lax
from jax.experimental import pallas as pl
from jax.experimental.pallas import tpu as pltpu

F32 = jnp.float32
MXU_DTYPE = jnp.bfloat16

D_MODEL = 1024
CHUNK = 64
HEAD_DIM = 64
A_HEADS = 6
A_WIDTH = A_HEADS * HEAD_DIM
A_BAND_PAST = 8 * CHUNK
A_MAX_REL = 128
S5_GROUPS = 16
S5_GROUP_CH = 16
S5_WIDTH = S5_GROUPS * S5_GROUP_CH
S5_STATE = 64
S5_FLAT = S5_GROUPS * S5_STATE
C_HEADS = 6
C_WIDTH = C_HEADS * HEAD_DIM
IDX_HEADS = 8
IDX_DIM = 32
INDEX_SCALE = (IDX_HEADS * IDX_DIM) ** -0.5
TOPK_MAX = 256
Q_BLOCK = 128
T5_BUCKETS = 32
T5_MAX_DIST = 128
FFN_DIM = 2816
CONV_W = 3
EPS = 1e-6
NEG_INF = -1e30
INT_MIN = -(2 ** 31)

COL_AQ, COL_AK, COL_AV, COL_CQ = 0, 384, 768, 1152
COL_U, COL_QI, COL_CKV, COL_KIW = 1536, 1792, 2048, 2176
Z_WIDTH = 2304
VMEM_LIMIT = 56 * 1024 * 1024


def _nt_dot(a, b):
    return lax.dot_general(a, b, (((1,), (1,)), ((), ())), preferred_element_type=F32)


def _dot(a, b):
    return jnp.dot(a, b, preferred_element_type=F32)


def _rms(x, g):
    return x * lax.rsqrt(jnp.mean(x * x, axis=-1, keepdims=True) + EPS) * g


def _in_proj_kernel(x_ref, g_ref, w_ref, z_ref):
    xn = _rms(x_ref[...], g_ref[...]).astype(MXU_DTYPE)
    z_ref[...] = _dot(xn, w_ref[...])


def _in_proj(x2d, g, w_r, tm):
    m = x2d.shape[0]
    return pl.pallas_call(
        _in_proj_kernel,
        grid=(m // tm,),
        in_specs=[
            pl.BlockSpec((tm, D_MODEL), lambda i: (i, 0)),
            pl.BlockSpec((1, D_MODEL), lambda i: (0, 0)),
            pl.BlockSpec((D_MODEL, Z_WIDTH), lambda i: (0, 0)),
        ],
        out_specs=pl.BlockSpec((tm, Z_WIDTH), lambda i: (i, 0)),
        out_shape=jax.ShapeDtypeStruct((m, Z_WIDTH), F32),
        compiler_params=pltpu.CompilerParams(dimension_semantics=("parallel",), vmem_limit_bytes=VMEM_LIMIT),
        name="in_proj",
    )(x2d, g, w_r)


A_QB = 2 * CHUNK
A_BAND = A_BAND_PAST + A_QB


def _eye(n, dtype):
    return (lax.broadcasted_iota(jnp.int32, (n, n), 0) == lax.broadcasted_iota(jnp.int32, (n, n), 1)).astype(dtype)


def _attn_a_kernel(q_ref, k_ref, v_ref, bias_ref, gq_ref, gk_ref, o_ref, ak_ref, av_ref, kn_s, vt_s, s_s, p_s,
                   *, seq, keep):
    j = pl.program_id(1)

    @pl.when(j == 0)
    def _prep():
        for h in range(A_HEADS):
            sl = slice(h * HEAD_DIM, (h + 1) * HEAD_DIM)
            khn = _rms(k_ref[0, :, sl], gk_ref[...])
            kn_s[h, 0:A_BAND_PAST, :] = jnp.zeros((A_BAND_PAST, HEAD_DIM), MXU_DTYPE)
            kn_s[h, A_BAND_PAST:A_BAND_PAST + seq, :] = khn.astype(MXU_DTYPE)
            ak_ref[0, :, sl] = khn[seq - keep:, :]
        v = v_ref[0]
        av_ref[0] = v[seq - keep:, :]
        vt_s[:, 0:A_BAND_PAST] = jnp.zeros((A_WIDTH, A_BAND_PAST), MXU_DTYPE)
        vt_s[:, A_BAND_PAST:A_BAND_PAST + seq] = _nt_dot(_eye(A_WIDTH, MXU_DTYPE), v.astype(MXU_DTYPE)).astype(MXU_DTYPE)

    start = pl.multiple_of(j * A_QB, A_QB)
    variant = jnp.minimum(j, A_LEAD_STEPS)
    for h in range(A_HEADS):
        sl = slice(h * HEAD_DIM, (h + 1) * HEAD_DIM)
        qh = (_rms(q_ref[0, :, sl], gq_ref[...]) * HEAD_DIM ** -0.5).astype(MXU_DTYPE)
        s_s[h] = _nt_dot(kn_s[h, pl.ds(start, A_BAND), :], qh) + bias_ref[variant, h]
    denoms = []
    for h in range(A_HEADS):
        s = s_s[h]
        p = jnp.exp(s - jnp.max(s, axis=0, keepdims=True))
        denoms.append(jnp.sum(p, axis=0, keepdims=True))
        p_s[h] = p.astype(MXU_DTYPE)
    outs = [_dot(vt_s[h * HEAD_DIM:(h + 1) * HEAD_DIM, pl.ds(start, A_BAND)], p_s[h]) / denoms[h]
            for h in range(A_HEADS)]
    o_ref[0] = jnp.concatenate(outs, axis=0).T


A_LEAD_STEPS = A_BAND_PAST // A_QB


def _attn_a_bias(rel_table):
    n_ext = A_BAND + A_QB - 1
    m = jnp.arange(n_ext)
    ext = rel_table.astype(F32)[:, jnp.clip(m - (A_QB - 1) - A_BAND_PAST, -A_MAX_REL, A_MAX_REL) + A_MAX_REL]
    rows = jnp.pad(jnp.broadcast_to(ext[:, None, :], (A_HEADS, A_QB, n_ext)), ((0, 0), (0, 0), (0, 1)))
    skew = rows.reshape(A_HEADS, A_QB * (n_ext + 1))[:, :A_QB * n_ext].reshape(A_HEADS, A_QB, n_ext)
    bias = skew[:, :, A_QB - 1:].transpose(0, 2, 1)
    c = jnp.arange(A_BAND)[:, None]
    lo = (jnp.arange(A_QB)[None, :] // CHUNK) * CHUNK
    in_band = (c >= lo) & (c < lo + A_BAND_PAST + CHUNK)
    first_live = (A_LEAD_STEPS - jnp.arange(A_LEAD_STEPS + 1)) * A_QB
    live = in_band[None] & (c[None] >= first_live[:, None, None])
    return jnp.where(live[:, None], bias[None], NEG_INF)


def _attn_a(z3, bias, gq, gk):
    b, seq, _ = z3.shape
    keep = min(A_BAND_PAST, seq)
    kern = functools.partial(_attn_a_kernel, seq=seq, keep=keep)
    return pl.pallas_call(
        kern,
        grid=(b, seq // A_QB),
        in_specs=[
            pl.BlockSpec((1, A_QB, A_WIDTH), lambda i, j: (i, j, COL_AQ // A_WIDTH)),
            pl.BlockSpec((1, seq, A_WIDTH), lambda i, j: (i, 0, COL_AK // A_WIDTH)),
            pl.BlockSpec((1, seq, A_WIDTH), lambda i, j: (i, 0, COL_AV // A_WIDTH)),
            pl.BlockSpec((A_LEAD_STEPS + 1, A_HEADS, A_BAND, A_QB), lambda i, j: (0, 0, 0, 0),
                         pipeline_mode=pl.Buffered(1)),
            pl.BlockSpec((1, HEAD_DIM), lambda i, j: (0, 0)),
            pl.BlockSpec((1, HEAD_DIM), lambda i, j: (0, 0)),
        ],
        out_specs=[
            pl.BlockSpec((1, A_QB, A_WIDTH), lambda i, j: (i, j, 0)),
            pl.BlockSpec((1, keep, A_WIDTH), lambda i, j: (i, 0, 0)),
            pl.BlockSpec((1, keep, A_WIDTH), lambda i, j: (i, 0, 0)),
        ],
        out_shape=[
            jax.ShapeDtypeStruct((b, seq, A_WIDTH), F32),
            jax.ShapeDtypeStruct((b, keep, A_WIDTH), F32),
            jax.ShapeDtypeStruct((b, keep, A_WIDTH), F32),
        ],
        scratch_shapes=[
            pltpu.VMEM((A_HEADS, A_BAND_PAST + seq, HEAD_DIM), MXU_DTYPE),
            pltpu.VMEM((A_WIDTH, A_BAND_PAST + seq), MXU_DTYPE),
            pltpu.VMEM((A_HEADS, A_BAND, A_QB), F32),
            pltpu.VMEM((A_HEADS, A_BAND, A_QB), MXU_DTYPE),
        ],
        compiler_params=pltpu.CompilerParams(
            dimension_semantics=("parallel", "arbitrary"), vmem_limit_bytes=VMEM_LIMIT),
        name="attn_a",
    )(z3, z3, z3, bias, gq, gk)


def _s5_kernel(u_ref, h0r_ref, h0i_ref, ar_ref, ai_ref, wbr_ref, wbi_ref, wcr_ref, wci_ref, d_ref, wg_ref, bg_ref,
               o_ref, hr_out, hi_out, xr_s, xi_s, hr_c, hi_c, *, bt, tc, exact_in):
    c = pl.program_id(1)

    @pl.when(c == 0)
    def _init():
        hr_c[...] = h0r_ref[...]
        hi_c[...] = h0i_ref[...]

    u = jnp.swapaxes(u_ref[...], 0, 1).reshape(tc * bt, S5_WIDTH)
    if exact_in:
        xr_s[...] = jnp.dot(u, wbr_ref[...], preferred_element_type=F32, precision=lax.Precision.HIGHEST)
        xi_s[...] = jnp.dot(u, wbi_ref[...], preferred_element_type=F32, precision=lax.Precision.HIGHEST)
    else:
        ub = u.astype(MXU_DTYPE)
        xr_s[...] = _dot(ub, wbr_ref[...])
        xi_s[...] = _dot(ub, wbi_ref[...])
    ar = ar_ref[...]
    ai = ai_ref[...]

    def step(t, carry):
        hr, hi = carry
        rows = pl.ds(pl.multiple_of(t * bt, bt), bt)
        nhr = ar * hr - ai * hi + xr_s[rows, :]
        nhi = ar * hi + ai * hr + xi_s[rows, :]
        xr_s[rows, :] = nhr
        xi_s[rows, :] = nhi
        return nhr, nhi

    hr, hi = lax.fori_loop(0, tc, step, (hr_c[...], hi_c[...]))
    hr_c[...] = hr
    hi_c[...] = hi
    hr_out[...] = hr
    hi_out[...] = hi
    y = (_dot(xr_s[...].astype(MXU_DTYPE), wcr_ref[...]) - _dot(xi_s[...].astype(MXU_DTYPE), wci_ref[...])
         + d_ref[...] * u)
    g = jax.nn.gelu(y)
    out = g * jax.nn.sigmoid(_dot(g.astype(MXU_DTYPE), wg_ref[...]) + bg_ref[...])
    o_ref[...] = jnp.swapaxes(out.reshape(tc, bt, S5_WIDTH), 0, 1)


def _s5_params(lam_re, lam_im, log_dt, b_re, b_im, c_re, c_im):
    dt = jnp.exp(log_dt.astype(F32))[:, None]
    lr, li = lam_re.astype(F32), lam_im.astype(F32)
    mag = jnp.exp(lr * dt)
    ab_re, ab_im = mag * jnp.cos(li * dt), mag * jnp.sin(li * dt)
    den = lr * lr + li * li
    cr = ((ab_re - 1.0) * lr + ab_im * li) / den
    ci = (ab_im * lr - (ab_re - 1.0) * li) / den
    br, bi = b_re.astype(F32), b_im.astype(F32)
    bb_re = cr[..., None] * br - ci[..., None] * bi
    bb_im = cr[..., None] * bi + ci[..., None] * br
    eye = jnp.eye(S5_GROUPS, dtype=F32)
    wb_re = jnp.einsum("gpc,gh->gchp", bb_re, eye).reshape(S5_WIDTH, S5_FLAT)
    wb_im = jnp.einsum("gpc,gh->gchp", bb_im, eye).reshape(S5_WIDTH, S5_FLAT)
    wc_re = jnp.einsum("gcp,gh->gphc", c_re.astype(F32), eye).reshape(S5_FLAT, S5_WIDTH).astype(MXU_DTYPE)
    wc_im = jnp.einsum("gcp,gh->gphc", c_im.astype(F32), eye).reshape(S5_FLAT, S5_WIDTH).astype(MXU_DTYPE)
    return ab_re.reshape(1, S5_FLAT), ab_im.reshape(1, S5_FLAT), wb_re, wb_im, wc_re, wc_im


def _s5(z3, h0r, h0i, sp, d, wg, bg, *, bt, tc, exact_in):
    b, seq, _ = z3.shape
    ar, ai, wbr, wbi, wcr, wci = sp
    if not exact_in:
        wbr, wbi = wbr.astype(MXU_DTYPE), wbi.astype(MXU_DTYPE)
    kern = functools.partial(_s5_kernel, bt=bt, tc=tc, exact_in=exact_in)
    const = lambda shape: pl.BlockSpec(shape, lambda i, c: (0,) * len(shape))
    return pl.pallas_call(
        kern,
        grid=(b // bt, seq // tc),
        in_specs=[
            pl.BlockSpec((bt, tc, S5_WIDTH), lambda i, c: (i, c, COL_U // S5_WIDTH)),
            pl.BlockSpec((bt, S5_FLAT), lambda i, c: (i, 0)),
            pl.BlockSpec((bt, S5_FLAT), lambda i, c: (i, 0)),
            const((1, S5_FLAT)), const((1, S5_FLAT)),
            const((S5_WIDTH, S5_FLAT)), const((S5_WIDTH, S5_FLAT)),
            const((S5_FLAT, S5_WIDTH)), const((S5_FLAT, S5_WIDTH)),
            const((1, S5_WIDTH)), const((S5_WIDTH, S5_WIDTH)), const((1, S5_WIDTH)),
        ],
        out_specs=[
            pl.BlockSpec((bt, tc, S5_WIDTH), lambda i, c: (i, c, 0)),
            pl.BlockSpec((bt, S5_FLAT), lambda i, c: (i, 0)),
            pl.BlockSpec((bt, S5_FLAT), lambda i, c: (i, 0)),
        ],
        out_shape=[
            jax.ShapeDtypeStruct((b, seq, S5_WIDTH), F32),
            jax.ShapeDtypeStruct((b, S5_FLAT), F32),
            jax.ShapeDtypeStruct((b, S5_FLAT), F32),
        ],
        scratch_shapes=[
            pltpu.VMEM((bt * tc, S5_FLAT), F32), pltpu.VMEM((bt * tc, S5_FLAT), F32),
            pltpu.VMEM((bt, S5_FLAT), F32), pltpu.VMEM((bt, S5_FLAT), F32),
        ],
        compiler_params=pltpu.CompilerParams(
            dimension_semantics=("parallel", "arbitrary"), vmem_limit_bytes=VMEM_LIMIT),
        name="s5",
    )(z3, h0r, h0i, ar, ai, wbr, wbi, wcr, wci, d, wg, bg)


def _sortable_key(score):
    bits = lax.bitcast_convert_type(score + 0.0, jnp.int32)
    return bits ^ ((bits >> 31) & jnp.int32(0x7FFFFFFF))


DSA_TRIP_BLOCKS = 4
DSA_KT = DSA_TRIP_BLOCKS * Q_BLOCK
I16_MIN = -(2 ** 15)


def _fold_rows(x, rows, op=jnp.add):
    parts = [x[i:i + rows] for i in range(0, x.shape[0], rows)]
    while len(parts) > 1:
        parts = [op(a, b) for a, b in zip(parts[0::2], parts[1::2])] + parts[len(parts) & ~1:]
    return parts[0]


def _bisect16(count_ge, k):
    def body(it, res):
        cand = res | lax.shift_left(jnp.int32(1), jnp.int32(15) - it)
        cnt = count_ge((cand + I16_MIN).astype(jnp.int16))
        return jnp.where(cnt >= k, cand, res)

    return lax.fori_loop(0, 16, body, jnp.zeros((1, Q_BLOCK), jnp.int32))


def _dsa_kernel(cq_ref, ckv_ref, qi_ref, kiwq_ref, kiwk_ref, gq_ref, gk_ref, bias_ref,
                o_ref, ck_out, cv_out, cki_out, kn_s, vt_s, ki_s, keys_s, hi_s, lo_s, negm_s, s_s, p_s, acc_s,
                *, seq, topk):
    j = pl.program_id(1)
    nkt = j // DSA_TRIP_BLOCKS + 1

    @pl.when(j == 0)
    def _prep():
        kn = _rms(ckv_ref[0, :, 0:HEAD_DIM], gk_ref[...])
        cv = ckv_ref[0, :, HEAD_DIM:2 * HEAD_DIM]
        ck_out[0] = kn
        cv_out[0] = cv
        kn_s[...] = kn.astype(MXU_DTYPE)
        eye = (lax.broadcasted_iota(jnp.int32, (HEAD_DIM, HEAD_DIM), 0)
               == lax.broadcasted_iota(jnp.int32, (HEAD_DIM, HEAD_DIM), 1)).astype(MXU_DTYPE)
        vt_s[...] = _nt_dot(eye, cv.astype(MXU_DTYPE)).astype(MXU_DTYPE)
        ki = kiwk_ref[0, :, 0:IDX_DIM]
        cki_out[0] = ki
        ki_s[...] = ki.astype(MXU_DTYPE)
        hi_s[...] = jnp.full(hi_s.shape, I16_MIN, jnp.int16)
        lo_s[...] = jnp.full(lo_s.shape, I16_MIN, jnp.int16)

    cq = cq_ref[0]
    q_all = jnp.concatenate(
        [(_rms(cq[:, h * HEAD_DIM:(h + 1) * HEAD_DIM], gq_ref[...]) * HEAD_DIM ** -0.5).astype(MXU_DTYPE)
         for h in range(C_HEADS)], axis=0)
    qi = qi_ref[0].astype(MXU_DTYPE)
    qi_all = jnp.concatenate([qi[:, h * IDX_DIM:(h + 1) * IDX_DIM] for h in range(IDX_HEADS)], axis=0)
    w_t = kiwq_ref[0].T
    w_rows = [w_t[IDX_DIM + h:IDX_DIM + h + 1, :] * INDEX_SCALE for h in range(IDX_HEADS)]
    q_chunk = (j * Q_BLOCK + lax.broadcasted_iota(jnp.int32, (1, Q_BLOCK), 1)) // CHUNK
    kt_iota = lax.broadcasted_iota(jnp.int32, (DSA_KT, 1), 0)

    def trip_rows(kt):
        return pl.multiple_of(kt * DSA_KT, DSA_KT)

    def idx_body(kt, _):
        off = trip_rows(kt)
        dots = _nt_dot(ki_s[pl.ds(off, DSA_KT), :], qi_all)
        score = w_rows[0] * jnp.maximum(dots[:, 0:Q_BLOCK], 0.0)
        for h in range(1, IDX_HEADS):
            score = score + w_rows[h] * jnp.maximum(dots[:, h * Q_BLOCK:(h + 1) * Q_BLOCK], 0.0)
        adm = ((off + kt_iota) // CHUNK) <= q_chunk
        key = jnp.where(adm, _sortable_key(score), jnp.int32(INT_MIN))
        keys_s[pl.ds(off, DSA_KT), :] = key
        hi_s[pl.ds(off, DSA_KT), :] = (key >> 16).astype(jnp.int16)
        lo_s[pl.ds(off, DSA_KT), :] = ((key & 0xFFFF) + I16_MIN).astype(jnp.int16)
        return 0

    lax.fori_loop(0, nkt, idx_body, 0)

    def count16(ref, pred):
        parts = [_fold_rows(jnp.where(pred(ref[r:r + DSA_KT, :]), jnp.int16(1), jnp.int16(0)), 16)
                 for r in range(0, seq, DSA_KT)]
        acc = _fold_rows(jnp.concatenate(parts, axis=0), 16)
        return jnp.sum(acc.astype(jnp.int32), axis=0, keepdims=True)

    def count(pred):
        def body(kt, acc):
            off = trip_rows(kt)
            hit = jnp.where(pred(keys_s[pl.ds(off, DSA_KT), :], off + kt_iota), 1, 0)
            return acc + jnp.sum(hit.reshape(DSA_KT // 8, 8, Q_BLOCK), axis=0)

        acc = lax.fori_loop(0, nkt, body, jnp.zeros((8, Q_BLOCK), jnp.int32))
        return jnp.sum(acc, axis=0, keepdims=True)

    hi_u = _bisect16(lambda c: count16(hi_s, lambda v: v >= c), topk)
    thr_hi = (hi_u + I16_MIN).astype(jnp.int16)
    need_lo = topk - count16(hi_s, lambda v: v > thr_hi)

    def mask_lo(kt, _):
        rows = pl.ds(trip_rows(kt), DSA_KT)
        lo_s[rows, :] = jnp.where(hi_s[rows, :] == thr_hi, lo_s[rows, :], jnp.int16(I16_MIN))
        return 0

    lax.fori_loop(0, nkt, mask_lo, 0)
    lo_u = _bisect16(lambda c: count16(lo_s, lambda v: v >= c), need_lo)
    thr = lax.shift_left(hi_u + I16_MIN, 16) | lo_u
    real_thr = thr != jnp.int32(INT_MIN)
    n_tied = count16(lo_s, lambda v: v >= (lo_u + I16_MIN).astype(jnp.int16))
    pos_bits = (seq - 1).bit_length()

    def _tie_search():
        need = topk - count(lambda k, pos: k > thr)

        def body(it, v):
            cand = v | lax.shift_left(jnp.int32(1), jnp.int32(pos_bits - 1) - it)
            return jnp.where(count(lambda k, pos: (k == thr) & (pos < cand)) < need, cand, v)

        return lax.fori_loop(0, pos_bits, body, jnp.zeros((1, Q_BLOCK), jnp.int32))

    has_ties = jnp.max(jnp.where(real_thr & (n_tied > need_lo), 1, 0)) > 0
    tie_last = lax.cond(has_ties, _tie_search, lambda: jnp.full((1, Q_BLOCK), 2 ** pos_bits - 1, jnp.int32))
    tie_last = jnp.where(real_thr, tie_last, -1)

    def mask_body(kt, _):
        off = trip_rows(kt)
        keyc = keys_s[pl.ds(off, DSA_KT), :]
        sel = (keyc > thr) | ((keyc == thr) & (off + kt_iota <= tie_last))
        negm_s[pl.ds(off, DSA_KT), :] = jnp.where(sel, 0.0, NEG_INF)
        return 0

    lax.fori_loop(0, nkt, mask_body, 0)

    heads = [slice(h * Q_BLOCK, (h + 1) * Q_BLOCK) for h in range(C_HEADS)]

    def score_body(kt, m8):
        off = trip_rows(kt)
        rows = pl.ds(off, DSA_KT)
        s_all = _nt_dot(kn_s[rows, :], q_all)
        negm = negm_s[rows, :]
        bidx = [jnp.clip(j - (DSA_TRIP_BLOCKS * kt + t), 0, 2) for t in range(DSA_TRIP_BLOCKS)]
        tops = []
        for ls in heads:
            bias = jnp.concatenate([bias_ref[bi, :, ls] for bi in bidx], axis=0)
            s = s_all[:, ls] + bias + negm
            s_s[rows, ls] = s
            tops.append(_fold_rows(s, 8, jnp.maximum))
        return jnp.maximum(m8, jnp.concatenate(tops, axis=1))

    m8 = lax.fori_loop(0, nkt, score_body, jnp.full((8, C_HEADS * Q_BLOCK), NEG_INF, F32))
    m = jnp.max(m8, axis=0, keepdims=True)

    def prob_body(kt, l8):
        rows = pl.ds(trip_rows(kt), DSA_KT)
        p = jnp.exp(s_s[rows, :] - m)
        p_s[rows, :] = p.astype(MXU_DTYPE)
        return l8 + _fold_rows(p, 8)

    l8 = lax.fori_loop(0, nkt, prob_body, jnp.zeros((8, C_HEADS * Q_BLOCK), F32))
    denom = jnp.sum(l8, axis=0, keepdims=True)

    acc_s[...] = jnp.zeros(acc_s.shape, F32)

    def value_body(kt, _):
        rows = pl.ds(trip_rows(kt), DSA_KT)
        vtc = vt_s[:, rows]
        for h, ls in enumerate(heads):
            rs = slice(h * HEAD_DIM, (h + 1) * HEAD_DIM)
            acc_s[rs, :] = acc_s[rs, :] + _dot(vtc, p_s[rows, ls])
        return 0

    lax.fori_loop(0, nkt, value_body, 0)
    out_t = jnp.concatenate(
        [acc_s[h * HEAD_DIM:(h + 1) * HEAD_DIM, :] / denom[:, ls] for h, ls in enumerate(heads)], axis=0)
    o_ref[0] = out_t.T


def _t5_bucket(rel):
    half = T5_BUCKETS // 2
    max_exact = half // 2
    n = jnp.abs(rel)
    log_val = jnp.log(jnp.maximum(n, 1).astype(F32) / max_exact) / math.log(T5_MAX_DIST / max_exact)
    large = jnp.minimum(max_exact + (log_val * (half - max_exact)).astype(jnp.int32), half - 1)
    return jnp.where(rel > 0, half, 0) + jnp.where(n < max_exact, n, large)


def _dsa_bias_tiles(t5_table):
    s = jnp.arange(Q_BLOCK)[:, None]
    t = jnp.arange(Q_BLOCK)[None, :]
    tiles = []
    for d in range(3):
        rel = s - d * Q_BLOCK - t
        tile = t5_table.astype(F32)[_t5_bucket(rel)]
        tiles.append(tile.transpose(0, 2, 1).reshape(Q_BLOCK, C_HEADS * Q_BLOCK))
    return jnp.stack(tiles)


def _dsa(z3, bias, gq, gk):
    b, seq, _ = z3.shape
    assert seq % DSA_KT == 0
    topk = min(TOPK_MAX, seq // 4)
    kern = functools.partial(_dsa_kernel, seq=seq, topk=topk)
    return pl.pallas_call(
        kern,
        grid=(b, seq // Q_BLOCK),
        in_specs=[
            pl.BlockSpec((1, Q_BLOCK, C_WIDTH), lambda i, j: (i, j, COL_CQ // C_WIDTH)),
            pl.BlockSpec((1, seq, 128), lambda i, j: (i, 0, COL_CKV // 128)),
            pl.BlockSpec((1, Q_BLOCK, 256), lambda i, j: (i, j, COL_QI // 256)),
            pl.BlockSpec((1, Q_BLOCK, 128), lambda i, j: (i, j, COL_KIW // 128)),
            pl.BlockSpec((1, seq, 128), lambda i, j: (i, 0, COL_KIW // 128)),
            pl.BlockSpec((1, HEAD_DIM), lambda i, j: (0, 0)),
            pl.BlockSpec((1, HEAD_DIM), lambda i, j: (0, 0)),
            pl.BlockSpec((3, Q_BLOCK, C_HEADS * Q_BLOCK), lambda i, j: (0, 0, 0)),
        ],
        out_specs=[
            pl.BlockSpec((1, Q_BLOCK, C_WIDTH), lambda i, j: (i, j, 0)),
            pl.BlockSpec((1, seq, HEAD_DIM), lambda i, j: (i, 0, 0)),
            pl.BlockSpec((1, seq, HEAD_DIM), lambda i, j: (i, 0, 0)),
            pl.BlockSpec((1, seq, IDX_DIM), lambda i, j: (i, 0, 0)),
        ],
        out_shape=[
            jax.ShapeDtypeStruct((b, seq, C_WIDTH), F32),
            jax.ShapeDtypeStruct((b, seq, HEAD_DIM), F32),
            jax.ShapeDtypeStruct((b, seq, HEAD_DIM), F32),
            jax.ShapeDtypeStruct((b, seq, IDX_DIM), F32),
        ],
        scratch_shapes=[
            pltpu.VMEM((seq, HEAD_DIM), MXU_DTYPE),
            pltpu.VMEM((HEAD_DIM, seq), MXU_DTYPE),
            pltpu.VMEM((seq, IDX_DIM), MXU_DTYPE),
            pltpu.VMEM((seq, Q_BLOCK), jnp.int32),
            pltpu.VMEM((seq, Q_BLOCK), jnp.int16),
            pltpu.VMEM((seq, Q_BLOCK), jnp.int16),
            pltpu.VMEM((seq, Q_BLOCK), F32),
            pltpu.VMEM((seq, C_HEADS * Q_BLOCK), F32),
            pltpu.VMEM((seq, C_HEADS * Q_BLOCK), MXU_DTYPE),
            pltpu.VMEM((C_WIDTH, Q_BLOCK), F32),
        ],
        compiler_params=pltpu.CompilerParams(
            dimension_semantics=("parallel", "arbitrary"), vmem_limit_bytes=VMEM_LIMIT),
        name="dsa",
    )(z3, z3, z3, z3, z3, gq, gk, bias)


def _mix_residual(x_ref, oa_ref, ob_ref, oc_ref, wo_ref):
    mix = (_dot(oa_ref[...].astype(MXU_DTYPE), wo_ref[0:A_WIDTH, :])
           + _dot(ob_ref[...].astype(MXU_DTYPE), wo_ref[A_WIDTH:A_WIDTH + S5_WIDTH, :])
           + _dot(oc_ref[...].astype(MXU_DTYPE), wo_ref[A_WIDTH + S5_WIDTH:, :]))
    return x_ref[...] + mix


def _ffn_chunk(xn, g, g1, g2, sl, wu_ref, cw_ref, cb_ref, wd_ref):
    up = _dot(xn, wu_ref[:, sl])
    gc = cw_ref[0:1, sl] * g2 + cw_ref[1:2, sl] * g1 + cw_ref[2:3, sl] * g + cb_ref[:, sl]
    return _dot((jax.nn.silu(gc) * up).astype(MXU_DTYPE), wd_ref[sl, :])


def _out_ffn_kernel(x_ref, oa_ref, ob_ref, oc_ref, wo_ref, g2_ref, wg_ref, wu_ref, cw_ref, cb_ref, wd_ref,
                    y_ref, conv_ref, carry_s, *, tiles_per_seq, tf):
    i = pl.program_id(0)
    tm = x_ref.shape[0]
    x1 = _mix_residual(x_ref, oa_ref, ob_ref, oc_ref, wo_ref)
    xn = _rms(x1, g2_ref[...]).astype(MXU_DTYPE)

    @pl.when(i % tiles_per_seq == 0)
    def _seq_start():
        carry_s[...] = jnp.zeros(carry_s.shape, F32)

    row = lax.broadcasted_iota(jnp.int32, (tm, 1), 0)
    acc = x1
    for f in range(FFN_DIM // tf):
        sl = slice(f * tf, (f + 1) * tf)
        g = _dot(xn, wg_ref[:, sl])
        prev2, prev1 = carry_s[0:1, sl], carry_s[1:2, sl]
        g1 = jnp.where(row == 0, prev1, pltpu.roll(g, 1, 0))
        g2 = jnp.where(row == 0, prev2, jnp.where(row == 1, prev1, pltpu.roll(g, 2, 0)))
        acc = acc + _ffn_chunk(xn, g, g1, g2, sl, wu_ref, cw_ref, cb_ref, wd_ref)
        carry_s[0:2, sl] = g[tm - 2:tm, :]
    y_ref[...] = acc
    conv_ref[0] = carry_s[0:2, :]


def _out_ffn(x2d, oa, ob, oc, wo, g2, wg, wu, cw, cb, wd, *, seq, tm, tf):
    m = x2d.shape[0]
    tiles_per_seq = seq // tm
    kern = functools.partial(_out_ffn_kernel, tiles_per_seq=tiles_per_seq, tf=tf)
    row = lambda w: pl.BlockSpec((tm, w), lambda i: (i, 0))
    const = lambda shape: pl.BlockSpec(shape, lambda i: (0,) * len(shape), pipeline_mode=pl.Buffered(1))
    return pl.pallas_call(
        kern,
        grid=(m // tm,),
        in_specs=[
            row(D_MODEL), row(A_WIDTH), row(S5_WIDTH), row(C_WIDTH),
            const((D_MODEL, D_MODEL)), const((1, D_MODEL)),
            const((D_MODEL, FFN_DIM)), const((D_MODEL, FFN_DIM)),
            const((CONV_W, FFN_DIM)), const((1, FFN_DIM)), const((FFN_DIM, D_MODEL)),
        ],
        out_specs=[
            row(D_MODEL),
            pl.BlockSpec((1, CONV_W - 1, FFN_DIM), lambda i: (i // tiles_per_seq, 0, 0)),
        ],
        out_shape=[
            jax.ShapeDtypeStruct((m, D_MODEL), F32),
            jax.ShapeDtypeStruct((m // seq, CONV_W - 1, FFN_DIM), F32),
        ],
        scratch_shapes=[pltpu.VMEM((8, FFN_DIM), F32)],
        compiler_params=pltpu.CompilerParams(dimension_semantics=("arbitrary",), vmem_limit_bytes=VMEM_LIMIT),
        name="out_ffn",
    )(x2d, oa, ob, oc, wo, g2, wg, wu, cw, cb, wd)


def _layer_params(l, t5_bias, norm1_g, w_in, a_q_gain, a_k_gain, a_rel_bias, s5_lam_re, s5_lam_im, s5_log_dt,
                  s5_b_re, s5_b_im, s5_c_re, s5_c_im, s5_d, s5_w_glu, s5_b_glu, c_q_gain, c_k_gain, w_out,
                  norm2_g, ffn_w_gate, ffn_w_up, ffn_conv_w, ffn_conv_b, ffn_w_down):
    w = w_in[l]
    sizes = (A_WIDTH, A_WIDTH, A_WIDTH, S5_WIDTH, C_WIDTH, HEAD_DIM, HEAD_DIM, IDX_HEADS * IDX_DIM, IDX_DIM, IDX_HEADS)
    cuts = [0]
    for s in sizes:
        cuts.append(cuts[-1] + s)
    aq, ak, av, u, cq, ck, cv, qi, ki, wi = [w[:, cuts[n]:cuts[n + 1]] for n in range(len(sizes))]
    pad = jnp.zeros((D_MODEL, Z_WIDTH - COL_KIW - IDX_DIM - IDX_HEADS), w.dtype)
    w_r = jnp.concatenate([aq, ak, av, cq, u, qi, ck, cv, ki, wi, pad], axis=1).astype(MXU_DTYPE)
    return dict(
        norm1=norm1_g[l].reshape(1, D_MODEL), w_in=w_r,
        a_gq=a_q_gain[l].reshape(1, HEAD_DIM), a_gk=a_k_gain[l].reshape(1, HEAD_DIM), a_rel=a_rel_bias[l],
        s5=_s5_params(s5_lam_re[l], s5_lam_im[l], s5_log_dt[l], s5_b_re[l], s5_b_im[l], s5_c_re[l], s5_c_im[l]),
        s5_d=s5_d[l].reshape(1, S5_WIDTH), s5_wg=s5_w_glu[l].astype(MXU_DTYPE), s5_bg=s5_b_glu[l].reshape(1, S5_WIDTH),
        c_gq=c_q_gain[l].reshape(1, HEAD_DIM), c_gk=c_k_gain[l].reshape(1, HEAD_DIM),
        w_out=w_out[l].astype(MXU_DTYPE), norm2=norm2_g[l].reshape(1, D_MODEL),
        wg=ffn_w_gate[l].astype(MXU_DTYPE), wu=ffn_w_up[l].astype(MXU_DTYPE), cw=ffn_conv_w[l],
        cb=ffn_conv_b[l].reshape(1, FFN_DIM), wd=ffn_w_down[l].astype(MXU_DTYPE),
    )


def _prompt_layer(x, p, dsa_bias, *, tm, tf, s5_bt, s5_tc):
    b, seq, _ = x.shape
    x2d = x.reshape(b * seq, D_MODEL)
    z3 = _in_proj(x2d, p["norm1"], p["w_in"], tm).reshape(b, seq, Z_WIDTH)
    oa, a_k, a_v = _attn_a(z3, _attn_a_bias(p["a_rel"]), p["a_gq"], p["a_gk"])
    zero_h = jnp.zeros((b, S5_FLAT), F32)
    ob, h_re, h_im = _s5(z3, zero_h, zero_h, p["s5"], p["s5_d"], p["s5_wg"], p["s5_bg"],
                         bt=s5_bt, tc=s5_tc, exact_in=False)
    oc, c_k, c_v, c_ki = _dsa(z3, dsa_bias, p["c_gq"], p["c_gk"])
    y, conv = _out_ffn(x2d, oa.reshape(b * seq, A_WIDTH), ob.reshape(b * seq, S5_WIDTH),
                       oc.reshape(b * seq, C_WIDTH), p["w_out"], p["norm2"], p["wg"], p["wu"], p["cw"], p["cb"],
                       p["wd"], seq=seq, tm=tm, tf=tf)
    keep = a_k.shape[1]
    states = (a_k.reshape(b, keep, A_HEADS, HEAD_DIM), a_v.reshape(b, keep, A_HEADS, HEAD_DIM),
              h_re.reshape(b, S5_GROUPS, S5_STATE), h_im.reshape(b, S5_GROUPS, S5_STATE), c_k, c_v, c_ki, conv)
    return y.reshape(b, seq, D_MODEL), states


def _attn_a_step_kernel(q_ref, k_ref, v_ref, ck_ref, cv_ref, bc_ref, bn_ref, gq_ref, gk_ref, o_ref, ak_ref, av_ref):
    v = v_ref[0]
    av_ref[0] = v
    for h in range(A_HEADS):
        sl = slice(h * HEAD_DIM, (h + 1) * HEAD_DIM)
        qh = (_rms(q_ref[0, :, sl], gq_ref[...]) * HEAD_DIM ** -0.5).astype(MXU_DTYPE)
        khn = _rms(k_ref[0, :, sl], gk_ref[...])
        ak_ref[0, :, sl] = khn
        s_c = _nt_dot(qh, ck_ref[0, :, sl].astype(MXU_DTYPE)) + bc_ref[h]
        s_n = _nt_dot(qh, khn.astype(MXU_DTYPE)) + bn_ref[h]
        m = jnp.maximum(jnp.max(s_c, axis=-1, keepdims=True), jnp.max(s_n, axis=-1, keepdims=True))
        p_c = jnp.exp(s_c - m)
        p_n = jnp.exp(s_n - m)
        denom = jnp.sum(p_c, axis=-1, keepdims=True) + jnp.sum(p_n, axis=-1, keepdims=True)
        oh = (_dot(p_c.astype(MXU_DTYPE), cv_ref[0, :, sl].astype(MXU_DTYPE))
              + _dot(p_n.astype(MXU_DTYPE), v[:, sl].astype(MXU_DTYPE)))
        o_ref[0, :, sl] = oh / denom


def _attn_a_step_bias(rel_table, past, t_new):
    t = jnp.arange(t_new)[:, None]
    rel = jnp.concatenate([jnp.arange(past) - past, jnp.arange(t_new)])[None, :] - t
    bias = rel_table.astype(F32)[:, jnp.clip(rel, -A_MAX_REL, A_MAX_REL) + A_MAX_REL]
    return bias[:, :, :past], bias[:, :, past:]


def _attn_a_step(z3, cache_k, cache_v, layer, bias_c, bias_n, gq, gk):
    b, t_new, _ = z3.shape
    past = cache_k.shape[2]
    new = lambda col: pl.BlockSpec((1, t_new, A_WIDTH), lambda i: (i, 0, col // A_WIDTH))
    const = lambda shape: pl.BlockSpec(shape, lambda i: (0,) * len(shape))
    return pl.pallas_call(
        _attn_a_step_kernel,
        grid=(b,),
        in_specs=[
            new(COL_AQ), new(COL_AK), new(COL_AV),
            pl.BlockSpec((None, 1, past, A_WIDTH), lambda i: (layer, i, 0, 0)),
            pl.BlockSpec((None, 1, past, A_WIDTH), lambda i: (layer, i, 0, 0)),
            const((A_HEADS, t_new, past)), const((A_HEADS, t_new, t_new)),
            const((1, HEAD_DIM)), const((1, HEAD_DIM)),
        ],
        out_specs=[pl.BlockSpec((1, t_new, A_WIDTH), lambda i: (i, 0, 0))] * 3,
        out_shape=[jax.ShapeDtypeStruct((b, t_new, A_WIDTH), F32)] * 3,
        compiler_params=pltpu.CompilerParams(dimension_semantics=("parallel",), vmem_limit_bytes=VMEM_LIMIT),
        name="attn_a_step",
    )(z3, z3, z3, cache_k, cache_v, bias_c, bias_n, gq, gk)


def _dsa_step_kernel(cq_ref, ckv_ref, qi_ref, kiw_ref, ck_ref, cv_ref, cki_ref, gq_ref, gk_ref, bias_ref,
                     o_ref, ck_out, cv_out, cki_out, *, past, t_new, topk):
    n_keys = past + t_new
    kn_new = _rms(ckv_ref[0, :, 0:HEAD_DIM], gk_ref[...])
    cv_new = ckv_ref[0, :, HEAD_DIM:2 * HEAD_DIM]
    ki_new = kiw_ref[0, :, 0:IDX_DIM]
    ck_out[0] = kn_new
    cv_out[0] = cv_new
    cki_out[0] = ki_new
    k_all = jnp.concatenate([ck_ref[0], kn_new], axis=0).astype(MXU_DTYPE)
    v_all = jnp.concatenate([cv_ref[0], cv_new], axis=0).astype(MXU_DTYPE)
    ki_all = jnp.concatenate([cki_ref[0], ki_new], axis=0).astype(MXU_DTYPE)
    eye = (lax.broadcasted_iota(jnp.int32, (HEAD_DIM, HEAD_DIM), 0)
           == lax.broadcasted_iota(jnp.int32, (HEAD_DIM, HEAD_DIM), 1)).astype(MXU_DTYPE)
    v_t = _nt_dot(eye, v_all).astype(MXU_DTYPE)

    qi = qi_ref[0].astype(MXU_DTYPE)
    kiw_pad = jnp.concatenate([kiw_ref[0], jnp.zeros((128 - t_new, 128), F32)], axis=0)
    w_t = kiw_pad.T[:, 0:t_new]
    score = jnp.zeros((n_keys, t_new), F32)
    for h in range(IDX_HEADS):
        dots = _nt_dot(ki_all, qi[:, h * IDX_DIM:(h + 1) * IDX_DIM])
        score = score + (w_t[IDX_DIM + h:IDX_DIM + h + 1, :] * INDEX_SCALE) * jnp.maximum(dots, 0.0)
    pos = lax.broadcasted_iota(jnp.int32, (n_keys, 1), 0)
    q_chunk = (past + lax.broadcasted_iota(jnp.int32, (1, t_new), 1)) // CHUNK
    keys = jnp.where((pos // CHUNK) <= q_chunk, _sortable_key(score), jnp.int32(INT_MIN))

    def count(pred):
        return jnp.sum(jnp.where(pred(keys, pos), 1, 0), axis=0, keepdims=True)

    def kth_body(it, res):
        cand = res | lax.shift_left(jnp.int32(1), jnp.int32(31) - it)
        thr_c = cand ^ jnp.int32(INT_MIN)
        return jnp.where(count(lambda k, p: k >= thr_c) >= topk, cand, res)

    thr = lax.fori_loop(0, 32, kth_body, jnp.zeros((1, t_new), jnp.int32)) ^ jnp.int32(INT_MIN)
    real_thr = thr != jnp.int32(INT_MIN)
    need = topk - count(lambda k, p: k > thr)
    pos_bits = (n_keys - 1).bit_length()

    def tie_body(it, v):
        cand = v | lax.shift_left(jnp.int32(1), jnp.int32(pos_bits - 1) - it)
        return jnp.where(count(lambda k, p: (k == thr) & (p < cand)) < need, cand, v)

    tie_last = lax.fori_loop(0, pos_bits, tie_body, jnp.zeros((1, t_new), jnp.int32))
    tie_last = jnp.where(real_thr, tie_last, -1)
    negm = jnp.where((keys > thr) | ((keys == thr) & (pos <= tie_last)), 0.0, NEG_INF)

    cq = cq_ref[0]
    outs = []
    for h in range(C_HEADS):
        qh = (_rms(cq[:, h * HEAD_DIM:(h + 1) * HEAD_DIM], gq_ref[...]) * HEAD_DIM ** -0.5).astype(MXU_DTYPE)
        s = _nt_dot(k_all, qh) + bias_ref[h] + negm
        p = jnp.exp(s - jnp.max(s, axis=0, keepdims=True))
        outs.append(_dot(v_t, p.astype(MXU_DTYPE)) / jnp.sum(p, axis=0, keepdims=True))
    out_t = jnp.concatenate(outs, axis=0)
    out_pad = jnp.concatenate([out_t, jnp.zeros((C_WIDTH, 128 - t_new), F32)], axis=1)
    o_ref[0] = out_pad.T[0:t_new, :]


def _dsa_step_bias(t5_table, past, t_new):
    s = jnp.arange(past + t_new)[:, None]
    t = jnp.arange(t_new)[None, :]
    return t5_table.astype(F32)[_t5_bucket(s - (past + t))].transpose(2, 0, 1)


def _dsa_step(z3, cache_k, cache_v, cache_ki, layer, bias, gq, gk):
    b, t_new, _ = z3.shape
    past = cache_k.shape[2]
    topk = min(TOPK_MAX, (past + t_new) // 4)
    kern = functools.partial(_dsa_step_kernel, past=past, t_new=t_new, topk=topk)
    new = lambda w, col: pl.BlockSpec((1, t_new, w), lambda i: (i, 0, col // w))
    per_seq = lambda n, w: pl.BlockSpec((1, n, w), lambda i: (i, 0, 0))
    cached = lambda w: pl.BlockSpec((None, 1, past, w), lambda i: (layer, i, 0, 0))
    const = lambda shape: pl.BlockSpec(shape, lambda i: (0,) * len(shape))
    return pl.pallas_call(
        kern,
        grid=(b,),
        in_specs=[
            new(C_WIDTH, COL_CQ), new(128, COL_CKV), new(256, COL_QI), new(128, COL_KIW),
            cached(HEAD_DIM), cached(HEAD_DIM), cached(IDX_DIM),
            const((1, HEAD_DIM)), const((1, HEAD_DIM)), const((C_HEADS, past + t_new, t_new)),
        ],
        out_specs=[per_seq(t_new, C_WIDTH), per_seq(t_new, HEAD_DIM), per_seq(t_new, HEAD_DIM),
                   per_seq(t_new, IDX_DIM)],
        out_shape=[
            jax.ShapeDtypeStruct((b, t_new, C_WIDTH), F32),
            jax.ShapeDtypeStruct((b, t_new, HEAD_DIM), F32),
            jax.ShapeDtypeStruct((b, t_new, HEAD_DIM), F32),
            jax.ShapeDtypeStruct((b, t_new, IDX_DIM), F32),
        ],
        compiler_params=pltpu.CompilerParams(dimension_semantics=("parallel",), vmem_limit_bytes=VMEM_LIMIT),
        name="dsa_step",
    )(z3, z3, z3, z3, cache_k, cache_v, cache_ki, gq, gk, bias)


def _out_ffn_step_kernel(x_ref, oa_ref, ob_ref, oc_ref, wo_ref, g2_ref, wg_ref, wu_ref, cw_ref, cb_ref, wd_ref,
                         e1_ref, e2_ref, y_ref, gate_ref, *, t_new, tf):
    tm = x_ref.shape[0]
    x1 = _mix_residual(x_ref, oa_ref, ob_ref, oc_ref, wo_ref)
    xn = _rms(x1, g2_ref[...]).astype(MXU_DTYPE)
    t = lax.broadcasted_iota(jnp.int32, (tm, 1), 0) % t_new
    acc = x1
    for f in range(FFN_DIM // tf):
        sl = slice(f * tf, (f + 1) * tf)
        g = _dot(xn, wg_ref[:, sl])
        gate_ref[:, sl] = g
        g1 = jnp.where(t == 0, e1_ref[:, sl], pltpu.roll(g, 1, 0))
        g2 = jnp.where(t <= 1, e2_ref[:, sl], pltpu.roll(g, 2, 0))
        acc = acc + _ffn_chunk(xn, g, g1, g2, sl, wu_ref, cw_ref, cb_ref, wd_ref)
    y_ref[...] = acc


def _out_ffn_step(x2d, oa, ob, oc, wo, g2, wg, wu, cw, cb, wd, conv_prev, *, t_new, tf):
    m = x2d.shape[0]
    b = m // t_new
    e1 = jnp.zeros((b, t_new, FFN_DIM), F32).at[:, 0].set(conv_prev[:, 1]).reshape(m, FFN_DIM)
    e2 = jnp.zeros((b, t_new, FFN_DIM), F32).at[:, 0].set(conv_prev[:, 0]).at[:, 1].set(conv_prev[:, 1])
    e2 = e2.reshape(m, FFN_DIM)
    kern = functools.partial(_out_ffn_step_kernel, t_new=t_new, tf=tf)
    full = lambda shape: pl.BlockSpec(shape, lambda i: (0,) * len(shape))
    y, gate = pl.pallas_call(
        kern,
        grid=(1,),
        in_specs=[
            full((m, D_MODEL)), full((m, A_WIDTH)), full((m, S5_WIDTH)), full((m, C_WIDTH)),
            full((D_MODEL, D_MODEL)), full((1, D_MODEL)),
            full((D_MODEL, FFN_DIM)), full((D_MODEL, FFN_DIM)),
            full((CONV_W, FFN_DIM)), full((1, FFN_DIM)), full((FFN_DIM, D_MODEL)),
            full((m, FFN_DIM)), full((m, FFN_DIM)),
        ],
        out_specs=[full((m, D_MODEL)), full((m, FFN_DIM))],
        out_shape=[jax.ShapeDtypeStruct((m, D_MODEL), F32), jax.ShapeDtypeStruct((m, FFN_DIM), F32)],
        compiler_params=pltpu.CompilerParams(dimension_semantics=("arbitrary",), vmem_limit_bytes=VMEM_LIMIT),
        name="out_ffn_step",
    )(x2d, oa, ob, oc, wo, g2, wg, wu, cw, cb, wd, e1, e2)
    return y, gate.reshape(b, t_new, FFN_DIM)[:, t_new - (CONV_W - 1):]


def _sample_layer(x, p, dsa_bias, layer, ca_k, ca_v, h_re0, h_im0, cc_k, cc_v, cc_ki, conv_prev, *, tf):
    b, t_new, _ = x.shape
    m = b * t_new
    x2d = x.reshape(m, D_MODEL)
    z3 = _in_proj(x2d, p["norm1"], p["w_in"], m).reshape(b, t_new, Z_WIDTH)
    depth, _, a_past = ca_k.shape[:3]
    bias_c, bias_n = _attn_a_step_bias(p["a_rel"], a_past, t_new)
    oa, a_k, a_v = _attn_a_step(z3, ca_k.reshape(depth, b, a_past, A_WIDTH), ca_v.reshape(depth, b, a_past, A_WIDTH),
                                layer, bias_c, bias_n, p["a_gq"], p["a_gk"])
    ob, h_re, h_im = _s5(z3, h_re0.reshape(b, S5_FLAT), h_im0.reshape(b, S5_FLAT), p["s5"], p["s5_d"], p["s5_wg"],
                         p["s5_bg"], bt=8, tc=t_new, exact_in=True)
    oc, c_k, c_v, c_ki = _dsa_step(z3, cc_k, cc_v, cc_ki, layer, dsa_bias, p["c_gq"], p["c_gk"])
    y, conv = _out_ffn_step(x2d, oa.reshape(m, A_WIDTH), ob.reshape(m, S5_WIDTH), oc.reshape(m, C_WIDTH),
                            p["w_out"], p["norm2"], p["wg"], p["wu"], p["cw"], p["cb"], p["wd"], conv_prev,
                            t_new=t_new, tf=tf)
    states = (a_k.reshape(b, t_new, A_HEADS, HEAD_DIM), a_v.reshape(b, t_new, A_HEADS, HEAD_DIM),
              h_re.reshape(b, S5_GROUPS, S5_STATE), h_im.reshape(b, S5_GROUPS, S5_STATE), c_k, c_v, c_ki, conv)
    return y.reshape(b, t_new, D_MODEL), states


def kernel(x_prompt, x_sample, cache_a_k, cache_a_v, state_s5_re, state_s5_im, cache_c_k, cache_c_v, cache_c_idx_k,
           state_ffn_conv, t5_bias, norm1_g, w_in, a_q_gain, a_k_gain, a_rel_bias, s5_lam_re, s5_lam_im, s5_log_dt,
           s5_b_re, s5_b_im, s5_c_re, s5_c_im, s5_d, s5_w_glu, s5_b_glu, c_q_gain, c_k_gain, w_out, norm2_g,
           ffn_w_gate, ffn_w_up, ffn_conv_w, ffn_conv_b, ffn_w_down):
    depth = w_in.shape[0]
    dsa_bias = _dsa_bias_tiles(t5_bias)
    dsa_step_bias = _dsa_step_bias(t5_bias, cache_c_k.shape[2], x_sample.shape[1])
    yp, ys = x_prompt, x_sample
    prompt_states, sample_states = [], []
    for l in range(depth):
        p = _layer_params(l, t5_bias, norm1_g, w_in, a_q_gain, a_k_gain, a_rel_bias, s5_lam_re, s5_lam_im, s5_log_dt,
                          s5_b_re, s5_b_im, s5_c_re, s5_c_im, s5_d, s5_w_glu, s5_b_glu, c_q_gain, c_k_gain, w_out,
                          norm2_g, ffn_w_gate, ffn_w_up, ffn_conv_w, ffn_conv_b, ffn_w_down)
        yp, st_p = _prompt_layer(yp, p, dsa_bias, tm=512, tf=FFN_DIM, s5_bt=8, s5_tc=256)
        ys, st_s = _sample_layer(ys, p, dsa_step_bias, l, cache_a_k, cache_a_v, state_s5_re[l], state_s5_im[l],
                                 cache_c_k, cache_c_v, cache_c_idx_k, state_ffn_conv[l], tf=FFN_DIM)
        prompt_states.append(st_p)
        sample_states.append(st_s)
    (a_k_p, a_v_p, s5_re_p, s5_im_p, c_k_p, c_v_p, c_ki_p, conv_p) = [jnp.stack(z) for z in zip(*prompt_states)]
    (a_k_s, a_v_s, s5_re_s, s5_im_s, c_k_s, c_v_s, c_ki_s, conv_s) = [jnp.stack(z) for z in zip(*sample_states)]
    return (yp, ys, a_k_p, a_v_p, a_k_s, a_v_s, s5_re_p, s5_im_p, s5_re_s, s5_im_s,
            c_k_p, c_v_p, c_ki_p, c_k_s, c_v_s, c_ki_s, conv_p, conv_s)
```

```python
import functools
import math

import jax
import jax.numpy as jnp
from jax import lax
from jax.experimental import pallas as pl
from jax.experimental.pallas import tpu as pltpu

F32 = jnp.float32
MXU_DTYPE = jnp.bfloat16

D_MODEL = 1024
CHUNK = 64
HEAD_DIM = 64
A_HEADS = 6
A_WIDTH = A_HEADS * HEAD_DIM
A_BAND_PAST = 8 * CHUNK
A_MAX_REL = 128
S5_GROUPS = 16
S5_GROUP_CH = 16
S5_WIDTH = S5_GROUPS * S5_GROUP_CH
S5_STATE = 64
S5_FLAT = S5_GROUPS * S5_STATE
C_HEADS = 6
C_WIDTH = C_HEADS * HEAD_DIM
IDX_HEADS = 8
IDX_DIM = 32
INDEX_SCALE = (IDX_HEADS * IDX_DIM) ** -0.5
TOPK_MAX = 256
Q_BLOCK = 128
T5_BUCKETS = 32
T5_MAX_DIST = 128
FFN_DIM = 2816
CONV_W = 3
EPS = 1e-6
NEG_INF = -1e30
INT_MIN = -(2 ** 31)

COL_AQ, COL_AK, COL_AV, COL_CQ = 0, 384, 768, 1152
COL_U, COL_QI, COL_CKV, COL_KIW = 1536, 1792, 2048, 2176
Z_WIDTH = 2304
VMEM_LIMIT = 56 * 1024 * 1024


def _nt_dot(a, b):
    return lax.dot_general(a, b, (((1,), (1,)), ((), ())), preferred_element_type=F32)


def _dot(a, b):
    return jnp.dot(a, b, preferred_element_type=F32)


def _rms(x, g):
    return x * lax.rsqrt(jnp.mean(x * x, axis=-1, keepdims=True) + EPS) * g


def _in_proj_kernel(x_ref, g_ref, w_ref, z_ref):
    xn = _rms(x_ref[...], g_ref[...]).astype(MXU_DTYPE)
    z_ref[...] = _dot(xn, w_ref[...])


def _in_proj(x2d, g, w_r, tm):
    m = x2d.shape[0]
    return pl.pallas_call(
        _in_proj_kernel,
        grid=(m // tm,),
        in_specs=[
            pl.BlockSpec((tm, D_MODEL), lambda i: (i, 0)),
            pl.BlockSpec((1, D_MODEL), lambda i: (0, 0)),
            pl.BlockSpec((D_MODEL, Z_WIDTH), lambda i: (0, 0)),
        ],
        out_specs=pl.BlockSpec((tm, Z_WIDTH), lambda i: (i, 0)),
        out_shape=jax.ShapeDtypeStruct((m, Z_WIDTH), F32),
        compiler_params=pltpu.CompilerParams(dimension_semantics=("parallel",), vmem_limit_bytes=VMEM_LIMIT),
        name="in_proj",
    )(x2d, g, w_r)


A_QB = 2 * CHUNK
A_BAND = A_BAND_PAST + A_QB


def _eye(n, dtype):
    return (lax.broadcasted_iota(jnp.int32, (n, n), 0) == lax.broadcasted_iota(jnp.int32, (n, n), 1)).astype(dtype)


def _attn_a_kernel(q_ref, k_ref, v_ref, bias_ref, gq_ref, gk_ref, o_ref, ak_ref, av_ref, kn_s, vt_s, s_s, p_s,
                   *, seq, keep):
    j = pl.program_id(1)

    @pl.when(j == 0)
    def _prep():
        for h in range(A_HEADS):
            sl = slice(h * HEAD_DIM, (h + 1) * HEAD_DIM)
            khn = _rms(k_ref[0, :, sl], gk_ref[...])
            kn_s[h, 0:A_BAND_PAST, :] = jnp.zeros((A_BAND_PAST, HEAD_DIM), MXU_DTYPE)
            kn_s[h, A_BAND_PAST:A_BAND_PAST + seq, :] = khn.astype(MXU_DTYPE)
            ak_ref[0, :, sl] = khn[seq - keep:, :]
        v = v_ref[0]
        av_ref[0] = v[seq - keep:, :]
        vt_s[:, 0:A_BAND_PAST] = jnp.zeros((A_WIDTH, A_BAND_PAST), MXU_DTYPE)
        vt_s[:, A_BAND_PAST:A_BAND_PAST + seq] = _nt_dot(_eye(A_WIDTH, MXU_DTYPE), v.astype(MXU_DTYPE)).astype(MXU_DTYPE)

    start = pl.multiple_of(j * A_QB, A_QB)
    variant = jnp.minimum(j, A_LEAD_STEPS)
    for h in range(A_HEADS):
        sl = slice(h * HEAD_DIM, (h + 1) * HEAD_DIM)
        qh = (_rms(q_ref[0, :, sl], gq_ref[...]) * HEAD_DIM ** -0.5).astype(MXU_DTYPE)
        s_s[h] = _nt_dot(kn_s[h, pl.ds(start, A_BAND), :], qh) + bias_ref[variant, h]
    denoms = []
    for h in range(A_HEADS):
        s = s_s[h]
        p = jnp.exp(s - jnp.max(s, axis=0, keepdims=True))
        denoms.append(jnp.sum(p, axis=0, keepdims=True))
        p_s[h] = p.astype(MXU_DTYPE)
    outs = [_dot(vt_s[h * HEAD_DIM:(h + 1) * HEAD_DIM, pl.ds(start, A_BAND)], p_s[h]) / denoms[h]
            for h in range(A_HEADS)]
    o_ref[0] = jnp.concatenate(outs, axis=0).T


A_LEAD_STEPS = A_BAND_PAST // A_QB


def _attn_a_bias(rel_table):
    n_ext = A_BAND + A_QB - 1
    m = jnp.arange(n_ext)
    ext = rel_table.astype(F32)[:, jnp.clip(m - (A_QB - 1) - A_BAND_PAST, -A_MAX_REL, A_MAX_REL) + A_MAX_REL]
    rows = jnp.pad(jnp.broadcast_to(ext[:, None, :], (A_HEADS, A_QB, n_ext)), ((0, 0), (0, 0), (0, 1)))
    skew = rows.reshape(A_HEADS, A_QB * (n_ext + 1))[:, :A_QB * n_ext].reshape(A_HEADS, A_QB, n_ext)
    bias = skew[:, :, A_QB - 1:].transpose(0, 2, 1)
    c = jnp.arange(A_BAND)[:, None]
    lo = (jnp.arange(A_QB)[None, :] // CHUNK) * CHUNK
    in_band = (c >= lo) & (c < lo + A_BAND_PAST + CHUNK)
    first_live = (A_LEAD_STEPS - jnp.arange(A_LEAD_STEPS + 1)) * A_QB
    live = in_band[None] & (c[None] >= first_live[:, None, None])
    return jnp.where(live[:, None], bias[None], NEG_INF)


def _attn_a(z3, bias, gq, gk):
    b, seq, _ = z3.shape
    keep = min(A_BAND_PAST, seq)
    kern = functools.partial(_attn_a_kernel, seq=seq, keep=keep)
    return pl.pallas_call(
        kern,
        grid=(b, seq // A_QB),
        in_specs=[
            pl.BlockSpec((1, A_QB, A_WIDTH), lambda i, j: (i, j, COL_AQ // A_WIDTH)),
            pl.BlockSpec((1, seq, A_WIDTH), lambda i, j: (i, 0, COL_AK // A_WIDTH)),
            pl.BlockSpec((1, seq, A_WIDTH), lambda i, j: (i, 0, COL_AV // A_WIDTH)),
            pl.BlockSpec((A_LEAD_STEPS + 1, A_HEADS, A_BAND, A_QB), lambda i, j: (0, 0, 0, 0),
                         pipeline_mode=pl.Buffered(1)),
            pl.BlockSpec((1, HEAD_DIM), lambda i, j: (0, 0)),
            pl.BlockSpec((1, HEAD_DIM), lambda i, j: (0, 0)),
        ],
        out_specs=[
            pl.BlockSpec((1, A_QB, A_WIDTH), lambda i, j: (i, j, 0)),
            pl.BlockSpec((1, keep, A_WIDTH), lambda i, j: (i, 0, 0)),
            pl.BlockSpec((1, keep, A_WIDTH), lambda i, j: (i, 0, 0)),
        ],
        out_shape=[
            jax.ShapeDtypeStruct((b, seq, A_WIDTH), F32),
            jax.ShapeDtypeStruct((b, keep, A_WIDTH), F32),
            jax.ShapeDtypeStruct((b, keep, A_WIDTH), F32),
        ],
        scratch_shapes=[
            pltpu.VMEM((A_HEADS, A_BAND_PAST + seq, HEAD_DIM), MXU_DTYPE),
            pltpu.VMEM((A_WIDTH, A_BAND_PAST + seq), MXU_DTYPE),
            pltpu.VMEM((A_HEADS, A_BAND, A_QB), F32),
            pltpu.VMEM((A_HEADS, A_BAND, A_QB), MXU_DTYPE),
        ],
        compiler_params=pltpu.CompilerParams(
            dimension_semantics=("parallel", "arbitrary"), vmem_limit_bytes=VMEM_LIMIT),
        name="attn_a",
    )(z3, z3, z3, bias, gq, gk)


def _s5_kernel(u_ref, h0r_ref, h0i_ref, ar_ref, ai_ref, wbr_ref, wbi_ref, wcr_ref, wci_ref, d_ref, wg_ref, bg_ref,
               o_ref, hr_out, hi_out, xr_s, xi_s, hr_c, hi_c, *, bt, tc, exact_in):
    c = pl.program_id(1)

    @pl.when(c == 0)
    def _init():
        hr_c[...] = h0r_ref[...]
        hi_c[...] = h0i_ref[...]

    u = jnp.swapaxes(u_ref[...], 0, 1).reshape(tc * bt, S5_WIDTH)
    if exact_in:
        xr_s[...] = jnp.dot(u, wbr_ref[...], preferred_element_type=F32, precision=lax.Precision.HIGHEST)
        xi_s[...] = jnp.dot(u, wbi_ref[...], preferred_element_type=F32, precision=lax.Precision.HIGHEST)
    else:
        ub = u.astype(MXU_DTYPE)
        xr_s[...] = _dot(ub, wbr_ref[...])
        xi_s[...] = _dot(ub, wbi_ref[...])
    ar = ar_ref[...]
    ai = ai_ref[...]

    def step(t, carry):
        hr, hi = carry
        rows = pl.ds(pl.multiple_of(t * bt, bt), bt)
        nhr = ar * hr - ai * hi + xr_s[rows, :]
        nhi = ar * hi + ai * hr + xi_s[rows, :]
        xr_s[rows, :] = nhr
        xi_s[rows, :] = nhi
        return nhr, nhi

    hr, hi = lax.fori_loop(0, tc, step, (hr_c[...], hi_c[...]))
    hr_c[...] = hr
    hi_c[...] = hi
    hr_out[...] = hr
    hi_out[...] = hi
    y = (_dot(xr_s[...].astype(MXU_DTYPE), wcr_ref[...]) - _dot(xi_s[...].astype(MXU_DTYPE), wci_ref[...])
         + d_ref[...] * u)
    g = jax.nn.gelu(y)
    out = g * jax.nn.sigmoid(_dot(g.astype(MXU_DTYPE), wg_ref[...]) + bg_ref[...])
    o_ref[...] = jnp.swapaxes(out.reshape(tc, bt, S5_WIDTH), 0, 1)


def _s5_params(lam_re, lam_im, log_dt, b_re, b_im, c_re, c_im):
    dt = jnp.exp(log_dt.astype(F32))[:, None]
    lr, li = lam_re.astype(F32), lam_im.astype(F32)
    mag = jnp.exp(lr * dt)
    ab_re, ab_im = mag * jnp.cos(li * dt), mag * jnp.sin(li * dt)
    den = lr * lr + li * li
    cr = ((ab_re - 1.0) * lr + ab_im * li) / den
    ci = (ab_im * lr - (ab_re - 1.0) * li) / den
    br, bi = b_re.astype(F32), b_im.astype(F32)
    bb_re = cr[..., None] * br - ci[..., None] * bi
    bb_im = cr[..., None] * bi + ci[..., None] * br
    eye = jnp.eye(S5_GROUPS, dtype=F32)
    wb_re = jnp.einsum("gpc,gh->gchp", bb_re, eye).reshape(S5_WIDTH, S5_FLAT)
    wb_im = jnp.einsum("gpc,gh->gchp", bb_im, eye).reshape(S5_WIDTH, S5_FLAT)
    wc_re = jnp.einsum("gcp,gh->gphc", c_re.astype(F32), eye).reshape(S5_FLAT, S5_WIDTH).astype(MXU_DTYPE)
    wc_im = jnp.einsum("gcp,gh->gphc", c_im.astype(F32), eye).reshape(S5_FLAT, S5_WIDTH).astype(MXU_DTYPE)
    return ab_re.reshape(1, S5_FLAT), ab_im.reshape(1, S5_FLAT), wb_re, wb_im, wc_re, wc_im


def _s5(z3, h0r, h0i, sp, d, wg, bg, *, bt, tc, exact_in):
    b, seq, _ = z3.shape
    ar, ai, wbr, wbi, wcr, wci = sp
    if not exact_in:
        wbr, wbi = wbr.astype(MXU_DTYPE), wbi.astype(MXU_DTYPE)
    kern = functools.partial(_s5_kernel, bt=bt, tc=tc, exact_in=exact_in)
    const = lambda shape: pl.BlockSpec(shape, lambda i, c: (0,) * len(shape))
    return pl.pallas_call(
        kern,
        grid=(b // bt, seq // tc),
        in_specs=[
            pl.BlockSpec((bt, tc, S5_WIDTH), lambda i, c: (i, c, COL_U // S5_WIDTH)),
            pl.BlockSpec((bt, S5_FLAT), lambda i, c: (i, 0)),
            pl.BlockSpec((bt, S5_FLAT), lambda i, c: (i, 0)),
            const((1, S5_FLAT)), const((1, S5_FLAT)),
            const((S5_WIDTH, S5_FLAT)), const((S5_WIDTH, S5_FLAT)),
            const((S5_FLAT, S5_WIDTH)), const((S5_FLAT, S5_WIDTH)),
            const((1, S5_WIDTH)), const((S5_WIDTH, S5_WIDTH)), const((1, S5_WIDTH)),
        ],
        out_specs=[
            pl.BlockSpec((bt, tc, S5_WIDTH), lambda i, c: (i, c, 0)),
            pl.BlockSpec((bt, S5_FLAT), lambda i, c: (i, 0)),
            pl.BlockSpec((bt, S5_FLAT), lambda i, c: (i, 0)),
        ],
        out_shape=[
            jax.ShapeDtypeStruct((b, seq, S5_WIDTH), F32),
            jax.ShapeDtypeStruct((b, S5_FLAT), F32),
            jax.ShapeDtypeStruct((b, S5_FLAT), F32),
        ],
        scratch_shapes=[
            pltpu.VMEM((bt * tc, S5_FLAT), F32), pltpu.VMEM((bt * tc, S5_FLAT), F32),
            pltpu.VMEM((bt, S5_FLAT), F32), pltpu.VMEM((bt, S5_FLAT), F32),
        ],
        compiler_params=pltpu.CompilerParams(
            dimension_semantics=("parallel", "arbitrary"), vmem_limit_bytes=VMEM_LIMIT),
        name="s5",
    )(z3, h0r, h0i, ar, ai, wbr, wbi, wcr, wci, d, wg, bg)


def _sortable_key(score):
    bits = lax.bitcast_convert_type(score + 0.0, jnp.int32)
    return bits ^ ((bits >> 31) & jnp.int32(0x7FFFFFFF))


DSA_TRIP_BLOCKS = 4
DSA_KT = DSA_TRIP_BLOCKS * Q_BLOCK
I16_MIN = -(2 ** 15)


def _fold_rows(x, rows, op=jnp.add):
    parts = [x[i:i + rows] for i in range(0, x.shape[0], rows)]
    while len(parts) > 1:
        parts = [op(a, b) for a, b in zip(parts[0::2], parts[1::2])] + parts[len(parts) & ~1:]
    return parts[0]


def _bisect16(count_ge, k):
    def body(it, res):
        cand = res | lax.shift_left(jnp.int32(1), jnp.int32(15) - it)
        cnt = count_ge((cand + I16_MIN).astype(jnp.int16))
        return jnp.where(cnt >= k, cand, res)

    return lax.fori_loop(0, 16, body, jnp.zeros((1, Q_BLOCK), jnp.int32))


def _dsa_kernel(cq_ref, ckv_ref, qi_ref, kiwq_ref, kiwk_ref, gq_ref, gk_ref, bias_ref,
                o_ref, ck_out, cv_out, cki_out, kn_s, vt_s, ki_s, keys_s, hi_s, lo_s, negm_s, s_s, p_s,
                *, seq, topk):
    j = pl.program_id(1)
    nkt = j // DSA_TRIP_BLOCKS + 1

    @pl.when(j == 0)
    def _prep():
        kn = _rms(ckv_ref[0, :, 0:HEAD_DIM], gk_ref[...])
        cv = ckv_ref[0, :, HEAD_DIM:2 * HEAD_DIM]
        ck_out[0] = kn
        cv_out[0] = cv
        kn_s[...] = kn.astype(MXU_DTYPE)
        eye = (lax.broadcasted_iota(jnp.int32, (HEAD_DIM, HEAD_DIM), 0)
               == lax.broadcasted_iota(jnp.int32, (HEAD_DIM, HEAD_DIM), 1)).astype(MXU_DTYPE)
        vt_s[...] = _nt_dot(eye, cv.astype(MXU_DTYPE)).astype(MXU_DTYPE)
        ki = kiwk_ref[0, :, 0:IDX_DIM]
        cki_out[0] = ki
        ki_s[...] = ki.astype(MXU_DTYPE)

    cq = cq_ref[0]
    q_all = jnp.concatenate(
        [(_rms(cq[:, h * HEAD_DIM:(h + 1) * HEAD_DIM], gq_ref[...]) * HEAD_DIM ** -0.5).astype(MXU_DTYPE)
         for h in range(C_HEADS)], axis=0)
    qi = qi_ref[0].astype(MXU_DTYPE)
    qi_all = jnp.concatenate([qi[:, h * IDX_DIM:(h + 1) * IDX_DIM] for h in range(IDX_HEADS)], axis=0)
    w_t = kiwq_ref[0].T
    w_rows = [w_t[IDX_DIM + h:IDX_DIM + h + 1, :] * INDEX_SCALE for h in range(IDX_HEADS)]
    q_chunk = (j * Q_BLOCK + lax.broadcasted_iota(jnp.int32, (1, Q_BLOCK), 1)) // CHUNK
    kt_iota = lax.broadcasted_iota(jnp.int32, (DSA_KT, 1), 0)

    def trip_rows(kt):
        return pl.multiple_of(kt * DSA_KT, DSA_KT)

    def idx_body(kt, _):
        off = trip_rows(kt)
        dots = _nt_dot(ki_s[pl.ds(off, DSA_KT), :], qi_all)
        score = w_rows[0] * jnp.maximum(dots[:, 0:Q_BLOCK], 0.0)
        for h in range(1, IDX_HEADS):
            score = score + w_rows[h] * jnp.maximum(dots[:, h * Q_BLOCK:(h + 1) * Q_BLOCK], 0.0)
        adm = ((off + kt_iota) // CHUNK) <= q_chunk
        key = jnp.where(adm, _sortable_key(score), jnp.int32(INT_MIN))
        keys_s[pl.ds(off, DSA_KT), :] = key
        hi_s[pl.ds(off, DSA_KT), :] = (key >> 16).astype(jnp.int16)
        lo_s[pl.ds(off, DSA_KT), :] = ((key & 0xFFFF) + I16_MIN).astype(jnp.int16)
        return 0

    lax.fori_loop(0, nkt, idx_body, 0)

    def count16(ref, pred):
        def body(kt, acc):
            hit = jnp.where(pred(ref[pl.ds(trip_rows(kt), DSA_KT), :]), jnp.int16(1), jnp.int16(0))
            return acc + _fold_rows(hit, 16)

        acc = lax.fori_loop(0, nkt, body, jnp.zeros((16, Q_BLOCK), jnp.int16))
        return jnp.sum(acc.astype(jnp.int32), axis=0, keepdims=True)

    def count(pred):
        def body(kt, acc):
            off = trip_rows(kt)
            hit = jnp.where(pred(keys_s[pl.ds(off, DSA_KT), :], off + kt_iota), 1, 0)
            return acc + jnp.sum(hit.reshape(DSA_KT // 8, 8, Q_BLOCK), axis=0)

        acc = lax.fori_loop(0, nkt, body, jnp.zeros((8, Q_BLOCK), jnp.int32))
        return jnp.sum(acc, axis=0, keepdims=True)

    hi_u = _bisect16(lambda c: count16(hi_s, lambda v: v >= c), topk)
    thr_hi = (hi_u + I16_MIN).astype(jnp.int16)
    need_lo = topk - count16(hi_s, lambda v: v > thr_hi)

    def mask_lo(kt, _):
        rows = pl.ds(trip_rows(kt), DSA_KT)
        lo_s[rows, :] = jnp.where(hi_s[rows, :] == thr_hi, lo_s[rows, :], jnp.int16(I16_MIN))
        return 0

    lax.fori_loop(0, nkt, mask_lo, 0)
    lo_u = _bisect16(lambda c: count16(lo_s, lambda v: v >= c), need_lo)
    thr = lax.shift_left(hi_u + I16_MIN, 16) | lo_u
    real_thr = thr != jnp.int32(INT_MIN)
    n_tied = count16(lo_s, lambda v: v >= (lo_u + I16_MIN).astype(jnp.int16))
    pos_bits = (seq - 1).bit_length()

    def _tie_search():
        need = topk - count(lambda k, pos: k > thr)

        def body(it, v):
            cand = v | lax.shift_left(jnp.int32(1), jnp.int32(pos_bits - 1) - it)
            return jnp.where(count(lambda k, pos: (k == thr) & (pos < cand)) < need, cand, v)

        return lax.fori_loop(0, pos_bits, body, jnp.zeros((1, Q_BLOCK), jnp.int32))

    has_ties = jnp.max(jnp.where(real_thr & (n_tied > need_lo), 1, 0)) > 0
    tie_last = lax.cond(has_ties, _tie_search, lambda: jnp.full((1, Q_BLOCK), 2 ** pos_bits - 1, jnp.int32))
    tie_last = jnp.where(real_thr, tie_last, -1)

    def mask_body(kt, _):
        off = trip_rows(kt)
        keyc = keys_s[pl.ds(off, DSA_KT), :]
        sel = (keyc > thr) | ((keyc == thr) & (off + kt_iota <= tie_last))
        negm_s[pl.ds(off, DSA_KT), :] = jnp.where(sel, 0.0, NEG_INF)
        return 0

    lax.fori_loop(0, nkt, mask_body, 0)

    heads = [slice(h * Q_BLOCK, (h + 1) * Q_BLOCK) for h in range(C_HEADS)]

    def score_body(kt, m8):
        off = trip_rows(kt)
        rows = pl.ds(off, DSA_KT)
        s_all = _nt_dot(kn_s[rows, :], q_all)
        negm = negm_s[rows, :]
        bidx = [jnp.clip(j - (DSA_TRIP_BLOCKS * kt + t), 0, 2) for t in range(DSA_TRIP_BLOCKS)]
        tops = []
        for ls in heads:
            bias = jnp.concatenate([bias_ref[bi, :, ls] for bi in bidx], axis=0)
            s = s_all[:, ls] + bias + negm
            s_s[rows, ls] = s
            tops.append(_fold_rows(s, 8, jnp.maximum))
        return jnp.maximum(m8, jnp.concatenate(tops, axis=1))

    m8 = lax.fori_loop(0, nkt, score_body, jnp.full((8, C_HEADS * Q_BLOCK), NEG_INF, F32))
    m = jnp.max(m8, axis=0, keepdims=True)

    def prob_body(kt, l8):
        rows = pl.ds(trip_rows(kt), DSA_KT)
        p = jnp.exp(s_s[rows, :] - m)
        p_s[rows, :] = p.astype(MXU_DTYPE)
        return l8 + _fold_rows(p, 8)

    l8 = lax.fori_loop(0, nkt, prob_body, jnp.zeros((8, C_HEADS * Q_BLOCK), F32))
    denom = jnp.sum(l8, axis=0, keepdims=True)

    def value_body(kt, acc):
        rows = pl.ds(trip_rows(kt), DSA_KT)
        return acc + _dot(vt_s[:, rows], p_s[rows, :])

    acc = lax.fori_loop(0, nkt, value_body, jnp.zeros((HEAD_DIM, C_HEADS * Q_BLOCK), F32)) / denom
    o_ref[0] = jnp.concatenate([acc[:, ls] for ls in heads], axis=0).T


def _t5_bucket(rel):
    half = T5_BUCKETS // 2
    max_exact = half // 2
    n = jnp.abs(rel)
    log_val = jnp.log(jnp.maximum(n, 1).astype(F32) / max_exact) / math.log(T5_MAX_DIST / max_exact)
    large = jnp.minimum(max_exact + (log_val * (half - max_exact)).astype(jnp.int32), half - 1)
    return jnp.where(rel > 0, half, 0) + jnp.where(n < max_exact, n, large)


def _dsa_bias_tiles(t5_table):
    s = jnp.arange(Q_BLOCK)[:, None]
    t = jnp.arange(Q_BLOCK)[None, :]
    tiles = []
    for d in range(3):
        rel = s - d * Q_BLOCK - t
        tile = t5_table.astype(F32)[_t5_bucket(rel)]
        tiles.append(tile.transpose(0, 2, 1).reshape(Q_BLOCK, C_HEADS * Q_BLOCK))
    return jnp.stack(tiles)


def _dsa(z3, bias, gq, gk):
    b, seq, _ = z3.shape
    assert seq % DSA_KT == 0
    topk = min(TOPK_MAX, seq // 4)
    kern = functools.partial(_dsa_kernel, seq=seq, topk=topk)
    return pl.pallas_call(
        kern,
        grid=(b, seq // Q_BLOCK),
        in_specs=[
            pl.BlockSpec((1, Q_BLOCK, C_WIDTH), lambda i, j: (i, j, COL_CQ // C_WIDTH)),
            pl.BlockSpec((1, seq, 128), lambda i, j: (i, 0, COL_CKV // 128)),
            pl.BlockSpec((1, Q_BLOCK, 256), lambda i, j: (i, j, COL_QI // 256)),
            pl.BlockSpec((1, Q_BLOCK, 128), lambda i, j: (i, j, COL_KIW // 128)),
            pl.BlockSpec((1, seq, 128), lambda i, j: (i, 0, COL_KIW // 128)),
            pl.BlockSpec((1, HEAD_DIM), lambda i, j: (0, 0)),
            pl.BlockSpec((1, HEAD_DIM), lambda i, j: (0, 0)),
            pl.BlockSpec((3, Q_BLOCK, C_HEADS * Q_BLOCK), lambda i, j: (0, 0, 0)),
        ],
        out_specs=[
            pl.BlockSpec((1, Q_BLOCK, C_WIDTH), lambda i, j: (i, j, 0)),
            pl.BlockSpec((1, seq, HEAD_DIM), lambda i, j: (i, 0, 0)),
            pl.BlockSpec((1, seq, HEAD_DIM), lambda i, j: (i, 0, 0)),
            pl.BlockSpec((1, seq, IDX_DIM), lambda i, j: (i, 0, 0)),
        ],
        out_shape=[
            jax.ShapeDtypeStruct((b, seq, C_WIDTH), F32),
            jax.ShapeDtypeStruct((b, seq, HEAD_DIM), F32),
            jax.ShapeDtypeStruct((b, seq, HEAD_DIM), F32),
            jax.ShapeDtypeStruct((b, seq, IDX_DIM), F32),
        ],
        scratch_shapes=[
            pltpu.VMEM((seq, HEAD_DIM), MXU_DTYPE),
            pltpu.VMEM((HEAD_DIM, seq), MXU_DTYPE),
            pltpu.VMEM((seq, IDX_DIM), MXU_DTYPE),
            pltpu.VMEM((seq, Q_BLOCK), jnp.int32),
            pltpu.VMEM((seq, Q_BLOCK), jnp.int16),
            pltpu.VMEM((seq, Q_BLOCK), jnp.int16),
            pltpu.VMEM((seq, Q_BLOCK), F32),
            pltpu.VMEM((seq, C_HEADS * Q_BLOCK), F32),
            pltpu.VMEM((seq, C_HEADS * Q_BLOCK), MXU_DTYPE),
        ],
        compiler_params=pltpu.CompilerParams(
            dimension_semantics=("parallel", "arbitrary"), vmem_limit_bytes=VMEM_LIMIT),
        name="dsa",
    )(z3, z3, z3, z3, z3, gq, gk, bias)


def _mix_residual(x_ref, oa_ref, ob_ref, oc_ref, wo_ref):
    mix = (_dot(oa_ref[...].astype(MXU_DTYPE), wo_ref[0:A_WIDTH, :])
           + _dot(ob_ref[...].astype(MXU_DTYPE), wo_ref[A_WIDTH:A_WIDTH + S5_WIDTH, :])
           + _dot(oc_ref[...].astype(MXU_DTYPE), wo_ref[A_WIDTH + S5_WIDTH:, :]))
    return x_ref[...] + mix


def _ffn_chunk(xn, g, g1, g2, sl, wu_ref, cw_ref, cb_ref, wd_ref):
    up = _dot(xn, wu_ref[:, sl])
    gc = cw_ref[0:1, sl] * g2 + cw_ref[1:2, sl] * g1 + cw_ref[2:3, sl] * g + cb_ref[:, sl]
    return _dot((jax.nn.silu(gc) * up).astype(MXU_DTYPE), wd_ref[sl, :])


def _out_ffn_kernel(x_ref, oa_ref, ob_ref, oc_ref, wo_ref, g2_ref, wg_ref, wu_ref, cw_ref, cb_ref, wd_ref,
                    y_ref, conv_ref, carry_s, *, tiles_per_seq, tf):
    i = pl.program_id(0)
    tm = x_ref.shape[0]
    x1 = _mix_residual(x_ref, oa_ref, ob_ref, oc_ref, wo_ref)
    xn = _rms(x1, g2_ref[...]).astype(MXU_DTYPE)

    @pl.when(i % tiles_per_seq == 0)
    def _seq_start():
        carry_s[...] = jnp.zeros(carry_s.shape, F32)

    row = lax.broadcasted_iota(jnp.int32, (tm, 1), 0)
    acc = x1
    for f in range(FFN_DIM // tf):
        sl = slice(f * tf, (f + 1) * tf)
        g = _dot(xn, wg_ref[:, sl])
        prev2, prev1 = carry_s[0:1, sl], carry_s[1:2, sl]
        g1 = jnp.where(row == 0, prev1, pltpu.roll(g, 1, 0))
        g2 = jnp.where(row == 0, prev2, jnp.where(row == 1, prev1, pltpu.roll(g, 2, 0)))
        acc = acc + _ffn_chunk(xn, g, g1, g2, sl, wu_ref, cw_ref, cb_ref, wd_ref)
        carry_s[0:2, sl] = g[tm - 2:tm, :]
    y_ref[...] = acc
    conv_ref[0] = carry_s[0:2, :]


def _out_ffn(x2d, oa, ob, oc, wo, g2, wg, wu, cw, cb, wd, *, seq, tm, tf):
    m = x2d.shape[0]
    tiles_per_seq = seq // tm
    kern = functools.partial(_out_ffn_kernel, tiles_per_seq=tiles_per_seq, tf=tf)
    row = lambda w: pl.BlockSpec((tm, w), lambda i: (i, 0))
    const = lambda shape: pl.BlockSpec(shape, lambda i: (0,) * len(shape), pipeline_mode=pl.Buffered(1))
    return pl.pallas_call(
        kern,
        grid=(m // tm,),
        in_specs=[
            row(D_MODEL), row(A_WIDTH), row(S5_WIDTH), row(C_WIDTH),
            const((D_MODEL, D_MODEL)), const((1, D_MODEL)),
            const((D_MODEL, FFN_DIM)), const((D_MODEL, FFN_DIM)),
            const((CONV_W, FFN_DIM)), const((1, FFN_DIM)), const((FFN_DIM, D_MODEL)),
        ],
        out_specs=[
            row(D_MODEL),
            pl.BlockSpec((1, CONV_W - 1, FFN_DIM), lambda i: (i // tiles_per_seq, 0, 0)),
        ],
        out_shape=[
            jax.ShapeDtypeStruct((m, D_MODEL), F32),
            jax.ShapeDtypeStruct((m // seq, CONV_W - 1, FFN_DIM), F32),
        ],
        scratch_shapes=[pltpu.VMEM((8, FFN_DIM), F32)],
        compiler_params=pltpu.CompilerParams(dimension_semantics=("arbitrary",), vmem_limit_bytes=VMEM_LIMIT),
        name="out_ffn",
    )(x2d, oa, ob, oc, wo, g2, wg, wu, cw, cb, wd)


def _layer_params(l, t5_bias, norm1_g, w_in, a_q_gain, a_k_gain, a_rel_bias, s5_lam_re, s5_lam_im, s5_log_dt,
                  s5_b_re, s5_b_im, s5_c_re, s5_c_im, s5_d, s5_w_glu, s5_b_glu, c_q_gain, c_k_gain, w_out,
                  norm2_g, ffn_w_gate, ffn_w_up, ffn_conv_w, ffn_conv_b, ffn_w_down):
    w = w_in[l]
    sizes = (A_WIDTH, A_WIDTH, A_WIDTH, S5_WIDTH, C_WIDTH, HEAD_DIM, HEAD_DIM, IDX_HEADS * IDX_DIM, IDX_DIM, IDX_HEADS)
    cuts = [0]
    for s in sizes:
        cuts.append(cuts[-1] + s)
    aq, ak, av, u, cq, ck, cv, qi, ki, wi = [w[:, cuts[n]:cuts[n + 1]] for n in range(len(sizes))]
    pad = jnp.zeros((D_MODEL, Z_WIDTH - COL_KIW - IDX_DIM - IDX_HEADS), w.dtype)
    w_r = jnp.concatenate([aq, ak, av, cq, u, qi, ck, cv, ki, wi, pad], axis=1).astype(MXU_DTYPE)
    return dict(
        norm1=norm1_g[l].reshape(1, D_MODEL), w_in=w_r,
        a_gq=a_q_gain[l].reshape(1, HEAD_DIM), a_gk=a_k_gain[l].reshape(1, HEAD_DIM), a_rel=a_rel_bias[l],
        s5=_s5_params(s5_lam_re[l], s5_lam_im[l], s5_log_dt[l], s5_b_re[l], s5_b_im[l], s5_c_re[l], s5_c_im[l]),
        s5_d=s5_d[l].reshape(1, S5_WIDTH), s5_wg=s5_w_glu[l].astype(MXU_DTYPE), s5_bg=s5_b_glu[l].reshape(1, S5_WIDTH),
        c_gq=c_q_gain[l].reshape(1, HEAD_DIM), c_gk=c_k_gain[l].reshape(1, HEAD_DIM),
        w_out=w_out[l].astype(MXU_DTYPE), norm2=norm2_g[l].reshape(1, D_MODEL),
        wg=ffn_w_gate[l].astype(MXU_DTYPE), wu=ffn_w_up[l].astype(MXU_DTYPE), cw=ffn_conv_w[l],
        cb=ffn_conv_b[l].reshape(1, FFN_DIM), wd=ffn_w_down[l].astype(MXU_DTYPE),
    )


def _prompt_layer(x, p, dsa_bias, *, tm, tf, s5_bt, s5_tc):
    b, seq, _ = x.shape
    x2d = x.reshape(b * seq, D_MODEL)
    z3 = _in_proj(x2d, p["norm1"], p["w_in"], tm).reshape(b, seq, Z_WIDTH)
    oa, a_k, a_v = _attn_a(z3, _attn_a_bias(p["a_rel"]), p["a_gq"], p["a_gk"])
    zero_h = jnp.zeros((b, S5_FLAT), F32)
    ob, h_re, h_im = _s5(z3, zero_h, zero_h, p["s5"], p["s5_d"], p["s5_wg"], p["s5_bg"],
                         bt=s5_bt, tc=s5_tc, exact_in=False)
    oc, c_k, c_v, c_ki = _dsa(z3, dsa_bias, p["c_gq"], p["c_gk"])
    y, conv = _out_ffn(x2d, oa.reshape(b * seq, A_WIDTH), ob.reshape(b * seq, S5_WIDTH),
                       oc.reshape(b * seq, C_WIDTH), p["w_out"], p["norm2"], p["wg"], p["wu"], p["cw"], p["cb"],
                       p["wd"], seq=seq, tm=tm, tf=tf)
    keep = a_k.shape[1]
    states = (a_k.reshape(b, keep, A_HEADS, HEAD_DIM), a_v.reshape(b, keep, A_HEADS, HEAD_DIM),
              h_re.reshape(b, S5_GROUPS, S5_STATE), h_im.reshape(b, S5_GROUPS, S5_STATE), c_k, c_v, c_ki, conv)
    return y.reshape(b, seq, D_MODEL), states


def _attn_a_step_kernel(q_ref, k_ref, v_ref, ck_ref, cv_ref, bc_ref, bn_ref, gq_ref, gk_ref, o_ref, ak_ref, av_ref):
    v = v_ref[0]
    av_ref[0] = v
    for h in range(A_HEADS):
        sl = slice(h * HEAD_DIM, (h + 1) * HEAD_DIM)
        qh = (_rms(q_ref[0, :, sl], gq_ref[...]) * HEAD_DIM ** -0.5).astype(MXU_DTYPE)
        khn = _rms(k_ref[0, :, sl], gk_ref[...])
        ak_ref[0, :, sl] = khn
        s_c = _nt_dot(qh, ck_ref[0, :, sl].astype(MXU_DTYPE)) + bc_ref[h]
        s_n = _nt_dot(qh, khn.astype(MXU_DTYPE)) + bn_ref[h]
        m = jnp.maximum(jnp.max(s_c, axis=-1, keepdims=True), jnp.max(s_n, axis=-1, keepdims=True))
        p_c = jnp.exp(s_c - m)
        p_n = jnp.exp(s_n - m)
        denom = jnp.sum(p_c, axis=-1, keepdims=True) + jnp.sum(p_n, axis=-1, keepdims=True)
        oh = (_dot(p_c.astype(MXU_DTYPE), cv_ref[0, :, sl].astype(MXU_DTYPE))
              + _dot(p_n.astype(MXU_DTYPE), v[:, sl].astype(MXU_DTYPE)))
        o_ref[0, :, sl] = oh / denom


def _attn_a_step_bias(rel_table, past, t_new):
    t = jnp.arange(t_new)[:, None]
    rel = jnp.concatenate([jnp.arange(past) - past, jnp.arange(t_new)])[None, :] - t
    bias = rel_table.astype(F32)[:, jnp.clip(rel, -A_MAX_REL, A_MAX_REL) + A_MAX_REL]
    return bias[:, :, :past], bias[:, :, past:]


def _attn_a_step(z3, cache_k, cache_v, layer, bias_c, bias_n, gq, gk):
    b, t_new, _ = z3.shape
    past = cache_k.shape[2]
    new = lambda col: pl.BlockSpec((1, t_new, A_WIDTH), lambda i: (i, 0, col // A_WIDTH))
    const = lambda shape: pl.BlockSpec(shape, lambda i: (0,) * len(shape))
    return pl.pallas_call(
        _attn_a_step_kernel,
        grid=(b,),
        in_specs=[
            new(COL_AQ), new(COL_AK), new(COL_AV),
            pl.BlockSpec((None, 1, past, A_WIDTH), lambda i: (layer, i, 0, 0)),
            pl.BlockSpec((None, 1, past, A_WIDTH), lambda i: (layer, i, 0, 0)),
            const((A_HEADS, t_new, past)), const((A_HEADS, t_new, t_new)),
            const((1, HEAD_DIM)), const((1, HEAD_DIM)),
        ],
        out_specs=[pl.BlockSpec((1, t_new, A_WIDTH), lambda i: (i, 0, 0))] * 3,
        out_shape=[jax.ShapeDtypeStruct((b, t_new, A_WIDTH), F32)] * 3,
        compiler_params=pltpu.CompilerParams(dimension_semantics=("parallel",), vmem_limit_bytes=VMEM_LIMIT),
        name="attn_a_step",
    )(z3, z3, z3, cache_k, cache_v, bias_c, bias_n, gq, gk)


def _dsa_step_kernel(cq_ref, ckv_ref, qi_ref, kiw_ref, ck_ref, cv_ref, cki_ref, gq_ref, gk_ref, bias_ref,
                     o_ref, ck_out, cv_out, cki_out, *, past, t_new, topk):
    n_keys = past + t_new
    kn_new = _rms(ckv_ref[0, :, 0:HEAD_DIM], gk_ref[...])
    cv_new = ckv_ref[0, :, HEAD_DIM:2 * HEAD_DIM]
    ki_new = kiw_ref[0, :, 0:IDX_DIM]
    ck_out[0] = kn_new
    cv_out[0] = cv_new
    cki_out[0] = ki_new
    k_all = jnp.concatenate([ck_ref[0], kn_new], axis=0).astype(MXU_DTYPE)
    v_all = jnp.concatenate([cv_ref[0], cv_new], axis=0).astype(MXU_DTYPE)
    ki_all = jnp.concatenate([cki_ref[0], ki_new], axis=0).astype(MXU_DTYPE)
    v_t = _nt_dot(_eye(HEAD_DIM, MXU_DTYPE), v_all).astype(MXU_DTYPE)

    def spread(groups, dtype):
        r = lax.broadcasted_iota(jnp.int32, (t_new, groups * t_new), 0)
        c = lax.broadcasted_iota(jnp.int32, (t_new, groups * t_new), 1)
        return (c % t_new == r).astype(dtype)

    qi = qi_ref[0].astype(MXU_DTYPE)
    qi_all = jnp.concatenate([qi[:, h * IDX_DIM:(h + 1) * IDX_DIM] for h in range(IDX_HEADS)], axis=0)
    kiw_pad = jnp.concatenate([kiw_ref[0], jnp.zeros((128 - t_new, 128), F32)], axis=0)
    w_t = kiw_pad.T[:, 0:t_new]
    w_flat = jnp.concatenate([w_t[IDX_DIM + h:IDX_DIM + h + 1, :] for h in range(IDX_HEADS)], axis=1) * INDEX_SCALE
    weighted = w_flat * jnp.maximum(_nt_dot(ki_all, qi_all), 0.0)
    score = lax.dot_general(weighted, spread(IDX_HEADS, F32), (((1,), (1,)), ((), ())),
                            preferred_element_type=F32, precision=lax.Precision.HIGHEST)
    pos = lax.broadcasted_iota(jnp.int32, (n_keys, 1), 0)
    q_chunk = (past + lax.broadcasted_iota(jnp.int32, (1, t_new), 1)) // CHUNK
    keys = jnp.where((pos // CHUNK) <= q_chunk, _sortable_key(score), jnp.int32(INT_MIN))

    def count(pred):
        return jnp.sum(jnp.where(pred(keys, pos), 1, 0), axis=0, keepdims=True)

    def kth_body(it, res):
        cand = res | lax.shift_left(jnp.int32(1), jnp.int32(31) - it)
        thr_c = cand ^ jnp.int32(INT_MIN)
        return jnp.where(count(lambda k, p: k >= thr_c) >= topk, cand, res)

    thr = lax.fori_loop(0, 32, kth_body, jnp.zeros((1, t_new), jnp.int32)) ^ jnp.int32(INT_MIN)
    real_thr = thr != jnp.int32(INT_MIN)
    pos_bits = (n_keys - 1).bit_length()

    def _tie_search():
        need = topk - count(lambda k, p: k > thr)

        def tie_body(it, v):
            cand = v | lax.shift_left(jnp.int32(1), jnp.int32(pos_bits - 1) - it)
            return jnp.where(count(lambda k, p: (k == thr) & (p < cand)) < need, cand, v)

        return lax.fori_loop(0, pos_bits, tie_body, jnp.zeros((1, t_new), jnp.int32))

    has_ties = jnp.max(jnp.where(real_thr & (count(lambda k, p: k >= thr) > topk), 1, 0)) > 0
    tie_last = lax.cond(has_ties, _tie_search, lambda: jnp.full((1, t_new), 2 ** pos_bits - 1, jnp.int32))
    tie_last = jnp.where(real_thr, tie_last, -1)
    dropped = jnp.where((keys > thr) | ((keys == thr) & (pos <= tie_last)), 0.0, 1.0).astype(MXU_DTYPE)
    negm = _dot(dropped, spread(C_HEADS, MXU_DTYPE)) * NEG_INF

    cq = cq_ref[0]
    q_all = jnp.concatenate(
        [(_rms(cq[:, h * HEAD_DIM:(h + 1) * HEAD_DIM], gq_ref[...]) * HEAD_DIM ** -0.5).astype(MXU_DTYPE)
         for h in range(C_HEADS)], axis=0)
    s = _nt_dot(k_all, q_all) + bias_ref[...] + negm
    p = jnp.exp(s - jnp.max(s, axis=0, keepdims=True))
    out_t = _dot(v_t, p.astype(MXU_DTYPE)) / jnp.sum(p, axis=0, keepdims=True)
    hq = C_HEADS * t_new
    out_sq = jnp.concatenate([jnp.concatenate([out_t, jnp.zeros((HEAD_DIM, 128 - hq), F32)], axis=1),
                              jnp.zeros((128 - HEAD_DIM, 128), F32)], axis=0).T
    o_ref[0] = jnp.concatenate([out_sq[h * t_new:(h + 1) * t_new, 0:HEAD_DIM] for h in range(C_HEADS)], axis=1)


def _dsa_step_bias(t5_table, past, t_new):
    s = jnp.arange(past + t_new)[:, None]
    t = jnp.arange(t_new)[None, :]
    return t5_table.astype(F32)[_t5_bucket(s - (past + t))].transpose(0, 2, 1).reshape(past + t_new, C_HEADS * t_new)


def _dsa_step(z3, cache_k, cache_v, cache_ki, layer, bias, gq, gk):
    b, t_new, _ = z3.shape
    past = cache_k.shape[2]
    topk = min(TOPK_MAX, (past + t_new) // 4)
    kern = functools.partial(_dsa_step_kernel, past=past, t_new=t_new, topk=topk)
    new = lambda w, col: pl.BlockSpec((1, t_new, w), lambda i: (i, 0, col // w))
    per_seq = lambda n, w: pl.BlockSpec((1, n, w), lambda i: (i, 0, 0))
    cached = lambda w: pl.BlockSpec((None, 1, past, w), lambda i: (layer, i, 0, 0))
    const = lambda shape: pl.BlockSpec(shape, lambda i: (0,) * len(shape))
    return pl.pallas_call(
        kern,
        grid=(b,),
        in_specs=[
            new(C_WIDTH, COL_CQ), new(128, COL_CKV), new(256, COL_QI), new(128, COL_KIW),
            cached(HEAD_DIM), cached(HEAD_DIM), cached(IDX_DIM),
            const((1, HEAD_DIM)), const((1, HEAD_DIM)), const((past + t_new, C_HEADS * t_new)),
        ],
        out_specs=[per_seq(t_new, C_WIDTH), per_seq(t_new, HEAD_DIM), per_seq(t_new, HEAD_DIM),
                   per_seq(t_new, IDX_DIM)],
        out_shape=[
            jax.ShapeDtypeStruct((b, t_new, C_WIDTH), F32),
            jax.ShapeDtypeStruct((b, t_new, HEAD_DIM), F32),
            jax.ShapeDtypeStruct((b, t_new, HEAD_DIM), F32),
            jax.ShapeDtypeStruct((b, t_new, IDX_DIM), F32),
        ],
        compiler_params=pltpu.CompilerParams(dimension_semantics=("parallel",), vmem_limit_bytes=VMEM_LIMIT),
        name="dsa_step",
    )(z3, z3, z3, z3, cache_k, cache_v, cache_ki, gq, gk, bias)


def _out_ffn_step_kernel(x_ref, oa_ref, ob_ref, oc_ref, wo_ref, g2_ref, wg_ref, wu_ref, cw_ref, cb_ref, wd_ref,
                         e1_ref, e2_ref, y_ref, gate_ref, *, t_new, tf):
    tm = x_ref.shape[0]
    x1 = _mix_residual(x_ref, oa_ref, ob_ref, oc_ref, wo_ref)
    xn = _rms(x1, g2_ref[...]).astype(MXU_DTYPE)
    t = lax.broadcasted_iota(jnp.int32, (tm, 1), 0) % t_new
    acc = x1
    for f in range(FFN_DIM // tf):
        sl = slice(f * tf, (f + 1) * tf)
        g = _dot(xn, wg_ref[:, sl])
        gate_ref[:, sl] = g
        g1 = jnp.where(t == 0, e1_ref[:, sl], pltpu.roll(g, 1, 0))
        g2 = jnp.where(t <= 1, e2_ref[:, sl], pltpu.roll(g, 2, 0))
        acc = acc + _ffn_chunk(xn, g, g1, g2, sl, wu_ref, cw_ref, cb_ref, wd_ref)
    y_ref[...] = acc


def _out_ffn_step(x2d, oa, ob, oc, wo, g2, wg, wu, cw, cb, wd, conv_prev, *, t_new, tf):
    m = x2d.shape[0]
    b = m // t_new
    e1 = jnp.zeros((b, t_new, FFN_DIM), F32).at[:, 0].set(conv_prev[:, 1]).reshape(m, FFN_DIM)
    e2 = jnp.zeros((b, t_new, FFN_DIM), F32).at[:, 0].set(conv_prev[:, 0]).at[:, 1].set(conv_prev[:, 1])
    e2 = e2.reshape(m, FFN_DIM)
    kern = functools.partial(_out_ffn_step_kernel, t_new=t_new, tf=tf)
    full = lambda shape: pl.BlockSpec(shape, lambda i: (0,) * len(shape))
    y, gate = pl.pallas_call(
        kern,
        grid=(1,),
        in_specs=[
            full((m, D_MODEL)), full((m, A_WIDTH)), full((m, S5_WIDTH)), full((m, C_WIDTH)),
            full((D_MODEL, D_MODEL)), full((1, D_MODEL)),
            full((D_MODEL, FFN_DIM)), full((D_MODEL, FFN_DIM)),
            full((CONV_W, FFN_DIM)), full((1, FFN_DIM)), full((FFN_DIM, D_MODEL)),
            full((m, FFN_DIM)), full((m, FFN_DIM)),
        ],
        out_specs=[full((m, D_MODEL)), full((m, FFN_DIM))],
        out_shape=[jax.ShapeDtypeStruct((m, D_MODEL), F32), jax.ShapeDtypeStruct((m, FFN_DIM), F32)],
        compiler_params=pltpu.CompilerParams(dimension_semantics=("arbitrary",), vmem_limit_bytes=VMEM_LIMIT),
        name="out_ffn_step",
    )(x2d, oa, ob, oc, wo, g2, wg, wu, cw, cb, wd, e1, e2)
    return y, gate.reshape(b, t_new, FFN_DIM)[:, t_new - (CONV_W - 1):]


def _sample_layer(x, p, dsa_bias, layer, ca_k, ca_v, h_re0, h_im0, cc_k, cc_v, cc_ki, conv_prev, *, tf):
    b, t_new, _ = x.shape
    m = b * t_new
    x2d = x.reshape(m, D_MODEL)
    z3 = _in_proj(x2d, p["norm1"], p["w_in"], m).reshape(b, t_new, Z_WIDTH)
    depth, _, a_past = ca_k.shape[:3]
    bias_c, bias_n = _attn_a_step_bias(p["a_rel"], a_past, t_new)
    oa, a_k, a_v = _attn_a_step(z3, ca_k.reshape(depth, b, a_past, A_WIDTH), ca_v.reshape(depth, b, a_past, A_WIDTH),
                                layer, bias_c, bias_n, p["a_gq"], p["a_gk"])
    ob, h_re, h_im = _s5(z3, h_re0.reshape(b, S5_FLAT), h_im0.reshape(b, S5_FLAT), p["s5"], p["s5_d"], p["s5_wg"],
                         p["s5_bg"], bt=8, tc=t_new, exact_in=True)
    oc, c_k, c_v, c_ki = _dsa_step(z3, cc_k, cc_v, cc_ki, layer, dsa_bias, p["c_gq"], p["c_gk"])
    y, conv = _out_ffn_step(x2d, oa.reshape(m, A_WIDTH), ob.reshape(m, S5_WIDTH), oc.reshape(m, C_WIDTH),
                            p["w_out"], p["norm2"], p["wg"], p["wu"], p["cw"], p["cb"], p["wd"], conv_prev,
                            t_new=t_new, tf=tf)
    states = (a_k.reshape(b, t_new, A_HEADS, HEAD_DIM), a_v.reshape(b, t_new, A_HEADS, HEAD_DIM),
              h_re.reshape(b, S5_GROUPS, S5_STATE), h_im.reshape(b, S5_GROUPS, S5_STATE), c_k, c_v, c_ki, conv)
    return y.reshape(b, t_new, D_MODEL), states


def kernel(x_prompt, x_sample, cache_a_k, cache_a_v, state_s5_re, state_s5_im, cache_c_k, cache_c_v, cache_c_idx_k,
           state_ffn_conv, t5_bias, norm1_g, w_in, a_q_gain, a_k_gain, a_rel_bias, s5_lam_re, s5_lam_im, s5_log_dt,
           s5_b_re, s5_b_im, s5_c_re, s5_c_im, s5_d, s5_w_glu, s5_b_glu, c_q_gain, c_k_gain, w_out, norm2_g,
           ffn_w_gate, ffn_w_up, ffn_conv_w, ffn_conv_b, ffn_w_down):
    depth = w_in.shape[0]
    dsa_bias = _dsa_bias_tiles(t5_bias)
    dsa_step_bias = _dsa_step_bias(t5_bias, cache_c_k.shape[2], x_sample.shape[1])
    yp, ys = x_prompt, x_sample
    prompt_states, sample_states = [], []
    for l in range(depth):
        p = _layer_params(l, t5_bias, norm1_g, w_in, a_q_gain, a_k_gain, a_rel_bias, s5_lam_re, s5_lam_im, s5_log_dt,
                          s5_b_re, s5_b_im, s5_c_re, s5_c_im, s5_d, s5_w_glu, s5_b_glu, c_q_gain, c_k_gain, w_out,
                          norm2_g, ffn_w_gate, ffn_w_up, ffn_conv_w, ffn_conv_b, ffn_w_down)
        yp, st_p = _prompt_layer(yp, p, dsa_bias, tm=512, tf=FFN_DIM, s5_bt=8, s5_tc=256)
        ys, st_s = _sample_layer(ys, p, dsa_step_bias, l, cache_a_k, cache_a_v, state_s5_re[l], state_s5_im[l],
                                 cache_c_k, cache_c_v, cache_c_idx_k, state_ffn_conv[l], tf=FFN_DIM)
        prompt_states.append(st_p)
        sample_states.append(st_s)
    (a_k_p, a_v_p, s5_re_p, s5_im_p, c_k_p, c_v_p, c_ki_p, conv_p) = [jnp.stack(z) for z in zip(*prompt_states)]
    (a_k_s, a_v_s, s5_re_s, s5_im_s, c_k_s, c_v_s, c_ki_s, conv_s) = [jnp.stack(z) for z in zip(*sample_states)]
    return (yp, ys, a_k_p, a_v_p, a_k_s, a_v_s, s5_re_p, s5_im_p, s5_re_s, s5_im_s,
            c_k_p, c_v_p, c_ki_p, c_k_s, c_v_s, c_ki_s, conv_p, conv_s)
```

```python
import functools
import math

import jax
import jax.numpy as jnp
from jax import lax
from jax.experimental import pallas as pl
from jax.experimental.pallas import tpu as pltpu

F32 = jnp.float32
MXU_DTYPE = jnp.bfloat16

D_MODEL = 1024
CHUNK = 64
HEAD_DIM = 64
A_HEADS = 6
A_WIDTH = A_HEADS * HEAD_DIM
A_BAND_PAST = 8 * CHUNK
A_MAX_REL = 128
S5_GROUPS = 16
S5_GROUP_CH = 16
S5_WIDTH = S5_GROUPS * S5_GROUP_CH
S5_STATE = 64
S5_FLAT = S5_GROUPS * S5_STATE
C_HEADS = 6
C_WIDTH = C_HEADS * HEAD_DIM
IDX_HEADS = 8
IDX_DIM = 32
INDEX_SCALE = (IDX_HEADS * IDX_DIM) ** -0.5
TOPK_MAX = 256
Q_BLOCK = 128
T5_BUCKETS = 32
T5_MAX_DIST = 128
FFN_DIM = 2816
CONV_W = 3
EPS = 1e-6
NEG_INF = -1e30
INT_MIN = -(2 ** 31)

COL_AQ, COL_AK, COL_AV, COL_CQ = 0, 384, 768, 1152
COL_U, COL_QI, COL_CKV, COL_KIW = 1536, 1792, 2048, 2176
Z_WIDTH = 2304
VMEM_LIMIT = 56 * 1024 * 1024


def _nt_dot(a, b):
    return lax.dot_general(a, b, (((1,), (1,)), ((), ())), preferred_element_type=F32)


def _dot(a, b):
    return jnp.dot(a, b, preferred_element_type=F32)


def _rms(x, g):
    return x * lax.rsqrt(jnp.mean(x * x, axis=-1, keepdims=True) + EPS) * g


def _in_proj_kernel(x_ref, g_ref, w_ref, z_ref):
    xn = _rms(x_ref[...], g_ref[...]).astype(MXU_DTYPE)
    z_ref[...] = _dot(xn, w_ref[...])


def _in_proj(x2d, g, w_r, tm):
    m = x2d.shape[0]
    return pl.pallas_call(
        _in_proj_kernel,
        grid=(m // tm,),
        in_specs=[
            pl.BlockSpec((tm, D_MODEL), lambda i: (i, 0)),
            pl.BlockSpec((1, D_MODEL), lambda i: (0, 0)),
            pl.BlockSpec((D_MODEL, Z_WIDTH), lambda i: (0, 0)),
        ],
        out_specs=pl.BlockSpec((tm, Z_WIDTH), lambda i: (i, 0)),
        out_shape=jax.ShapeDtypeStruct((m, Z_WIDTH), F32),
        compiler_params=pltpu.CompilerParams(dimension_semantics=("parallel",), vmem_limit_bytes=VMEM_LIMIT),
        name="in_proj",
    )(x2d, g, w_r)


A_QB = 2 * CHUNK
A_BAND = A_BAND_PAST + A_QB


def _eye(n, dtype):
    return (lax.broadcasted_iota(jnp.int32, (n, n), 0) == lax.broadcasted_iota(jnp.int32, (n, n), 1)).astype(dtype)


def _attn_a_kernel(q_ref, k_ref, v_ref, bias_ref, gq_ref, gk_ref, o_ref, ak_ref, av_ref, kn_s, vt_s, s_s, p_s,
                   *, seq, keep):
    j = pl.program_id(1)

    @pl.when(j == 0)
    def _prep():
        for h in range(A_HEADS):
            sl = slice(h * HEAD_DIM, (h + 1) * HEAD_DIM)
            khn = _rms(k_ref[0, :, sl], gk_ref[...])
            kn_s[h, 0:A_BAND_PAST, :] = jnp.zeros((A_BAND_PAST, HEAD_DIM), MXU_DTYPE)
            kn_s[h, A_BAND_PAST:A_BAND_PAST + seq, :] = khn.astype(MXU_DTYPE)
            ak_ref[0, :, sl] = khn[seq - keep:, :]
        v = v_ref[0]
        av_ref[0] = v[seq - keep:, :]
        vt_s[:, 0:A_BAND_PAST] = jnp.zeros((A_WIDTH, A_BAND_PAST), MXU_DTYPE)
        vt_s[:, A_BAND_PAST:A_BAND_PAST + seq] = _nt_dot(_eye(A_WIDTH, MXU_DTYPE), v.astype(MXU_DTYPE)).astype(MXU_DTYPE)

    start = pl.multiple_of(j * A_QB, A_QB)
    variant = jnp.minimum(j, A_LEAD_STEPS)
    for h in range(A_HEADS):
        sl = slice(h * HEAD_DIM, (h + 1) * HEAD_DIM)
        qh = (_rms(q_ref[0, :, sl], gq_ref[...]) * HEAD_DIM ** -0.5).astype(MXU_DTYPE)
        s_s[h] = _nt_dot(kn_s[h, pl.ds(start, A_BAND), :], qh) + bias_ref[variant, h]
    denoms = []
    for h in range(A_HEADS):
        s = s_s[h]
        p = jnp.exp(s - jnp.max(s, axis=0, keepdims=True))
        denoms.append(jnp.sum(p, axis=0, keepdims=True))
        p_s[h] = p.astype(MXU_DTYPE)
    outs = [_dot(vt_s[h * HEAD_DIM:(h + 1) * HEAD_DIM, pl.ds(start, A_BAND)], p_s[h]) / denoms[h]
            for h in range(A_HEADS)]
    o_ref[0] = jnp.concatenate(outs, axis=0).T


A_LEAD_STEPS = A_BAND_PAST // A_QB


def _attn_a_bias(rel_table):
    n_ext = A_BAND + A_QB - 1
    m = jnp.arange(n_ext)
    ext = rel_table.astype(F32)[:, jnp.clip(m - (A_QB - 1) - A_BAND_PAST, -A_MAX_REL, A_MAX_REL) + A_MAX_REL]
    rows = jnp.pad(jnp.broadcast_to(ext[:, None, :], (A_HEADS, A_QB, n_ext)), ((0, 0), (0, 0), (0, 1)))
    skew = rows.reshape(A_HEADS, A_QB * (n_ext + 1))[:, :A_QB * n_ext].reshape(A_HEADS, A_QB, n_ext)
    bias = skew[:, :, A_QB - 1:].transpose(0, 2, 1)
    c = jnp.arange(A_BAND)[:, None]
    lo = (jnp.arange(A_QB)[None, :] // CHUNK) * CHUNK
    in_band = (c >= lo) & (c < lo + A_BAND_PAST + CHUNK)
    first_live = (A_LEAD_STEPS - jnp.arange(A_LEAD_STEPS + 1)) * A_QB
    live = in_band[None] & (c[None] >= first_live[:, None, None])
    return jnp.where(live[:, None], bias[None], NEG_INF)


def _attn_a(z3, bias, gq, gk):
    b, seq, _ = z3.shape
    keep = min(A_BAND_PAST, seq)
    kern = functools.partial(_attn_a_kernel, seq=seq, keep=keep)
    return pl.pallas_call(
        kern,
        grid=(b, seq // A_QB),
        in_specs=[
            pl.BlockSpec((1, A_QB, A_WIDTH), lambda i, j: (i, j, COL_AQ // A_WIDTH)),
            pl.BlockSpec((1, seq, A_WIDTH), lambda i, j: (i, 0, COL_AK // A_WIDTH)),
            pl.BlockSpec((1, seq, A_WIDTH), lambda i, j: (i, 0, COL_AV // A_WIDTH)),
            pl.BlockSpec((A_LEAD_STEPS + 1, A_HEADS, A_BAND, A_QB), lambda i, j: (0, 0, 0, 0),
                         pipeline_mode=pl.Buffered(1)),
            pl.BlockSpec((1, HEAD_DIM), lambda i, j: (0, 0)),
            pl.BlockSpec((1, HEAD_DIM), lambda i, j: (0, 0)),
        ],
        out_specs=[
            pl.BlockSpec((1, A_QB, A_WIDTH), lambda i, j: (i, j, 0)),
            pl.BlockSpec((1, keep, A_WIDTH), lambda i, j: (i, 0, 0)),
            pl.BlockSpec((1, keep, A_WIDTH), lambda i, j: (i, 0, 0)),
        ],
        out_shape=[
            jax.ShapeDtypeStruct((b, seq, A_WIDTH), F32),
            jax.ShapeDtypeStruct((b, keep, A_WIDTH), F32),
            jax.ShapeDtypeStruct((b, keep, A_WIDTH), F32),
        ],
        scratch_shapes=[
            pltpu.VMEM((A_HEADS, A_BAND_PAST + seq, HEAD_DIM), MXU_DTYPE),
            pltpu.VMEM((A_WIDTH, A_BAND_PAST + seq), MXU_DTYPE),
            pltpu.VMEM((A_HEADS, A_BAND, A_QB), F32),
            pltpu.VMEM((A_HEADS, A_BAND, A_QB), MXU_DTYPE),
        ],
        compiler_params=pltpu.CompilerParams(
            dimension_semantics=("parallel", "arbitrary"), vmem_limit_bytes=VMEM_LIMIT),
        name="attn_a",
    )(z3, z3, z3, bias, gq, gk)


def _s5_kernel(u_ref, h0r_ref, h0i_ref, ar_ref, ai_ref, wbr_ref, wbi_ref, wcr_ref, wci_ref, d_ref, wg_ref, bg_ref,
               o_ref, hr_out, hi_out, xr_s, xi_s, hr_c, hi_c, *, bt, tc, exact_in):
    c = pl.program_id(1)

    @pl.when(c == 0)
    def _init():
        hr_c[...] = h0r_ref[...]
        hi_c[...] = h0i_ref[...]

    u = jnp.swapaxes(u_ref[...], 0, 1).reshape(tc * bt, S5_WIDTH)
    if exact_in:
        xr_s[...] = jnp.dot(u, wbr_ref[...], preferred_element_type=F32, precision=lax.Precision.HIGHEST)
        xi_s[...] = jnp.dot(u, wbi_ref[...], preferred_element_type=F32, precision=lax.Precision.HIGHEST)
    else:
        ub = u.astype(MXU_DTYPE)
        xr_s[...] = _dot(ub, wbr_ref[...])
        xi_s[...] = _dot(ub, wbi_ref[...])
    ar = ar_ref[...]
    ai = ai_ref[...]

    def step(t, carry):
        hr, hi = carry
        rows = pl.ds(pl.multiple_of(t * bt, bt), bt)
        nhr = ar * hr - ai * hi + xr_s[rows, :]
        nhi = ar * hi + ai * hr + xi_s[rows, :]
        xr_s[rows, :] = nhr
        xi_s[rows, :] = nhi
        return nhr, nhi

    hr, hi = lax.fori_loop(0, tc, step, (hr_c[...], hi_c[...]))
    hr_c[...] = hr
    hi_c[...] = hi
    hr_out[...] = hr
    hi_out[...] = hi
    y = (_dot(xr_s[...].astype(MXU_DTYPE), wcr_ref[...]) - _dot(xi_s[...].astype(MXU_DTYPE), wci_ref[...])
         + d_ref[...] * u)
    g = jax.nn.gelu(y)
    out = g * jax.nn.sigmoid(_dot(g.astype(MXU_DTYPE), wg_ref[...]) + bg_ref[...])
    o_ref[...] = jnp.swapaxes(out.reshape(tc, bt, S5_WIDTH), 0, 1)


def _s5_params(lam_re, lam_im, log_dt, b_re, b_im, c_re, c_im):
    dt = jnp.exp(log_dt.astype(F32))[:, None]
    lr, li = lam_re.astype(F32), lam_im.astype(F32)
    mag = jnp.exp(lr * dt)
    ab_re, ab_im = mag * jnp.cos(li * dt), mag * jnp.sin(li * dt)
    den = lr * lr + li * li
    cr = ((ab_re - 1.0) * lr + ab_im * li) / den
    ci = (ab_im * lr - (ab_re - 1.0) * li) / den
    br, bi = b_re.astype(F32), b_im.astype(F32)
    bb_re = cr[..., None] * br - ci[..., None] * bi
    bb_im = cr[..., None] * bi + ci[..., None] * br
    eye = jnp.eye(S5_GROUPS, dtype=F32)
    wb_re = jnp.einsum("gpc,gh->gchp", bb_re, eye).reshape(S5_WIDTH, S5_FLAT)
    wb_im = jnp.einsum("gpc,gh->gchp", bb_im, eye).reshape(S5_WIDTH, S5_FLAT)
    wc_re = jnp.einsum("gcp,gh->gphc", c_re.astype(F32), eye).reshape(S5_FLAT, S5_WIDTH).astype(MXU_DTYPE)
    wc_im = jnp.einsum("gcp,gh->gphc", c_im.astype(F32), eye).reshape(S5_FLAT, S5_WIDTH).astype(MXU_DTYPE)
    return ab_re.reshape(1, S5_FLAT), ab_im.reshape(1, S5_FLAT), wb_re, wb_im, wc_re, wc_im


def _s5(z3, h0r, h0i, sp, d, wg, bg, *, bt, tc, exact_in):
    b, seq, _ = z3.shape
    ar, ai, wbr, wbi, wcr, wci = sp
    if not exact_in:
        wbr, wbi = wbr.astype(MXU_DTYPE), wbi.astype(MXU_DTYPE)
    kern = functools.partial(_s5_kernel, bt=bt, tc=tc, exact_in=exact_in)
    const = lambda shape: pl.BlockSpec(shape, lambda i, c: (0,) * len(shape))
    return pl.pallas_call(
        kern,
        grid=(b // bt, seq // tc),
        in_specs=[
            pl.BlockSpec((bt, tc, S5_WIDTH), lambda i, c: (i, c, COL_U // S5_WIDTH)),
            pl.BlockSpec((bt, S5_FLAT), lambda i, c: (i, 0)),
            pl.BlockSpec((bt, S5_FLAT), lambda i, c: (i, 0)),
            const((1, S5_FLAT)), const((1, S5_FLAT)),
            const((S5_WIDTH, S5_FLAT)), const((S5_WIDTH, S5_FLAT)),
            const((S5_FLAT, S5_WIDTH)), const((S5_FLAT, S5_WIDTH)),
            const((1, S5_WIDTH)), const((S5_WIDTH, S5_WIDTH)), const((1, S5_WIDTH)),
        ],
        out_specs=[
            pl.BlockSpec((bt, tc, S5_WIDTH), lambda i, c: (i, c, 0)),
            pl.BlockSpec((bt, S5_FLAT), lambda i, c: (i, 0)),
            pl.BlockSpec((bt, S5_FLAT), lambda i, c: (i, 0)),
        ],
        out_shape=[
            jax.ShapeDtypeStruct((b, seq, S5_WIDTH), F32),
            jax.ShapeDtypeStruct((b, S5_FLAT), F32),
            jax.ShapeDtypeStruct((b, S5_FLAT), F32),
        ],
        scratch_shapes=[
            pltpu.VMEM((bt * tc, S5_FLAT), F32), pltpu.VMEM((bt * tc, S5_FLAT), F32),
            pltpu.VMEM((bt, S5_FLAT), F32), pltpu.VMEM((bt, S5_FLAT), F32),
        ],
        compiler_params=pltpu.CompilerParams(
            dimension_semantics=("parallel", "arbitrary"), vmem_limit_bytes=VMEM_LIMIT),
        name="s5",
    )(z3, h0r, h0i, ar, ai, wbr, wbi, wcr, wci, d, wg, bg)


def _sortable_key(score):
    bits = lax.bitcast_convert_type(score + 0.0, jnp.int32)
    return bits ^ ((bits >> 31) & jnp.int32(0x7FFFFFFF))


DSA_TRIP_BLOCKS = 4
DSA_KT = DSA_TRIP_BLOCKS * Q_BLOCK
I16_MIN = -(2 ** 15)


def _fold_rows(x, rows, op=jnp.add):
    parts = [x[i:i + rows] for i in range(0, x.shape[0], rows)]
    while len(parts) > 1:
        parts = [op(a, b) for a, b in zip(parts[0::2], parts[1::2])] + parts[len(parts) & ~1:]
    return parts[0]


def _bisect16(count_ge, k):
    def body(it, res):
        cand = res | lax.shift_left(jnp.int32(1), jnp.int32(15) - it)
        cnt = count_ge((cand + I16_MIN).astype(jnp.int16))
        return jnp.where(cnt >= k, cand, res)

    return lax.fori_loop(0, 16, body, jnp.zeros((1, Q_BLOCK), jnp.int32))


def _dsa_kernel(cq_ref, ckv_ref, qi_ref, kiwq_ref, kiwk_ref, gq_ref, gk_ref, bias_ref,
                o_ref, ck_out, cv_out, cki_out, kn_s, vt_s, ki_s, keys_s, hi_s, lo_s, negm_s, s_s, p_s,
                *, seq, topk):
    j = pl.program_id(1)
    nkt = j // DSA_TRIP_BLOCKS + 1

    @pl.when(j == 0)
    def _prep():
        kn = _rms(ckv_ref[0, :, 0:HEAD_DIM], gk_ref[...])
        cv = ckv_ref[0, :, HEAD_DIM:2 * HEAD_DIM]
        ck_out[0] = kn
        cv_out[0] = cv
        kn_s[...] = kn.astype(MXU_DTYPE)
        eye = (lax.broadcasted_iota(jnp.int32, (HEAD_DIM, HEAD_DIM), 0)
               == lax.broadcasted_iota(jnp.int32, (HEAD_DIM, HEAD_DIM), 1)).astype(MXU_DTYPE)
        vt_s[...] = _nt_dot(eye, cv.astype(MXU_DTYPE)).astype(MXU_DTYPE)
        ki = kiwk_ref[0, :, 0:IDX_DIM]
        cki_out[0] = ki
        ki_s[...] = ki.astype(MXU_DTYPE)

    cq = cq_ref[0]
    q_all = jnp.concatenate(
        [(_rms(cq[:, h * HEAD_DIM:(h + 1) * HEAD_DIM], gq_ref[...]) * HEAD_DIM ** -0.5).astype(MXU_DTYPE)
         for h in range(C_HEADS)], axis=0)
    qi = qi_ref[0].astype(MXU_DTYPE)
    qi_all = jnp.concatenate([qi[:, h * IDX_DIM:(h + 1) * IDX_DIM] for h in range(IDX_HEADS)], axis=0)
    w_t = kiwq_ref[0].T
    w_rows = [w_t[IDX_DIM + h:IDX_DIM + h + 1, :] * INDEX_SCALE for h in range(IDX_HEADS)]
    q_chunk = (j * Q_BLOCK + lax.broadcasted_iota(jnp.int32, (1, Q_BLOCK), 1)) // CHUNK
    kt_iota = lax.broadcasted_iota(jnp.int32, (DSA_KT, 1), 0)

    def trip_rows(kt):
        return pl.multiple_of(kt * DSA_KT, DSA_KT)

    def idx_body(kt, _):
        off = trip_rows(kt)
        dots = _nt_dot(ki_s[pl.ds(off, DSA_KT), :], qi_all)
        score = w_rows[0] * jnp.maximum(dots[:, 0:Q_BLOCK], 0.0)
        for h in range(1, IDX_HEADS):
            score = score + w_rows[h] * jnp.maximum(dots[:, h * Q_BLOCK:(h + 1) * Q_BLOCK], 0.0)
        adm = ((off + kt_iota) // CHUNK) <= q_chunk
        key = jnp.where(adm, _sortable_key(score), jnp.int32(INT_MIN))
        keys_s[pl.ds(off, DSA_KT), :] = key
        hi_s[pl.ds(off, DSA_KT), :] = (key >> 16).astype(jnp.int16)
        lo_s[pl.ds(off, DSA_KT), :] = ((key & 0xFFFF) + I16_MIN).astype(jnp.int16)
        return 0

    lax.fori_loop(0, nkt, idx_body, 0)

    def count16(ref, pred, trips):
        parts = [_fold_rows(jnp.where(pred(ref[r:r + DSA_KT, :]), jnp.int16(1), jnp.int16(0)), 16)
                 for r in range(0, trips * DSA_KT, DSA_KT)]
        acc = _fold_rows(jnp.concatenate(parts, axis=0), 16)
        return jnp.sum(acc.astype(jnp.int32), axis=0, keepdims=True)

    def count(pred):
        def body(kt, acc):
            off = trip_rows(kt)
            hit = jnp.where(pred(keys_s[pl.ds(off, DSA_KT), :], off + kt_iota), 1, 0)
            return acc + jnp.sum(hit.reshape(DSA_KT // 8, 8, Q_BLOCK), axis=0)

        acc = lax.fori_loop(0, nkt, body, jnp.zeros((8, Q_BLOCK), jnp.int32))
        return jnp.sum(acc, axis=0, keepdims=True)

    def kth_largest(trips):
        hi_u = _bisect16(lambda c: count16(hi_s, lambda v: v >= c, trips), topk)
        thr_hi = (hi_u + I16_MIN).astype(jnp.int16)
        need_lo = topk - count16(hi_s, lambda v: v > thr_hi, trips)
        for r in range(0, trips * DSA_KT, DSA_KT):
            lo_s[r:r + DSA_KT, :] = jnp.where(hi_s[r:r + DSA_KT, :] == thr_hi, lo_s[r:r + DSA_KT, :],
                                              jnp.int16(I16_MIN))
        lo_u = _bisect16(lambda c: count16(lo_s, lambda v: v >= c, trips), need_lo)
        n_tied = count16(lo_s, lambda v: v >= (lo_u + I16_MIN).astype(jnp.int16), trips)
        return lax.shift_left(hi_u + I16_MIN, 16) | lo_u, n_tied - need_lo

    thr, surplus = lax.switch(nkt - 1, [functools.partial(kth_largest, n) for n in range(1, seq // DSA_KT + 1)])
    real_thr = thr != jnp.int32(INT_MIN)
    pos_bits = (seq - 1).bit_length()

    def _tie_search():
        need = topk - count(lambda k, pos: k > thr)

        def body(it, v):
            cand = v | lax.shift_left(jnp.int32(1), jnp.int32(pos_bits - 1) - it)
            return jnp.where(count(lambda k, pos: (k == thr) & (pos < cand)) < need, cand, v)

        return lax.fori_loop(0, pos_bits, body, jnp.zeros((1, Q_BLOCK), jnp.int32))

    has_ties = jnp.max(jnp.where(real_thr & (surplus > 0), 1, 0)) > 0
    tie_last = lax.cond(has_ties, _tie_search, lambda: jnp.full((1, Q_BLOCK), 2 ** pos_bits - 1, jnp.int32))
    tie_last = jnp.where(real_thr, tie_last, -1)

    def mask_body(kt, _):
        off = trip_rows(kt)
        keyc = keys_s[pl.ds(off, DSA_KT), :]
        sel = (keyc > thr) | ((keyc == thr) & (off + kt_iota <= tie_last))
        negm_s[pl.ds(off, DSA_KT), :] = jnp.where(sel, 0.0, NEG_INF)
        return 0

    lax.fori_loop(0, nkt, mask_body, 0)

    heads = [slice(h * Q_BLOCK, (h + 1) * Q_BLOCK) for h in range(C_HEADS)]

    def score_body(kt, m8):
        off = trip_rows(kt)
        rows = pl.ds(off, DSA_KT)
        s_all = _nt_dot(kn_s[rows, :], q_all)
        negm = negm_s[rows, :]
        bidx = [jnp.clip(j - (DSA_TRIP_BLOCKS * kt + t), 0, 2) for t in range(DSA_TRIP_BLOCKS)]
        tops = []
        for ls in heads:
            bias = jnp.concatenate([bias_ref[bi, :, ls] for bi in bidx], axis=0)
            s = s_all[:, ls] + bias + negm
            s_s[rows, ls] = s
            tops.append(_fold_rows(s, 8, jnp.maximum))
        return jnp.maximum(m8, jnp.concatenate(tops, axis=1))

    m8 = lax.fori_loop(0, nkt, score_body, jnp.full((8, C_HEADS * Q_BLOCK), NEG_INF, F32))
    m = jnp.max(m8, axis=0, keepdims=True)

    def prob_body(kt, l8):
        rows = pl.ds(trip_rows(kt), DSA_KT)
        p = jnp.exp(s_s[rows, :] - m)
        p_s[rows, :] = p.astype(MXU_DTYPE)
        return l8 + _fold_rows(p, 8)

    l8 = lax.fori_loop(0, nkt, prob_body, jnp.zeros((8, C_HEADS * Q_BLOCK), F32))
    denom = jnp.sum(l8, axis=0, keepdims=True)

    def value_body(kt, acc):
        rows = pl.ds(trip_rows(kt), DSA_KT)
        return acc + _dot(vt_s[:, rows], p_s[rows, :])

    acc = lax.fori_loop(0, nkt, value_body, jnp.zeros((HEAD_DIM, C_HEADS * Q_BLOCK), F32)) / denom
    o_ref[0] = jnp.concatenate([acc[:, ls] for ls in heads], axis=0).T


def _t5_bucket(rel):
    half = T5_BUCKETS // 2
    max_exact = half // 2
    n = jnp.abs(rel)
    log_val = jnp.log(jnp.maximum(n, 1).astype(F32) / max_exact) / math.log(T5_MAX_DIST / max_exact)
    large = jnp.minimum(max_exact + (log_val * (half - max_exact)).astype(jnp.int32), half - 1)
    return jnp.where(rel > 0, half, 0) + jnp.where(n < max_exact, n, large)


def _dsa_bias_tiles(t5_table):
    s = jnp.arange(Q_BLOCK)[:, None]
    t = jnp.arange(Q_BLOCK)[None, :]
    tiles = []
    for d in range(3):
        rel = s - d * Q_BLOCK - t
        tile = t5_table.astype(F32)[_t5_bucket(rel)]
        tiles.append(tile.transpose(0, 2, 1).reshape(Q_BLOCK, C_HEADS * Q_BLOCK))
    return jnp.stack(tiles)


def _dsa(z3, bias, gq, gk):
    b, seq, _ = z3.shape
    assert seq % DSA_KT == 0
    topk = min(TOPK_MAX, seq // 4)
    kern = functools.partial(_dsa_kernel, seq=seq, topk=topk)
    return pl.pallas_call(
        kern,
        grid=(b, seq // Q_BLOCK),
        in_specs=[
            pl.BlockSpec((1, Q_BLOCK, C_WIDTH), lambda i, j: (i, j, COL_CQ // C_WIDTH)),
            pl.BlockSpec((1, seq, 128), lambda i, j: (i, 0, COL_CKV // 128)),
            pl.BlockSpec((1, Q_BLOCK, 256), lambda i, j: (i, j, COL_QI // 256)),
            pl.BlockSpec((1, Q_BLOCK, 128), lambda i, j: (i, j, COL_KIW // 128)),
            pl.BlockSpec((1, seq, 128), lambda i, j: (i, 0, COL_KIW // 128)),
            pl.BlockSpec((1, HEAD_DIM), lambda i, j: (0, 0)),
            pl.BlockSpec((1, HEAD_DIM), lambda i, j: (0, 0)),
            pl.BlockSpec((3, Q_BLOCK, C_HEADS * Q_BLOCK), lambda i, j: (0, 0, 0)),
        ],
        out_specs=[
            pl.BlockSpec((1, Q_BLOCK, C_WIDTH), lambda i, j: (i, j, 0)),
            pl.BlockSpec((1, seq, HEAD_DIM), lambda i, j: (i, 0, 0)),
            pl.BlockSpec((1, seq, HEAD_DIM), lambda i, j: (i, 0, 0)),
            pl.BlockSpec((1, seq, IDX_DIM), lambda i, j: (i, 0, 0)),
        ],
        out_shape=[
            jax.ShapeDtypeStruct((b, seq, C_WIDTH), F32),
            jax.ShapeDtypeStruct((b, seq, HEAD_DIM), F32),
            jax.ShapeDtypeStruct((b, seq, HEAD_DIM), F32),
            jax.ShapeDtypeStruct((b, seq, IDX_DIM), F32),
        ],
        scratch_shapes=[
            pltpu.VMEM((seq, HEAD_DIM), MXU_DTYPE),
            pltpu.VMEM((HEAD_DIM, seq), MXU_DTYPE),
            pltpu.VMEM((seq, IDX_DIM), MXU_DTYPE),
            pltpu.VMEM((seq, Q_BLOCK), jnp.int32),
            pltpu.VMEM((seq, Q_BLOCK), jnp.int16),
            pltpu.VMEM((seq, Q_BLOCK), jnp.int16),
            pltpu.VMEM((seq, Q_BLOCK), F32),
            pltpu.VMEM((seq, C_HEADS * Q_BLOCK), F32),
            pltpu.VMEM((seq, C_HEADS * Q_BLOCK), MXU_DTYPE),
        ],
        compiler_params=pltpu.CompilerParams(
            dimension_semantics=("parallel", "arbitrary"), vmem_limit_bytes=VMEM_LIMIT),
        name="dsa",
    )(z3, z3, z3, z3, z3, gq, gk, bias)


def _mix_residual(x_ref, oa_ref, ob_ref, oc_ref, wo_ref):
    mix = (_dot(oa_ref[...].astype(MXU_DTYPE), wo_ref[0:A_WIDTH, :])
           + _dot(ob_ref[...].astype(MXU_DTYPE), wo_ref[A_WIDTH:A_WIDTH + S5_WIDTH, :])
           + _dot(oc_ref[...].astype(MXU_DTYPE), wo_ref[A_WIDTH + S5_WIDTH:, :]))
    return x_ref[...] + mix


def _ffn_chunk(xn, g, g1, g2, sl, wu_ref, cw_ref, cb_ref, wd_ref):
    up = _dot(xn, wu_ref[:, sl])
    gc = cw_ref[0:1, sl] * g2 + cw_ref[1:2, sl] * g1 + cw_ref[2:3, sl] * g + cb_ref[:, sl]
    return _dot((jax.nn.silu(gc) * up).astype(MXU_DTYPE), wd_ref[sl, :])


def _out_ffn_kernel(x_ref, oa_ref, ob_ref, oc_ref, wo_ref, g2_ref, wg_ref, wu_ref, cw_ref, cb_ref, wd_ref,
                    y_ref, conv_ref, carry_s, *, tiles_per_seq, tf):
    i = pl.program_id(0)
    tm = x_ref.shape[0]
    x1 = _mix_residual(x_ref, oa_ref, ob_ref, oc_ref, wo_ref)
    xn = _rms(x1, g2_ref[...]).astype(MXU_DTYPE)

    @pl.when(i % tiles_per_seq == 0)
    def _seq_start():
        carry_s[...] = jnp.zeros(carry_s.shape, F32)

    row = lax.broadcasted_iota(jnp.int32, (tm, 1), 0)
    acc = x1
    for f in range(FFN_DIM // tf):
        sl = slice(f * tf, (f + 1) * tf)
        g = _dot(xn, wg_ref[:, sl])
        prev2, prev1 = carry_s[0:1, sl], carry_s[1:2, sl]
        g1 = jnp.where(row == 0, prev1, pltpu.roll(g, 1, 0))
        g2 = jnp.where(row == 0, prev2, jnp.where(row == 1, prev1, pltpu.roll(g, 2, 0)))
        acc = acc + _ffn_chunk(xn, g, g1, g2, sl, wu_ref, cw_ref, cb_ref, wd_ref)
        carry_s[0:2, sl] = g[tm - 2:tm, :]
    y_ref[...] = acc
    conv_ref[0] = carry_s[0:2, :]


def _out_ffn(x2d, oa, ob, oc, wo, g2, wg, wu, cw, cb, wd, *, seq, tm, tf):
    m = x2d.shape[0]
    tiles_per_seq = seq // tm
    kern = functools.partial(_out_ffn_kernel, tiles_per_seq=tiles_per_seq, tf=tf)
    row = lambda w: pl.BlockSpec((tm, w), lambda i: (i, 0))
    const = lambda shape: pl.BlockSpec(shape, lambda i: (0,) * len(shape), pipeline_mode=pl.Buffered(1))
    return pl.pallas_call(
        kern,
        grid=(m // tm,),
        in_specs=[
            row(D_MODEL), row(A_WIDTH), row(S5_WIDTH), row(C_WIDTH),
            const((D_MODEL, D_MODEL)), const((1, D_MODEL)),
            const((D_MODEL, FFN_DIM)), const((D_MODEL, FFN_DIM)),
            const((CONV_W, FFN_DIM)), const((1, FFN_DIM)), const((FFN_DIM, D_MODEL)),
        ],
        out_specs=[
            row(D_MODEL),
            pl.BlockSpec((1, CONV_W - 1, FFN_DIM), lambda i: (i // tiles_per_seq, 0, 0)),
        ],
        out_shape=[
            jax.ShapeDtypeStruct((m, D_MODEL), F32),
            jax.ShapeDtypeStruct((m // seq, CONV_W - 1, FFN_DIM), F32),
        ],
        scratch_shapes=[pltpu.VMEM((8, FFN_DIM), F32)],
        compiler_params=pltpu.CompilerParams(dimension_semantics=("arbitrary",), vmem_limit_bytes=VMEM_LIMIT),
        name="out_ffn",
    )(x2d, oa, ob, oc, wo, g2, wg, wu, cw, cb, wd)


def _layer_params(l, t5_bias, norm1_g, w_in, a_q_gain, a_k_gain, a_rel_bias, s5_lam_re, s5_lam_im, s5_log_dt,
                  s5_b_re, s5_b_im, s5_c_re, s5_c_im, s5_d, s5_w_glu, s5_b_glu, c_q_gain, c_k_gain, w_out,
                  norm2_g, ffn_w_gate, ffn_w_up, ffn_conv_w, ffn_conv_b, ffn_w_down):
    w = w_in[l]
    sizes = (A_WIDTH, A_WIDTH, A_WIDTH, S5_WIDTH, C_WIDTH, HEAD_DIM, HEAD_DIM, IDX_HEADS * IDX_DIM, IDX_DIM, IDX_HEADS)
    cuts = [0]
    for s in sizes:
        cuts.append(cuts[-1] + s)
    aq, ak, av, u, cq, ck, cv, qi, ki, wi = [w[:, cuts[n]:cuts[n + 1]] for n in range(len(sizes))]
    pad = jnp.zeros((D_MODEL, Z_WIDTH - COL_KIW - IDX_DIM - IDX_HEADS), w.dtype)
    w_r = jnp.concatenate([aq, ak, av, cq, u, qi, ck, cv, ki, wi, pad], axis=1).astype(MXU_DTYPE)
    return dict(
        norm1=norm1_g[l].reshape(1, D_MODEL), w_in=w_r,
        a_gq=a_q_gain[l].reshape(1, HEAD_DIM), a_gk=a_k_gain[l].reshape(1, HEAD_DIM), a_rel=a_rel_bias[l],
        s5=_s5_params(s5_lam_re[l], s5_lam_im[l], s5_log_dt[l], s5_b_re[l], s5_b_im[l], s5_c_re[l], s5_c_im[l]),
        s5_d=s5_d[l].reshape(1, S5_WIDTH), s5_wg=s5_w_glu[l].astype(MXU_DTYPE), s5_bg=s5_b_glu[l].reshape(1, S5_WIDTH),
        c_gq=c_q_gain[l].reshape(1, HEAD_DIM), c_gk=c_k_gain[l].reshape(1, HEAD_DIM),
        w_out=w_out[l].astype(MXU_DTYPE), norm2=norm2_g[l].reshape(1, D_MODEL),
        wg=ffn_w_gate[l].astype(MXU_DTYPE), wu=ffn_w_up[l].astype(MXU_DTYPE), cw=ffn_conv_w[l],
        cb=ffn_conv_b[l].reshape(1, FFN_DIM), wd=ffn_w_down[l].astype(MXU_DTYPE),
    )


def _prompt_layer(x, p, dsa_bias, *, tm, tf, s5_bt, s5_tc):
    b, seq, _ = x.shape
    x2d = x.reshape(b * seq, D_MODEL)
    z3 = _in_proj(x2d, p["norm1"], p["w_in"], tm).reshape(b, seq, Z_WIDTH)
    oa, a_k, a_v = _attn_a(z3, _attn_a_bias(p["a_rel"]), p["a_gq"], p["a_gk"])
    zero_h = jnp.zeros((b, S5_FLAT), F32)
    ob, h_re, h_im = _s5(z3, zero_h, zero_h, p["s5"], p["s5_d"], p["s5_wg"], p["s5_bg"],
                         bt=s5_bt, tc=s5_tc, exact_in=False)
    oc, c_k, c_v, c_ki = _dsa(z3, dsa_bias, p["c_gq"], p["c_gk"])
    y, conv = _out_ffn(x2d, oa.reshape(b * seq, A_WIDTH), ob.reshape(b * seq, S5_WIDTH),
                       oc.reshape(b * seq, C_WIDTH), p["w_out"], p["norm2"], p["wg"], p["wu"], p["cw"], p["cb"],
                       p["wd"], seq=seq, tm=tm, tf=tf)
    keep = a_k.shape[1]
    states = (a_k.reshape(b, keep, A_HEADS, HEAD_DIM), a_v.reshape(b, keep, A_HEADS, HEAD_DIM),
              h_re.reshape(b, S5_GROUPS, S5_STATE), h_im.reshape(b, S5_GROUPS, S5_STATE), c_k, c_v, c_ki, conv)
    return y.reshape(b, seq, D_MODEL), states


def _attn_a_step_kernel(q_ref, k_ref, v_ref, ck_ref, cv_ref, bc_ref, bn_ref, gq_ref, gk_ref, o_ref, ak_ref, av_ref):
    v = v_ref[0]
    av_ref[0] = v
    for h in range(A_HEADS):
        sl = slice(h * HEAD_DIM, (h + 1) * HEAD_DIM)
        qh = (_rms(q_ref[0, :, sl], gq_ref[...]) * HEAD_DIM ** -0.5).astype(MXU_DTYPE)
        khn = _rms(k_ref[0, :, sl], gk_ref[...])
        ak_ref[0, :, sl] = khn
        s_c = _nt_dot(qh, ck_ref[0, :, sl].astype(MXU_DTYPE)) + bc_ref[h]
        s_n = _nt_dot(qh, khn.astype(MXU_DTYPE)) + bn_ref[h]
        m = jnp.maximum(jnp.max(s_c, axis=-1, keepdims=True), jnp.max(s_n, axis=-1, keepdims=True))
        p_c = jnp.exp(s_c - m)
        p_n = jnp.exp(s_n - m)
        denom = jnp.sum(p_c, axis=-1, keepdims=True) + jnp.sum(p_n, axis=-1, keepdims=True)
        oh = (_dot(p_c.astype(MXU_DTYPE), cv_ref[0, :, sl].astype(MXU_DTYPE))
              + _dot(p_n.astype(MXU_DTYPE), v[:, sl].astype(MXU_DTYPE)))
        o_ref[0, :, sl] = oh / denom


def _attn_a_step_bias(rel_table, past, t_new):
    t = jnp.arange(t_new)[:, None]
    rel = jnp.concatenate([jnp.arange(past) - past, jnp.arange(t_new)])[None, :] - t
    bias = rel_table.astype(F32)[:, jnp.clip(rel, -A_MAX_REL, A_MAX_REL) + A_MAX_REL]
    return bias[:, :, :past], bias[:, :, past:]


def _attn_a_step(z3, cache_k, cache_v, layer, bias_c, bias_n, gq, gk):
    b, t_new, _ = z3.shape
    past = cache_k.shape[2]
    new = lambda col: pl.BlockSpec((1, t_new, A_WIDTH), lambda i: (i, 0, col // A_WIDTH))
    const = lambda shape: pl.BlockSpec(shape, lambda i: (0,) * len(shape))
    return pl.pallas_call(
        _attn_a_step_kernel,
        grid=(b,),
        in_specs=[
            new(COL_AQ), new(COL_AK), new(COL_AV),
            pl.BlockSpec((None, 1, past, A_WIDTH), lambda i: (layer, i, 0, 0)),
            pl.BlockSpec((None, 1, past, A_WIDTH), lambda i: (layer, i, 0, 0)),
            const((A_HEADS, t_new, past)), const((A_HEADS, t_new, t_new)),
            const((1, HEAD_DIM)), const((1, HEAD_DIM)),
        ],
        out_specs=[pl.BlockSpec((1, t_new, A_WIDTH), lambda i: (i, 0, 0))] * 3,
        out_shape=[jax.ShapeDtypeStruct((b, t_new, A_WIDTH), F32)] * 3,
        compiler_params=pltpu.CompilerParams(dimension_semantics=("parallel",), vmem_limit_bytes=VMEM_LIMIT),
        name="attn_a_step",
    )(z3, z3, z3, cache_k, cache_v, bias_c, bias_n, gq, gk)


def _dsa_step_kernel(cq_ref, ckv_ref, qi_ref, kiw_ref, ck_ref, cv_ref, cki_ref, gq_ref, gk_ref, bias_ref,
                     o_ref, ck_out, cv_out, cki_out, *, past, t_new, topk):
    n_keys = past + t_new
    kn_new = _rms(ckv_ref[0, :, 0:HEAD_DIM], gk_ref[...])
    cv_new = ckv_ref[0, :, HEAD_DIM:2 * HEAD_DIM]
    ki_new = kiw_ref[0, :, 0:IDX_DIM]
    ck_out[0] = kn_new
    cv_out[0] = cv_new
    cki_out[0] = ki_new
    k_all = jnp.concatenate([ck_ref[0], kn_new], axis=0).astype(MXU_DTYPE)
    v_all = jnp.concatenate([cv_ref[0], cv_new], axis=0).astype(MXU_DTYPE)
    ki_all = jnp.concatenate([cki_ref[0], ki_new], axis=0).astype(MXU_DTYPE)
    v_t = _nt_dot(_eye(HEAD_DIM, MXU_DTYPE), v_all).astype(MXU_DTYPE)

    def spread(groups, dtype):
        r = lax.broadcasted_iota(jnp.int32, (t_new, groups * t_new), 0)
        c = lax.broadcasted_iota(jnp.int32, (t_new, groups * t_new), 1)
        return (c % t_new == r).astype(dtype)

    qi = qi_ref[0].astype(MXU_DTYPE)
    qi_all = jnp.concatenate([qi[:, h * IDX_DIM:(h + 1) * IDX_DIM] for h in range(IDX_HEADS)], axis=0)
    kiw_pad = jnp.concatenate([kiw_ref[0], jnp.zeros((128 - t_new, 128), F32)], axis=0)
    w_t = kiw_pad.T[:, 0:t_new]
    w_flat = jnp.concatenate([w_t[IDX_DIM + h:IDX_DIM + h + 1, :] for h in range(IDX_HEADS)], axis=1) * INDEX_SCALE
    weighted = w_flat * jnp.maximum(_nt_dot(ki_all, qi_all), 0.0)
    score = lax.dot_general(weighted, spread(IDX_HEADS, F32), (((1,), (1,)), ((), ())),
                            preferred_element_type=F32, precision=lax.Precision.HIGHEST)
    pos = lax.broadcasted_iota(jnp.int32, (n_keys, 1), 0)
    q_chunk = (past + lax.broadcasted_iota(jnp.int32, (1, t_new), 1)) // CHUNK
    keys = jnp.where((pos // CHUNK) <= q_chunk, _sortable_key(score), jnp.int32(INT_MIN))

    def count(pred):
        return jnp.sum(jnp.where(pred(keys, pos), 1, 0), axis=0, keepdims=True)

    def kth_body(it, res):
        cand = res | lax.shift_left(jnp.int32(1), jnp.int32(31) - it)
        thr_c = cand ^ jnp.int32(INT_MIN)
        return jnp.where(count(lambda k, p: k >= thr_c) >= topk, cand, res)

    thr = lax.fori_loop(0, 32, kth_body, jnp.zeros((1, t_new), jnp.int32)) ^ jnp.int32(INT_MIN)
    real_thr = thr != jnp.int32(INT_MIN)
    pos_bits = (n_keys - 1).bit_length()

    def _tie_search():
        need = topk - count(lambda k, p: k > thr)

        def tie_body(it, v):
            cand = v | lax.shift_left(jnp.int32(1), jnp.int32(pos_bits - 1) - it)
            return jnp.where(count(lambda k, p: (k == thr) & (p < cand)) < need, cand, v)

        return lax.fori_loop(0, pos_bits, tie_body, jnp.zeros((1, t_new), jnp.int32))

    has_ties = jnp.max(jnp.where(real_thr & (count(lambda k, p: k >= thr) > topk), 1, 0)) > 0
    tie_last = lax.cond(has_ties, _tie_search, lambda: jnp.full((1, t_new), 2 ** pos_bits - 1, jnp.int32))
    tie_last = jnp.where(real_thr, tie_last, -1)
    dropped = jnp.where((keys > thr) | ((keys == thr) & (pos <= tie_last)), 0.0, 1.0).astype(MXU_DTYPE)
    negm = _dot(dropped, spread(C_HEADS, MXU_DTYPE)) * NEG_INF

    cq = cq_ref[0]
    q_all = jnp.concatenate(
        [(_rms(cq[:, h * HEAD_DIM:(h + 1) * HEAD_DIM], gq_ref[...]) * HEAD_DIM ** -0.5).astype(MXU_DTYPE)
         for h in range(C_HEADS)], axis=0)
    s = _nt_dot(k_all, q_all) + bias_ref[...] + negm
    p = jnp.exp(s - jnp.max(s, axis=0, keepdims=True))
    out_t = _dot(v_t, p.astype(MXU_DTYPE)) / jnp.sum(p, axis=0, keepdims=True)
    hq = C_HEADS * t_new
    out_sq = jnp.concatenate([jnp.concatenate([out_t, jnp.zeros((HEAD_DIM, 128 - hq), F32)], axis=1),
                              jnp.zeros((128 - HEAD_DIM, 128), F32)], axis=0).T
    o_ref[0] = jnp.concatenate([out_sq[h * t_new:(h + 1) * t_new, 0:HEAD_DIM] for h in range(C_HEADS)], axis=1)


def _dsa_step_bias(t5_table, past, t_new):
    s = jnp.arange(past + t_new)[:, None]
    t = jnp.arange(t_new)[None, :]
    return t5_table.astype(F32)[_t5_bucket(s - (past + t))].transpose(0, 2, 1).reshape(past + t_new, C_HEADS * t_new)


def _dsa_step(z3, cache_k, cache_v, cache_ki, layer, bias, gq, gk):
    b, t_new, _ = z3.shape
    past = cache_k.shape[2]
    topk = min(TOPK_MAX, (past + t_new) // 4)
    kern = functools.partial(_dsa_step_kernel, past=past, t_new=t_new, topk=topk)
    new = lambda w, col: pl.BlockSpec((1, t_new, w), lambda i: (i, 0, col // w))
    per_seq = lambda n, w: pl.BlockSpec((1, n, w), lambda i: (i, 0, 0))
    cached = lambda w: pl.BlockSpec((None, 1, past, w), lambda i: (layer, i, 0, 0))
    const = lambda shape: pl.BlockSpec(shape, lambda i: (0,) * len(shape))
    return pl.pallas_call(
        kern,
        grid=(b,),
        in_specs=[
            new(C_WIDTH, COL_CQ), new(128, COL_CKV), new(256, COL_QI), new(128, COL_KIW),
            cached(HEAD_DIM), cached(HEAD_DIM), cached(IDX_DIM),
            const((1, HEAD_DIM)), const((1, HEAD_DIM)), const((past + t_new, C_HEADS * t_new)),
        ],
        out_specs=[per_seq(t_new, C_WIDTH), per_seq(t_new, HEAD_DIM), per_seq(t_new, HEAD_DIM),
                   per_seq(t_new, IDX_DIM)],
        out_shape=[
            jax.ShapeDtypeStruct((b, t_new, C_WIDTH), F32),
            jax.ShapeDtypeStruct((b, t_new, HEAD_DIM), F32),
            jax.ShapeDtypeStruct((b, t_new, HEAD_DIM), F32),
            jax.ShapeDtypeStruct((b, t_new, IDX_DIM), F32),
        ],
        compiler_params=pltpu.CompilerParams(dimension_semantics=("parallel",), vmem_limit_bytes=VMEM_LIMIT),
        name="dsa_step",
    )(z3, z3, z3, z3, cache_k, cache_v, cache_ki, gq, gk, bias)


def _out_ffn_step_kernel(x_ref, oa_ref, ob_ref, oc_ref, wo_ref, g2_ref, wg_ref, wu_ref, cw_ref, cb_ref, wd_ref,
                         e1_ref, e2_ref, y_ref, gate_ref, *, t_new, tf):
    tm = x_ref.shape[0]
    x1 = _mix_residual(x_ref, oa_ref, ob_ref, oc_ref, wo_ref)
    xn = _rms(x1, g2_ref[...]).astype(MXU_DTYPE)
    t = lax.broadcasted_iota(jnp.int32, (tm, 1), 0) % t_new
    acc = x1
    for f in range(FFN_DIM // tf):
        sl = slice(f * tf, (f + 1) * tf)
        g = _dot(xn, wg_ref[:, sl])
        gate_ref[:, sl] = g
        g1 = jnp.where(t == 0, e1_ref[:, sl], pltpu.roll(g, 1, 0))
        g2 = jnp.where(t <= 1, e2_ref[:, sl], pltpu.roll(g, 2, 0))
        acc = acc + _ffn_chunk(xn, g, g1, g2, sl, wu_ref, cw_ref, cb_ref, wd_ref)
    y_ref[...] = acc


def _out_ffn_step(x2d, oa, ob, oc, wo, g2, wg, wu, cw, cb, wd, conv_prev, *, t_new, tf):
    m = x2d.shape[0]
    b = m // t_new
    e1 = jnp.zeros((b, t_new, FFN_DIM), F32).at[:, 0].set(conv_prev[:, 1]).reshape(m, FFN_DIM)
    e2 = jnp.zeros((b, t_new, FFN_DIM), F32).at[:, 0].set(conv_prev[:, 0]).at[:, 1].set(conv_prev[:, 1])
    e2 = e2.reshape(m, FFN_DIM)
    kern = functools.partial(_out_ffn_step_kernel, t_new=t_new, tf=tf)
    full = lambda shape: pl.BlockSpec(shape, lambda i: (0,) * len(shape))
    y, gate = pl.pallas_call(
        kern,
        grid=(1,),
        in_specs=[
            full((m, D_MODEL)), full((m, A_WIDTH)), full((m, S5_WIDTH)), full((m, C_WIDTH)),
            full((D_MODEL, D_MODEL)), full((1, D_MODEL)),
            full((D_MODEL, FFN_DIM)), full((D_MODEL, FFN_DIM)),
            full((CONV_W, FFN_DIM)), full((1, FFN_DIM)), full((FFN_DIM, D_MODEL)),
            full((m, FFN_DIM)), full((m, FFN_DIM)),
        ],
        out_specs=[full((m, D_MODEL)), full((m, FFN_DIM))],
        out_shape=[jax.ShapeDtypeStruct((m, D_MODEL), F32), jax.ShapeDtypeStruct((m, FFN_DIM), F32)],
        compiler_params=pltpu.CompilerParams(dimension_semantics=("arbitrary",), vmem_limit_bytes=VMEM_LIMIT),
        name="out_ffn_step",
    )(x2d, oa, ob, oc, wo, g2, wg, wu, cw, cb, wd, e1, e2)
    return y, gate.reshape(b, t_new, FFN_DIM)[:, t_new - (CONV_W - 1):]


def _sample_layer(x, p, dsa_bias, layer, ca_k, ca_v, h_re0, h_im0, cc_k, cc_v, cc_ki, conv_prev, *, tf):
    b, t_new, _ = x.shape
    m = b * t_new
    x2d = x.reshape(m, D_MODEL)
    z3 = _in_proj(x2d, p["norm1"], p["w_in"], m).reshape(b, t_new, Z_WIDTH)
    depth, _, a_past = ca_k.shape[:3]
    bias_c, bias_n = _attn_a_step_bias(p["a_rel"], a_past, t_new)
    oa, a_k, a_v = _attn_a_step(z3, ca_k.reshape(depth, b, a_past, A_WIDTH), ca_v.reshape(depth, b, a_past, A_WIDTH),
                                layer, bias_c, bias_n, p["a_gq"], p["a_gk"])
    ob, h_re, h_im = _s5(z3, h_re0.reshape(b, S5_FLAT), h_im0.reshape(b, S5_FLAT), p["s5"], p["s5_d"], p["s5_wg"],
                         p["s5_bg"], bt=8, tc=t_new, exact_in=True)
    oc, c_k, c_v, c_ki = _dsa_step(z3, cc_k, cc_v, cc_ki, layer, dsa_bias, p["c_gq"], p["c_gk"])
    y, conv = _out_ffn_step(x2d, oa.reshape(m, A_WIDTH), ob.reshape(m, S5_WIDTH), oc.reshape(m, C_WIDTH),
                            p["w_out"], p["norm2"], p["wg"], p["wu"], p["cw"], p["cb"], p["wd"], conv_prev,
                            t_new=t_new, tf=tf)
    states = (a_k.reshape(b, t_new, A_HEADS, HEAD_DIM), a_v.reshape(b, t_new, A_HEADS, HEAD_DIM),
              h_re.reshape(b, S5_GROUPS, S5_STATE), h_im.reshape(b, S5_GROUPS, S5_STATE), c_k, c_v, c_ki, conv)
    return y.reshape(b, t_new, D_MODEL), states


def kernel(x_prompt, x_sample, cache_a_k, cache_a_v, state_s5_re, state_s5_im, cache_c_k, cache_c_v, cache_c_idx_k,
           state_ffn_conv, t5_bias, norm1_g, w_in, a_q_gain, a_k_gain, a_rel_bias, s5_lam_re, s5_lam_im, s5_log_dt,
           s5_b_re, s5_b_im, s5_c_re, s5_c_im, s5_d, s5_w_glu, s5_b_glu, c_q_gain, c_k_gain, w_out, norm2_g,
           ffn_w_gate, ffn_w_up, ffn_conv_w, ffn_conv_b, ffn_w_down):
    depth = w_in.shape[0]
    dsa_bias = _dsa_bias_tiles(t5_bias)
    dsa_step_bias = _dsa_step_bias(t5_bias, cache_c_k.shape[2], x_sample.shape[1])
    yp, ys = x_prompt, x_sample
    prompt_states, sample_states = [], []
    for l in range(depth):
        p = _layer_params(l, t5_bias, norm1_g, w_in, a_q_gain, a_k_gain, a_rel_bias, s5_lam_re, s5_lam_im, s5_log_dt,
                          s5_b_re, s5_b_im, s5_c_re, s5_c_im, s5_d, s5_w_glu, s5_b_glu, c_q_gain, c_k_gain, w_out,
                          norm2_g, ffn_w_gate, ffn_w_up, ffn_conv_w, ffn_conv_b, ffn_w_down)
        yp, st_p = _prompt_layer(yp, p, dsa_bias, tm=512, tf=FFN_DIM, s5_bt=8, s5_tc=256)
        ys, st_s = _sample_layer(ys, p, dsa_step_bias, l, cache_a_k, cache_a_v, state_s5_re[l], state_s5_im[l],
                                 cache_c_k, cache_c_v, cache_c_idx_k, state_ffn_conv[l], tf=FFN_DIM)
        prompt_states.append(st_p)
        sample_states.append(st_s)
    (a_k_p, a_v_p, s5_re_p, s5_im_p, c_k_p, c_v_p, c_ki_p, conv_p) = [jnp.stack(z) for z in zip(*prompt_states)]
    (a_k_s, a_v_s, s5_re_s, s5_im_s, c_k_s, c_v_s, c_ki_s, conv_s) = [jnp.stack(z) for z in zip(*sample_states)]
    return (yp, ys, a_k_p, a_v_p, a_k_s, a_v_s, s5_re_p, s5_im_p, s5_re_s, s5_im_s,
            c_k_p, c_v_p, c_ki_p, c_k_s, c_v_s, c_ki_s, conv_p, conv_s)
```

```python
import functools
import math

import jax
import jax.numpy as jnp
from jax import lax
from jax.experimental import pallas as pl
from jax.experimental.pallas import tpu as pltpu

F32 = jnp.float32
MXU_DTYPE = jnp.bfloat16

D_MODEL = 1024
CHUNK = 64
HEAD_DIM = 64
A_HEADS = 6
A_WIDTH = A_HEADS * HEAD_DIM
A_BAND_PAST = 8 * CHUNK
A_MAX_REL = 128
S5_GROUPS = 16
S5_GROUP_CH = 16
S5_WIDTH = S5_GROUPS * S5_GROUP_CH
S5_STATE = 64
S5_FLAT = S5_GROUPS * S5_STATE
C_HEADS = 6
C_WIDTH = C_HEADS * HEAD_DIM
IDX_HEADS = 8
IDX_DIM = 32
INDEX_SCALE = (IDX_HEADS * IDX_DIM) ** -0.5
TOPK_MAX = 256
Q_BLOCK = 128
T5_BUCKETS = 32
T5_MAX_DIST = 128
FFN_DIM = 2816
CONV_W = 3
EPS = 1e-6
NEG_INF = -1e30
INT_MIN = -(2 ** 31)

COL_AQ, COL_AK, COL_AV, COL_CQ = 0, 384, 768, 1152
COL_U, COL_QI, COL_CKV, COL_KIW = 1536, 1792, 2048, 2176
Z_WIDTH = 2304
VMEM_LIMIT = 56 * 1024 * 1024


def _nt_dot(a, b):
    return lax.dot_general(a, b, (((1,), (1,)), ((), ())), preferred_element_type=F32)


def _dot(a, b):
    return jnp.dot(a, b, preferred_element_type=F32)


def _rows_at(table, idx):
    onehot = (idx[..., None] == jnp.arange(table.shape[0])).astype(F32)
    return jnp.einsum("...n,nw->...w", onehot, table.astype(F32), precision=lax.Precision.HIGHEST)


def _rms(x, g):
    return x * lax.rsqrt(jnp.mean(x * x, axis=-1, keepdims=True) + EPS) * g


def _in_proj_kernel(x_ref, g_ref, w_ref, z_ref):
    xn = _rms(x_ref[...], g_ref[...]).astype(MXU_DTYPE)
    z_ref[...] = _dot(xn, w_ref[...])


def _in_proj(x2d, g, w_r, tm):
    m = x2d.shape[0]
    return pl.pallas_call(
        _in_proj_kernel,
        grid=(m // tm,),
        in_specs=[
            pl.BlockSpec((tm, D_MODEL), lambda i: (i, 0)),
            pl.BlockSpec((1, D_MODEL), lambda i: (0, 0)),
            pl.BlockSpec((D_MODEL, Z_WIDTH), lambda i: (0, 0)),
        ],
        out_specs=pl.BlockSpec((tm, Z_WIDTH), lambda i: (i, 0)),
        out_shape=jax.ShapeDtypeStruct((m, Z_WIDTH), F32),
        compiler_params=pltpu.CompilerParams(dimension_semantics=("parallel",), vmem_limit_bytes=VMEM_LIMIT),
        name="in_proj",
    )(x2d, g, w_r)


A_QB = 2 * CHUNK
A_BAND = A_BAND_PAST + A_QB


def _eye(n, dtype):
    return (lax.broadcasted_iota(jnp.int32, (n, n), 0) == lax.broadcasted_iota(jnp.int32, (n, n), 1)).astype(dtype)


def _attn_a_kernel(q_ref, k_ref, v_ref, bias_ref, gq_ref, gk_ref, o_ref, ak_ref, av_ref, kn_s, vt_s, s_s, p_s,
                   *, seq, keep):
    j = pl.program_id(1)

    @pl.when(j == 0)
    def _prep():
        for h in range(A_HEADS):
            sl = slice(h * HEAD_DIM, (h + 1) * HEAD_DIM)
            khn = _rms(k_ref[0, :, sl], gk_ref[...])
            kn_s[h, 0:A_BAND_PAST, :] = jnp.zeros((A_BAND_PAST, HEAD_DIM), MXU_DTYPE)
            kn_s[h, A_BAND_PAST:A_BAND_PAST + seq, :] = khn.astype(MXU_DTYPE)
            ak_ref[0, :, sl] = khn[seq - keep:, :]
        v = v_ref[0]
        av_ref[0] = v[seq - keep:, :]
        vt_s[:, 0:A_BAND_PAST] = jnp.zeros((A_WIDTH, A_BAND_PAST), MXU_DTYPE)
        vt_s[:, A_BAND_PAST:A_BAND_PAST + seq] = _nt_dot(_eye(A_WIDTH, MXU_DTYPE), v.astype(MXU_DTYPE)).astype(MXU_DTYPE)

    start = pl.multiple_of(j * A_QB, A_QB)
    variant = jnp.minimum(j, A_LEAD_STEPS)
    for h in range(A_HEADS):
        sl = slice(h * HEAD_DIM, (h + 1) * HEAD_DIM)
        qh = (_rms(q_ref[0, :, sl], gq_ref[...]) * HEAD_DIM ** -0.5).astype(MXU_DTYPE)
        s_s[h] = _nt_dot(kn_s[h, pl.ds(start, A_BAND), :], qh) + bias_ref[variant, h]
    denoms = []
    for h in range(A_HEADS):
        s = s_s[h]
        p = jnp.exp(s - jnp.max(s, axis=0, keepdims=True))
        denoms.append(jnp.sum(p, axis=0, keepdims=True))
        p_s[h] = p.astype(MXU_DTYPE)
    outs = [_dot(vt_s[h * HEAD_DIM:(h + 1) * HEAD_DIM, pl.ds(start, A_BAND)], p_s[h]) / denoms[h]
            for h in range(A_HEADS)]
    o_ref[0] = jnp.concatenate(outs, axis=0).T


A_LEAD_STEPS = A_BAND_PAST // A_QB


def _attn_a_bias(rel_table):
    n_ext = A_BAND + A_QB - 1
    m = jnp.arange(n_ext)
    ext = _rows_at(rel_table.T, jnp.clip(m - (A_QB - 1) - A_BAND_PAST, -A_MAX_REL, A_MAX_REL) + A_MAX_REL).T
    rows = jnp.pad(jnp.broadcast_to(ext[:, None, :], (A_HEADS, A_QB, n_ext)), ((0, 0), (0, 0), (0, 1)))
    skew = rows.reshape(A_HEADS, A_QB * (n_ext + 1))[:, :A_QB * n_ext].reshape(A_HEADS, A_QB, n_ext)
    bias = skew[:, :, A_QB - 1:].transpose(0, 2, 1)
    c = jnp.arange(A_BAND)[:, None]
    lo = (jnp.arange(A_QB)[None, :] // CHUNK) * CHUNK
    in_band = (c >= lo) & (c < lo + A_BAND_PAST + CHUNK)
    first_live = (A_LEAD_STEPS - jnp.arange(A_LEAD_STEPS + 1)) * A_QB
    live = in_band[None] & (c[None] >= first_live[:, None, None])
    return jnp.where(live[:, None], bias[None], NEG_INF)


def _attn_a(z3, bias, gq, gk):
    b, seq, _ = z3.shape
    keep = min(A_BAND_PAST, seq)
    kern = functools.partial(_attn_a_kernel, seq=seq, keep=keep)
    return pl.pallas_call(
        kern,
        grid=(b, seq // A_QB),
        in_specs=[
            pl.BlockSpec((1, A_QB, A_WIDTH), lambda i, j: (i, j, COL_AQ // A_WIDTH)),
            pl.BlockSpec((1, seq, A_WIDTH), lambda i, j: (i, 0, COL_AK // A_WIDTH)),
            pl.BlockSpec((1, seq, A_WIDTH), lambda i, j: (i, 0, COL_AV // A_WIDTH)),
            pl.BlockSpec((A_LEAD_STEPS + 1, A_HEADS, A_BAND, A_QB), lambda i, j: (0, 0, 0, 0),
                         pipeline_mode=pl.Buffered(1)),
            pl.BlockSpec((1, HEAD_DIM), lambda i, j: (0, 0)),
            pl.BlockSpec((1, HEAD_DIM), lambda i, j: (0, 0)),
        ],
        out_specs=[
            pl.BlockSpec((1, A_QB, A_WIDTH), lambda i, j: (i, j, 0)),
            pl.BlockSpec((1, keep, A_WIDTH), lambda i, j: (i, 0, 0)),
            pl.BlockSpec((1, keep, A_WIDTH), lambda i, j: (i, 0, 0)),
        ],
        out_shape=[
            jax.ShapeDtypeStruct((b, seq, A_WIDTH), F32),
            jax.ShapeDtypeStruct((b, keep, A_WIDTH), F32),
            jax.ShapeDtypeStruct((b, keep, A_WIDTH), F32),
        ],
        scratch_shapes=[
            pltpu.VMEM((A_HEADS, A_BAND_PAST + seq, HEAD_DIM), MXU_DTYPE),
            pltpu.VMEM((A_WIDTH, A_BAND_PAST + seq), MXU_DTYPE),
            pltpu.VMEM((A_HEADS, A_BAND, A_QB), F32),
            pltpu.VMEM((A_HEADS, A_BAND, A_QB), MXU_DTYPE),
        ],
        compiler_params=pltpu.CompilerParams(
            dimension_semantics=("parallel", "arbitrary"), vmem_limit_bytes=VMEM_LIMIT),
        name="attn_a",
    )(z3, z3, z3, bias, gq, gk)


def _s5_kernel(u_ref, h0r_ref, h0i_ref, ar_ref, ai_ref, wbr_ref, wbi_ref, wcr_ref, wci_ref, d_ref, wg_ref, bg_ref,
               o_ref, hr_out, hi_out, xr_s, xi_s, hr_c, hi_c, *, bt, tc, exact_in):
    c = pl.program_id(1)

    @pl.when(c == 0)
    def _init():
        hr_c[...] = h0r_ref[...]
        hi_c[...] = h0i_ref[...]

    u = jnp.swapaxes(u_ref[...], 0, 1).reshape(tc * bt, S5_WIDTH)
    if exact_in:
        xr_s[...] = jnp.dot(u, wbr_ref[...], preferred_element_type=F32, precision=lax.Precision.HIGHEST)
        xi_s[...] = jnp.dot(u, wbi_ref[...], preferred_element_type=F32, precision=lax.Precision.HIGHEST)
    else:
        ub = u.astype(MXU_DTYPE)
        xr_s[...] = _dot(ub, wbr_ref[...])
        xi_s[...] = _dot(ub, wbi_ref[...])
    ar = ar_ref[...]
    ai = ai_ref[...]

    def step(t, carry):
        hr, hi = carry
        rows = pl.ds(pl.multiple_of(t * bt, bt), bt)
        nhr = ar * hr - ai * hi + xr_s[rows, :]
        nhi = ar * hi + ai * hr + xi_s[rows, :]
        xr_s[rows, :] = nhr
        xi_s[rows, :] = nhi
        return nhr, nhi

    hr, hi = lax.fori_loop(0, tc, step, (hr_c[...], hi_c[...]))
    hr_c[...] = hr
    hi_c[...] = hi
    hr_out[...] = hr
    hi_out[...] = hi
    y = (_dot(xr_s[...].astype(MXU_DTYPE), wcr_ref[...]) - _dot(xi_s[...].astype(MXU_DTYPE), wci_ref[...])
         + d_ref[...] * u)
    g = jax.nn.gelu(y)
    out = g * jax.nn.sigmoid(_dot(g.astype(MXU_DTYPE), wg_ref[...]) + bg_ref[...])
    o_ref[...] = jnp.swapaxes(out.reshape(tc, bt, S5_WIDTH), 0, 1)


def _s5_params(lam_re, lam_im, log_dt, b_re, b_im, c_re, c_im):
    dt = jnp.exp(log_dt.astype(F32))[:, None]
    lr, li = lam_re.astype(F32), lam_im.astype(F32)
    mag = jnp.exp(lr * dt)
    ab_re, ab_im = mag * jnp.cos(li * dt), mag * jnp.sin(li * dt)
    den = lr * lr + li * li
    cr = ((ab_re - 1.0) * lr + ab_im * li) / den
    ci = (ab_im * lr - (ab_re - 1.0) * li) / den
    br, bi = b_re.astype(F32), b_im.astype(F32)
    bb_re = cr[..., None] * br - ci[..., None] * bi
    bb_im = cr[..., None] * bi + ci[..., None] * br
    eye = jnp.eye(S5_GROUPS, dtype=F32)
    wb_re = jnp.einsum("gpc,gh->gchp", bb_re, eye).reshape(S5_WIDTH, S5_FLAT)
    wb_im = jnp.einsum("gpc,gh->gchp", bb_im, eye).reshape(S5_WIDTH, S5_FLAT)
    wc_re = jnp.einsum("gcp,gh->gphc", c_re.astype(F32), eye).reshape(S5_FLAT, S5_WIDTH).astype(MXU_DTYPE)
    wc_im = jnp.einsum("gcp,gh->gphc", c_im.astype(F32), eye).reshape(S5_FLAT, S5_WIDTH).astype(MXU_DTYPE)
    return ab_re.reshape(1, S5_FLAT), ab_im.reshape(1, S5_FLAT), wb_re, wb_im, wc_re, wc_im


def _s5(z3, h0r, h0i, sp, d, wg, bg, *, bt, tc, exact_in):
    b, seq, _ = z3.shape
    ar, ai, wbr, wbi, wcr, wci = sp
    if not exact_in:
        wbr, wbi = wbr.astype(MXU_DTYPE), wbi.astype(MXU_DTYPE)
    kern = functools.partial(_s5_kernel, bt=bt, tc=tc, exact_in=exact_in)
    const = lambda shape: pl.BlockSpec(shape, lambda i, c: (0,) * len(shape))
    return pl.pallas_call(
        kern,
        grid=(b // bt, seq // tc),
        in_specs=[
            pl.BlockSpec((bt, tc, S5_WIDTH), lambda i, c: (i, c, COL_U // S5_WIDTH)),
            pl.BlockSpec((bt, S5_FLAT), lambda i, c: (i, 0)),
            pl.BlockSpec((bt, S5_FLAT), lambda i, c: (i, 0)),
            const((1, S5_FLAT)), const((1, S5_FLAT)),
            const((S5_WIDTH, S5_FLAT)), const((S5_WIDTH, S5_FLAT)),
            const((S5_FLAT, S5_WIDTH)), const((S5_FLAT, S5_WIDTH)),
            const((1, S5_WIDTH)), const((S5_WIDTH, S5_WIDTH)), const((1, S5_WIDTH)),
        ],
        out_specs=[
            pl.BlockSpec((bt, tc, S5_WIDTH), lambda i, c: (i, c, 0)),
            pl.BlockSpec((bt, S5_FLAT), lambda i, c: (i, 0)),
            pl.BlockSpec((bt, S5_FLAT), lambda i, c: (i, 0)),
        ],
        out_shape=[
            jax.ShapeDtypeStruct((b, seq, S5_WIDTH), F32),
            jax.ShapeDtypeStruct((b, S5_FLAT), F32),
            jax.ShapeDtypeStruct((b, S5_FLAT), F32),
        ],
        scratch_shapes=[
            pltpu.VMEM((bt * tc, S5_FLAT), F32), pltpu.VMEM((bt * tc, S5_FLAT), F32),
            pltpu.VMEM((bt, S5_FLAT), F32), pltpu.VMEM((bt, S5_FLAT), F32),
        ],
        compiler_params=pltpu.CompilerParams(
            dimension_semantics=("parallel", "arbitrary"), vmem_limit_bytes=VMEM_LIMIT),
        name="s5",
    )(z3, h0r, h0i, ar, ai, wbr, wbi, wcr, wci, d, wg, bg)


def _sortable_key(score):
    bits = lax.bitcast_convert_type(score + 0.0, jnp.int32)
    return bits ^ ((bits >> 31) & jnp.int32(0x7FFFFFFF))


DSA_TRIP_BLOCKS = 4
DSA_KT = DSA_TRIP_BLOCKS * Q_BLOCK
I16_MIN = -(2 ** 15)


def _fold_rows(x, rows, op=jnp.add):
    parts = [x[i:i + rows] for i in range(0, x.shape[0], rows)]
    while len(parts) > 1:
        parts = [op(a, b) for a, b in zip(parts[0::2], parts[1::2])] + parts[len(parts) & ~1:]
    return parts[0]


def _bisect16(count_ge, k):
    def body(it, res):
        cand = res | lax.shift_left(jnp.int32(1), jnp.int32(15) - it)
        cnt = count_ge((cand + I16_MIN).astype(jnp.int16))
        return jnp.where(cnt >= k, cand, res)

    return lax.fori_loop(0, 16, body, jnp.zeros((1, Q_BLOCK), jnp.int32))


def _dsa_kernel(cq_ref, ckv_ref, qi_ref, kiwq_ref, kiwk_ref, gq_ref, gk_ref, bias_ref,
                o_ref, ck_out, cv_out, cki_out, kn_s, vt_s, ki_s, keys_s, hi_s, lo_s, negm_s, s_s, p_s,
                *, seq, topk):
    j = pl.program_id(1)
    nkt = j // DSA_TRIP_BLOCKS + 1

    @pl.when(j == 0)
    def _prep():
        kn = _rms(ckv_ref[0, :, 0:HEAD_DIM], gk_ref[...])
        cv = ckv_ref[0, :, HEAD_DIM:2 * HEAD_DIM]
        ck_out[0] = kn
        cv_out[0] = cv
        kn_s[...] = kn.astype(MXU_DTYPE)
        eye = (lax.broadcasted_iota(jnp.int32, (HEAD_DIM, HEAD_DIM), 0)
               == lax.broadcasted_iota(jnp.int32, (HEAD_DIM, HEAD_DIM), 1)).astype(MXU_DTYPE)
        vt_s[...] = _nt_dot(eye, cv.astype(MXU_DTYPE)).astype(MXU_DTYPE)
        ki = kiwk_ref[0, :, 0:IDX_DIM]
        cki_out[0] = ki
        ki_s[...] = ki.astype(MXU_DTYPE)

    cq = cq_ref[0]
    q_all = jnp.concatenate(
        [(_rms(cq[:, h * HEAD_DIM:(h + 1) * HEAD_DIM], gq_ref[...]) * HEAD_DIM ** -0.5).astype(MXU_DTYPE)
         for h in range(C_HEADS)], axis=0)
    qi = qi_ref[0].astype(MXU_DTYPE)
    qi_all = jnp.concatenate([qi[:, h * IDX_DIM:(h + 1) * IDX_DIM] for h in range(IDX_HEADS)], axis=0)
    w_t = kiwq_ref[0].T
    w_rows = [w_t[IDX_DIM + h:IDX_DIM + h + 1, :] * INDEX_SCALE for h in range(IDX_HEADS)]
    q_chunk = (j * Q_BLOCK + lax.broadcasted_iota(jnp.int32, (1, Q_BLOCK), 1)) // CHUNK
    kt_iota = lax.broadcasted_iota(jnp.int32, (DSA_KT, 1), 0)

    def trip_rows(kt):
        return pl.multiple_of(kt * DSA_KT, DSA_KT)

    def idx_body(kt, _):
        off = trip_rows(kt)
        dots = _nt_dot(ki_s[pl.ds(off, DSA_KT), :], qi_all)
        score = w_rows[0] * jnp.maximum(dots[:, 0:Q_BLOCK], 0.0)
        for h in range(1, IDX_HEADS):
            score = score + w_rows[h] * jnp.maximum(dots[:, h * Q_BLOCK:(h + 1) * Q_BLOCK], 0.0)
        adm = ((off + kt_iota) // CHUNK) <= q_chunk
        key = jnp.where(adm, _sortable_key(score), jnp.int32(INT_MIN))
        keys_s[pl.ds(off, DSA_KT), :] = key
        hi_s[pl.ds(off, DSA_KT), :] = (key >> 16).astype(jnp.int16)
        lo_s[pl.ds(off, DSA_KT), :] = ((key & 0xFFFF) + I16_MIN).astype(jnp.int16)
        return 0

    lax.fori_loop(0, nkt, idx_body, 0)

    def count16(ref, pred, trips):
        parts = [_fold_rows(jnp.where(pred(ref[r:r + DSA_KT, :]), jnp.int16(1), jnp.int16(0)), 16)
                 for r in range(0, trips * DSA_KT, DSA_KT)]
        acc = _fold_rows(jnp.concatenate(parts, axis=0), 16)
        return jnp.sum(acc.astype(jnp.int32), axis=0, keepdims=True)

    def count(pred):
        def body(kt, acc):
            off = trip_rows(kt)
            hit = jnp.where(pred(keys_s[pl.ds(off, DSA_KT), :], off + kt_iota), 1, 0)
            return acc + jnp.sum(hit.reshape(DSA_KT // 8, 8, Q_BLOCK), axis=0)

        acc = lax.fori_loop(0, nkt, body, jnp.zeros((8, Q_BLOCK), jnp.int32))
        return jnp.sum(acc, axis=0, keepdims=True)

    def kth_largest(trips):
        hi_u = _bisect16(lambda c: count16(hi_s, lambda v: v >= c, trips), topk)
        thr_hi = (hi_u + I16_MIN).astype(jnp.int16)
        need_lo = topk - count16(hi_s, lambda v: v > thr_hi, trips)
        for r in range(0, trips * DSA_KT, DSA_KT):
            lo_s[r:r + DSA_KT, :] = jnp.where(hi_s[r:r + DSA_KT, :] == thr_hi, lo_s[r:r + DSA_KT, :],
                                              jnp.int16(I16_MIN))
        lo_u = _bisect16(lambda c: count16(lo_s, lambda v: v >= c, trips), need_lo)
        n_tied = count16(lo_s, lambda v: v >= (lo_u + I16_MIN).astype(jnp.int16), trips)
        return lax.shift_left(hi_u + I16_MIN, 16) | lo_u, n_tied - need_lo

    thr, surplus = lax.switch(nkt - 1, [functools.partial(kth_largest, n) for n in range(1, seq // DSA_KT + 1)])
    real_thr = thr != jnp.int32(INT_MIN)
    pos_bits = (seq - 1).bit_length()

    def _tie_search():
        need = topk - count(lambda k, pos: k > thr)

        def body(it, v):
            cand = v | lax.shift_left(jnp.int32(1), jnp.int32(pos_bits - 1) - it)
            return jnp.where(count(lambda k, pos: (k == thr) & (pos < cand)) < need, cand, v)

        return lax.fori_loop(0, pos_bits, body, jnp.zeros((1, Q_BLOCK), jnp.int32))

    has_ties = jnp.max(jnp.where(real_thr & (surplus > 0), 1, 0)) > 0
    tie_last = lax.cond(has_ties, _tie_search, lambda: jnp.full((1, Q_BLOCK), 2 ** pos_bits - 1, jnp.int32))
    tie_last = jnp.where(real_thr, tie_last, -1)

    def mask_body(kt, _):
        off = trip_rows(kt)
        keyc = keys_s[pl.ds(off, DSA_KT), :]
        sel = (keyc > thr) | ((keyc == thr) & (off + kt_iota <= tie_last))
        negm_s[pl.ds(off, DSA_KT), :] = jnp.where(sel, 0.0, NEG_INF)
        return 0

    lax.fori_loop(0, nkt, mask_body, 0)

    heads = [slice(h * Q_BLOCK, (h + 1) * Q_BLOCK) for h in range(C_HEADS)]

    def score_body(kt, m8):
        off = trip_rows(kt)
        rows = pl.ds(off, DSA_KT)
        s_all = _nt_dot(kn_s[rows, :], q_all)
        negm = negm_s[rows, :]
        bidx = [jnp.clip(j - (DSA_TRIP_BLOCKS * kt + t), 0, 2) for t in range(DSA_TRIP_BLOCKS)]
        tops = []
        for ls in heads:
            bias = jnp.concatenate([bias_ref[bi, :, ls] for bi in bidx], axis=0)
            s = s_all[:, ls] + bias + negm
            s_s[rows, ls] = s
            tops.append(_fold_rows(s, 8, jnp.maximum))
        return jnp.maximum(m8, jnp.concatenate(tops, axis=1))

    m8 = lax.fori_loop(0, nkt, score_body, jnp.full((8, C_HEADS * Q_BLOCK), NEG_INF, F32))
    m = jnp.max(m8, axis=0, keepdims=True)

    def prob_body(kt, l8):
        rows = pl.ds(trip_rows(kt), DSA_KT)
        p = jnp.exp(s_s[rows, :] - m)
        p_s[rows, :] = p.astype(MXU_DTYPE)
        return l8 + _fold_rows(p, 8)

    l8 = lax.fori_loop(0, nkt, prob_body, jnp.zeros((8, C_HEADS * Q_BLOCK), F32))
    denom = jnp.sum(l8, axis=0, keepdims=True)

    def value_body(kt, acc):
        rows = pl.ds(trip_rows(kt), DSA_KT)
        return acc + _dot(vt_s[:, rows], p_s[rows, :])

    acc = lax.fori_loop(0, nkt, value_body, jnp.zeros((HEAD_DIM, C_HEADS * Q_BLOCK), F32)) / denom
    o_ref[0] = jnp.concatenate([acc[:, ls] for ls in heads], axis=0).T


def _t5_bucket(rel):
    half = T5_BUCKETS // 2
    max_exact = half // 2
    n = jnp.abs(rel)
    log_val = jnp.log(jnp.maximum(n, 1).astype(F32) / max_exact) / math.log(T5_MAX_DIST / max_exact)
    large = jnp.minimum(max_exact + (log_val * (half - max_exact)).astype(jnp.int32), half - 1)
    return jnp.where(rel > 0, half, 0) + jnp.where(n < max_exact, n, large)


def _dsa_bias_tiles(t5_table):
    s = jnp.arange(Q_BLOCK)[:, None]
    t = jnp.arange(Q_BLOCK)[None, :]
    tiles = []
    for d in range(3):
        rel = s - d * Q_BLOCK - t
        tile = _rows_at(t5_table, _t5_bucket(rel))
        tiles.append(tile.transpose(0, 2, 1).reshape(Q_BLOCK, C_HEADS * Q_BLOCK))
    return jnp.stack(tiles)


def _dsa(z3, bias, gq, gk):
    b, seq, _ = z3.shape
    assert seq % DSA_KT == 0
    topk = min(TOPK_MAX, seq // 4)
    kern = functools.partial(_dsa_kernel, seq=seq, topk=topk)
    return pl.pallas_call(
        kern,
        grid=(b, seq // Q_BLOCK),
        in_specs=[
            pl.BlockSpec((1, Q_BLOCK, C_WIDTH), lambda i, j: (i, j, COL_CQ // C_WIDTH)),
            pl.BlockSpec((1, seq, 128), lambda i, j: (i, 0, COL_CKV // 128)),
            pl.BlockSpec((1, Q_BLOCK, 256), lambda i, j: (i, j, COL_QI // 256)),
            pl.BlockSpec((1, Q_BLOCK, 128), lambda i, j: (i, j, COL_KIW // 128)),
            pl.BlockSpec((1, seq, 128), lambda i, j: (i, 0, COL_KIW // 128)),
            pl.BlockSpec((1, HEAD_DIM), lambda i, j: (0, 0)),
            pl.BlockSpec((1, HEAD_DIM), lambda i, j: (0, 0)),
            pl.BlockSpec((3, Q_BLOCK, C_HEADS * Q_BLOCK), lambda i, j: (0, 0, 0)),
        ],
        out_specs=[
            pl.BlockSpec((1, Q_BLOCK, C_WIDTH), lambda i, j: (i, j, 0)),
            pl.BlockSpec((1, seq, HEAD_DIM), lambda i, j: (i, 0, 0)),
            pl.BlockSpec((1, seq, HEAD_DIM), lambda i, j: (i, 0, 0)),
            pl.BlockSpec((1, seq, IDX_DIM), lambda i, j: (i, 0, 0)),
        ],
        out_shape=[
            jax.ShapeDtypeStruct((b, seq, C_WIDTH), F32),
            jax.ShapeDtypeStruct((b, seq, HEAD_DIM), F32),
            jax.ShapeDtypeStruct((b, seq, HEAD_DIM), F32),
            jax.ShapeDtypeStruct((b, seq, IDX_DIM), F32),
        ],
        scratch_shapes=[
            pltpu.VMEM((seq, HEAD_DIM), MXU_DTYPE),
            pltpu.VMEM((HEAD_DIM, seq), MXU_DTYPE),
            pltpu.VMEM((seq, IDX_DIM), MXU_DTYPE),
            pltpu.VMEM((seq, Q_BLOCK), jnp.int32),
            pltpu.VMEM((seq, Q_BLOCK), jnp.int16),
            pltpu.VMEM((seq, Q_BLOCK), jnp.int16),
            pltpu.VMEM((seq, Q_BLOCK), F32),
            pltpu.VMEM((seq, C_HEADS * Q_BLOCK), F32),
            pltpu.VMEM((seq, C_HEADS * Q_BLOCK), MXU_DTYPE),
        ],
        compiler_params=pltpu.CompilerParams(
            dimension_semantics=("parallel", "arbitrary"), vmem_limit_bytes=VMEM_LIMIT),
        name="dsa",
    )(z3, z3, z3, z3, z3, gq, gk, bias)


def _mix_residual(x_ref, oa_ref, ob_ref, oc_ref, wo_ref):
    mix = (_dot(oa_ref[...].astype(MXU_DTYPE), wo_ref[0:A_WIDTH, :])
           + _dot(ob_ref[...].astype(MXU_DTYPE), wo_ref[A_WIDTH:A_WIDTH + S5_WIDTH, :])
           + _dot(oc_ref[...].astype(MXU_DTYPE), wo_ref[A_WIDTH + S5_WIDTH:, :]))
    return x_ref[...] + mix


def _ffn_chunk(xn, g, g1, g2, sl, wu_ref, cw_ref, cb_ref, wd_ref):
    up = _dot(xn, wu_ref[:, sl])
    gc = cw_ref[0:1, sl] * g2 + cw_ref[1:2, sl] * g1 + cw_ref[2:3, sl] * g + cb_ref[:, sl]
    return _dot((jax.nn.silu(gc) * up).astype(MXU_DTYPE), wd_ref[sl, :])


def _out_ffn_kernel(x_ref, oa_ref, ob_ref, oc_ref, wo_ref, g2_ref, wg_ref, wu_ref, cw_ref, cb_ref, wd_ref,
                    y_ref, conv_ref, carry_s, *, tiles_per_seq, tf):
    i = pl.program_id(0)
    tm = x_ref.shape[0]
    x1 = _mix_residual(x_ref, oa_ref, ob_ref, oc_ref, wo_ref)
    xn = _rms(x1, g2_ref[...]).astype(MXU_DTYPE)

    @pl.when(i % tiles_per_seq == 0)
    def _seq_start():
        carry_s[...] = jnp.zeros(carry_s.shape, F32)

    row = lax.broadcasted_iota(jnp.int32, (tm, 1), 0)
    acc = x1
    for f in range(FFN_DIM // tf):
        sl = slice(f * tf, (f + 1) * tf)
        g = _dot(xn, wg_ref[:, sl])
        prev2, prev1 = carry_s[0:1, sl], carry_s[1:2, sl]
        g1 = jnp.where(row == 0, prev1, pltpu.roll(g, 1, 0))
        g2 = jnp.where(row == 0, prev2, jnp.where(row == 1, prev1, pltpu.roll(g, 2, 0)))
        acc = acc + _ffn_chunk(xn, g, g1, g2, sl, wu_ref, cw_ref, cb_ref, wd_ref)
        carry_s[0:2, sl] = g[tm - 2:tm, :]
    y_ref[...] = acc
    conv_ref[0] = carry_s[0:2, :]


def _out_ffn(x2d, oa, ob, oc, wo, g2, wg, wu, cw, cb, wd, *, seq, tm, tf):
    m = x2d.shape[0]
    tiles_per_seq = seq // tm
    kern = functools.partial(_out_ffn_kernel, tiles_per_seq=tiles_per_seq, tf=tf)
    row = lambda w: pl.BlockSpec((tm, w), lambda i: (i, 0))
    const = lambda shape: pl.BlockSpec(shape, lambda i: (0,) * len(shape), pipeline_mode=pl.Buffered(1))
    return pl.pallas_call(
        kern,
        grid=(m // tm,),
        in_specs=[
            row(D_MODEL), row(A_WIDTH), row(S5_WIDTH), row(C_WIDTH),
            const((D_MODEL, D_MODEL)), const((1, D_MODEL)),
            const((D_MODEL, FFN_DIM)), const((D_MODEL, FFN_DIM)),
            const((CONV_W, FFN_DIM)), const((1, FFN_DIM)), const((FFN_DIM, D_MODEL)),
        ],
        out_specs=[
            row(D_MODEL),
            pl.BlockSpec((1, CONV_W - 1, FFN_DIM), lambda i: (i // tiles_per_seq, 0, 0)),
        ],
        out_shape=[
            jax.ShapeDtypeStruct((m, D_MODEL), F32),
            jax.ShapeDtypeStruct((m // seq, CONV_W - 1, FFN_DIM), F32),
        ],
        scratch_shapes=[pltpu.VMEM((8, FFN_DIM), F32)],
        compiler_params=pltpu.CompilerParams(dimension_semantics=("arbitrary",), vmem_limit_bytes=VMEM_LIMIT),
        name="out_ffn",
    )(x2d, oa, ob, oc, wo, g2, wg, wu, cw, cb, wd)


def _layer_params(l, t5_bias, norm1_g, w_in, a_q_gain, a_k_gain, a_rel_bias, s5_lam_re, s5_lam_im, s5_log_dt,
                  s5_b_re, s5_b_im, s5_c_re, s5_c_im, s5_d, s5_w_glu, s5_b_glu, c_q_gain, c_k_gain, w_out,
                  norm2_g, ffn_w_gate, ffn_w_up, ffn_conv_w, ffn_conv_b, ffn_w_down):
    w = w_in[l]
    sizes = (A_WIDTH, A_WIDTH, A_WIDTH, S5_WIDTH, C_WIDTH, HEAD_DIM, HEAD_DIM, IDX_HEADS * IDX_DIM, IDX_DIM, IDX_HEADS)
    cuts = [0]
    for s in sizes:
        cuts.append(cuts[-1] + s)
    aq, ak, av, u, cq, ck, cv, qi, ki, wi = [w[:, cuts[n]:cuts[n + 1]] for n in range(len(sizes))]
    pad = jnp.zeros((D_MODEL, Z_WIDTH - COL_KIW - IDX_DIM - IDX_HEADS), w.dtype)
    w_r = jnp.concatenate([aq, ak, av, cq, u, qi, ck, cv, ki, wi, pad], axis=1).astype(MXU_DTYPE)
    return dict(
        norm1=norm1_g[l].reshape(1, D_MODEL), w_in=w_r,
        a_gq=a_q_gain[l].reshape(1, HEAD_DIM), a_gk=a_k_gain[l].reshape(1, HEAD_DIM), a_rel=a_rel_bias[l],
        s5=_s5_params(s5_lam_re[l], s5_lam_im[l], s5_log_dt[l], s5_b_re[l], s5_b_im[l], s5_c_re[l], s5_c_im[l]),
        s5_d=s5_d[l].reshape(1, S5_WIDTH), s5_wg=s5_w_glu[l].astype(MXU_DTYPE), s5_bg=s5_b_glu[l].reshape(1, S5_WIDTH),
        c_gq=c_q_gain[l].reshape(1, HEAD_DIM), c_gk=c_k_gain[l].reshape(1, HEAD_DIM),
        w_out=w_out[l].astype(MXU_DTYPE), norm2=norm2_g[l].reshape(1, D_MODEL),
        wg=ffn_w_gate[l].astype(MXU_DTYPE), wu=ffn_w_up[l].astype(MXU_DTYPE), cw=ffn_conv_w[l],
        cb=ffn_conv_b[l].reshape(1, FFN_DIM), wd=ffn_w_down[l].astype(MXU_DTYPE),
    )


def _prompt_layer(x, p, dsa_bias, *, tm, tf, s5_bt, s5_tc):
    b, seq, _ = x.shape
    x2d = x.reshape(b * seq, D_MODEL)
    z3 = _in_proj(x2d, p["norm1"], p["w_in"], tm).reshape(b, seq, Z_WIDTH)
    oa, a_k, a_v = _attn_a(z3, _attn_a_bias(p["a_rel"]), p["a_gq"], p["a_gk"])
    zero_h = jnp.zeros((b, S5_FLAT), F32)
    ob, h_re, h_im = _s5(z3, zero_h, zero_h, p["s5"], p["s5_d"], p["s5_wg"], p["s5_bg"],
                         bt=s5_bt, tc=s5_tc, exact_in=False)
    oc, c_k, c_v, c_ki = _dsa(z3, dsa_bias, p["c_gq"], p["c_gk"])
    y, conv = _out_ffn(x2d, oa.reshape(b * seq, A_WIDTH), ob.reshape(b * seq, S5_WIDTH),
                       oc.reshape(b * seq, C_WIDTH), p["w_out"], p["norm2"], p["wg"], p["wu"], p["cw"], p["cb"],
                       p["wd"], seq=seq, tm=tm, tf=tf)
    keep = a_k.shape[1]
    states = (a_k.reshape(b, keep, A_HEADS, HEAD_DIM), a_v.reshape(b, keep, A_HEADS, HEAD_DIM),
              h_re.reshape(b, S5_GROUPS, S5_STATE), h_im.reshape(b, S5_GROUPS, S5_STATE), c_k, c_v, c_ki, conv)
    return y.reshape(b, seq, D_MODEL), states


def _attn_a_step_kernel(q_ref, k_ref, v_ref, ck_ref, cv_ref, bc_ref, bn_ref, gq_ref, gk_ref, o_ref, ak_ref, av_ref):
    v = v_ref[0]
    av_ref[0] = v
    for h in range(A_HEADS):
        sl = slice(h * HEAD_DIM, (h + 1) * HEAD_DIM)
        qh = (_rms(q_ref[0, :, sl], gq_ref[...]) * HEAD_DIM ** -0.5).astype(MXU_DTYPE)
        khn = _rms(k_ref[0, :, sl], gk_ref[...])
        ak_ref[0, :, sl] = khn
        s_c = _nt_dot(qh, ck_ref[0, :, sl].astype(MXU_DTYPE)) + bc_ref[h]
        s_n = _nt_dot(qh, khn.astype(MXU_DTYPE)) + bn_ref[h]
        m = jnp.maximum(jnp.max(s_c, axis=-1, keepdims=True), jnp.max(s_n, axis=-1, keepdims=True))
        p_c = jnp.exp(s_c - m)
        p_n = jnp.exp(s_n - m)
        denom = jnp.sum(p_c, axis=-1, keepdims=True) + jnp.sum(p_n, axis=-1, keepdims=True)
        oh = (_dot(p_c.astype(MXU_DTYPE), cv_ref[0, :, sl].astype(MXU_DTYPE))
              + _dot(p_n.astype(MXU_DTYPE), v[:, sl].astype(MXU_DTYPE)))
        o_ref[0, :, sl] = oh / denom


def _attn_a_step_bias(rel_table, past, t_new):
    t = jnp.arange(t_new)[:, None]
    rel = jnp.concatenate([jnp.arange(past) - past, jnp.arange(t_new)])[None, :] - t
    bias = _rows_at(rel_table.T, jnp.clip(rel, -A_MAX_REL, A_MAX_REL) + A_MAX_REL).transpose(2, 0, 1)
    return bias[:, :, :past], bias[:, :, past:]


def _attn_a_step(z3, cache_k, cache_v, layer, bias_c, bias_n, gq, gk):
    b, t_new, _ = z3.shape
    past = cache_k.shape[2]
    new = lambda col: pl.BlockSpec((1, t_new, A_WIDTH), lambda i: (i, 0, col // A_WIDTH))
    const = lambda shape: pl.BlockSpec(shape, lambda i: (0,) * len(shape))
    return pl.pallas_call(
        _attn_a_step_kernel,
        grid=(b,),
        in_specs=[
            new(COL_AQ), new(COL_AK), new(COL_AV),
            pl.BlockSpec((None, 1, past, A_WIDTH), lambda i: (layer, i, 0, 0)),
            pl.BlockSpec((None, 1, past, A_WIDTH), lambda i: (layer, i, 0, 0)),
            const((A_HEADS, t_new, past)), const((A_HEADS, t_new, t_new)),
            const((1, HEAD_DIM)), const((1, HEAD_DIM)),
        ],
        out_specs=[pl.BlockSpec((1, t_new, A_WIDTH), lambda i: (i, 0, 0))] * 3,
        out_shape=[jax.ShapeDtypeStruct((b, t_new, A_WIDTH), F32)] * 3,
        compiler_params=pltpu.CompilerParams(dimension_semantics=("parallel",), vmem_limit_bytes=VMEM_LIMIT),
        name="attn_a_step",
    )(z3, z3, z3, cache_k, cache_v, bias_c, bias_n, gq, gk)


def _dsa_step_kernel(cq_ref, ckv_ref, qi_ref, kiw_ref, ck_ref, cv_ref, cki_ref, gq_ref, gk_ref, bias_ref,
                     o_ref, ck_out, cv_out, cki_out, *, past, t_new, topk):
    n_keys = past + t_new
    kn_new = _rms(ckv_ref[0, :, 0:HEAD_DIM], gk_ref[...])
    cv_new = ckv_ref[0, :, HEAD_DIM:2 * HEAD_DIM]
    ki_new = kiw_ref[0, :, 0:IDX_DIM]
    ck_out[0] = kn_new
    cv_out[0] = cv_new
    cki_out[0] = ki_new
    k_all = jnp.concatenate([ck_ref[0], kn_new], axis=0).astype(MXU_DTYPE)
    v_all = jnp.concatenate([cv_ref[0], cv_new], axis=0).astype(MXU_DTYPE)
    ki_all = jnp.concatenate([cki_ref[0], ki_new], axis=0).astype(MXU_DTYPE)
    v_t = _nt_dot(_eye(HEAD_DIM, MXU_DTYPE), v_all).astype(MXU_DTYPE)

    def spread(groups, dtype):
        r = lax.broadcasted_iota(jnp.int32, (t_new, groups * t_new), 0)
        c = lax.broadcasted_iota(jnp.int32, (t_new, groups * t_new), 1)
        return (c % t_new == r).astype(dtype)

    qi = qi_ref[0].astype(MXU_DTYPE)
    qi_all = jnp.concatenate([qi[:, h * IDX_DIM:(h + 1) * IDX_DIM] for h in range(IDX_HEADS)], axis=0)
    kiw_pad = jnp.concatenate([kiw_ref[0], jnp.zeros((128 - t_new, 128), F32)], axis=0)
    w_t = kiw_pad.T[:, 0:t_new]
    w_flat = jnp.concatenate([w_t[IDX_DIM + h:IDX_DIM + h + 1, :] for h in range(IDX_HEADS)], axis=1) * INDEX_SCALE
    weighted = w_flat * jnp.maximum(_nt_dot(ki_all, qi_all), 0.0)
    score = lax.dot_general(weighted, spread(IDX_HEADS, F32), (((1,), (1,)), ((), ())),
                            preferred_element_type=F32, precision=lax.Precision.HIGHEST)
    pos = lax.broadcasted_iota(jnp.int32, (n_keys, 1), 0)
    q_chunk = (past + lax.broadcasted_iota(jnp.int32, (1, t_new), 1)) // CHUNK
    keys = jnp.where((pos // CHUNK) <= q_chunk, _sortable_key(score), jnp.int32(INT_MIN))

    def count(pred):
        return jnp.sum(jnp.where(pred(keys, pos), 1, 0), axis=0, keepdims=True)

    def kth_body(it, res):
        cand = res | lax.shift_left(jnp.int32(1), jnp.int32(31) - it)
        thr_c = cand ^ jnp.int32(INT_MIN)
        return jnp.where(count(lambda k, p: k >= thr_c) >= topk, cand, res)

    thr = lax.fori_loop(0, 32, kth_body, jnp.zeros((1, t_new), jnp.int32)) ^ jnp.int32(INT_MIN)
    real_thr = thr != jnp.int32(INT_MIN)
    pos_bits = (n_keys - 1).bit_length()

    def _tie_search():
        need = topk - count(lambda k, p: k > thr)

        def tie_body(it, v):
            cand = v | lax.shift_left(jnp.int32(1), jnp.int32(pos_bits - 1) - it)
            return jnp.where(count(lambda k, p: (k == thr) & (p < cand)) < need, cand, v)

        return lax.fori_loop(0, pos_bits, tie_body, jnp.zeros((1, t_new), jnp.int32))

    has_ties = jnp.max(jnp.where(real_thr & (count(lambda k, p: k >= thr) > topk), 1, 0)) > 0
    tie_last = lax.cond(has_ties, _tie_search, lambda: jnp.full((1, t_new), 2 ** pos_bits - 1, jnp.int32))
    tie_last = jnp.where(real_thr, tie_last, -1)
    dropped = jnp.where((keys > thr) | ((keys == thr) & (pos <= tie_last)), 0.0, 1.0).astype(MXU_DTYPE)
    negm = _dot(dropped, spread(C_HEADS, MXU_DTYPE)) * NEG_INF

    cq = cq_ref[0]
    q_all = jnp.concatenate(
        [(_rms(cq[:, h * HEAD_DIM:(h + 1) * HEAD_DIM], gq_ref[...]) * HEAD_DIM ** -0.5).astype(MXU_DTYPE)
         for h in range(C_HEADS)], axis=0)
    s = _nt_dot(k_all, q_all) + bias_ref[...] + negm
    p = jnp.exp(s - jnp.max(s, axis=0, keepdims=True))
    out_t = _dot(v_t, p.astype(MXU_DTYPE)) / jnp.sum(p, axis=0, keepdims=True)
    hq = C_HEADS * t_new
    out_sq = jnp.concatenate([jnp.concatenate([out_t, jnp.zeros((HEAD_DIM, 128 - hq), F32)], axis=1),
                              jnp.zeros((128 - HEAD_DIM, 128), F32)], axis=0).T
    o_ref[0] = jnp.concatenate([out_sq[h * t_new:(h + 1) * t_new, 0:HEAD_DIM] for h in range(C_HEADS)], axis=1)


def _dsa_step_bias(t5_table, past, t_new):
    s = jnp.arange(past + t_new)[:, None]
    t = jnp.arange(t_new)[None, :]
    return _rows_at(t5_table, _t5_bucket(s - (past + t))).transpose(0, 2, 1).reshape(past + t_new, C_HEADS * t_new)


def _dsa_step(z3, cache_k, cache_v, cache_ki, layer, bias, gq, gk):
    b, t_new, _ = z3.shape
    past = cache_k.shape[2]
    topk = min(TOPK_MAX, (past + t_new) // 4)
    kern = functools.partial(_dsa_step_kernel, past=past, t_new=t_new, topk=topk)
    new = lambda w, col: pl.BlockSpec((1, t_new, w), lambda i: (i, 0, col // w))
    per_seq = lambda n, w: pl.BlockSpec((1, n, w), lambda i: (i, 0, 0))
    cached = lambda w: pl.BlockSpec((None, 1, past, w), lambda i: (layer, i, 0, 0))
    const = lambda shape: pl.BlockSpec(shape, lambda i: (0,) * len(shape))
    return pl.pallas_call(
        kern,
        grid=(b,),
        in_specs=[
            new(C_WIDTH, COL_CQ), new(128, COL_CKV), new(256, COL_QI), new(128, COL_KIW),
            cached(HEAD_DIM), cached(HEAD_DIM), cached(IDX_DIM),
            const((1, HEAD_DIM)), const((1, HEAD_DIM)), const((past + t_new, C_HEADS * t_new)),
        ],
        out_specs=[per_seq(t_new, C_WIDTH), per_seq(t_new, HEAD_DIM), per_seq(t_new, HEAD_DIM),
                   per_seq(t_new, IDX_DIM)],
        out_shape=[
            jax.ShapeDtypeStruct((b, t_new, C_WIDTH), F32),
            jax.ShapeDtypeStruct((b, t_new, HEAD_DIM), F32),
            jax.ShapeDtypeStruct((b, t_new, HEAD_DIM), F32),
            jax.ShapeDtypeStruct((b, t_new, IDX_DIM), F32),
        ],
        compiler_params=pltpu.CompilerParams(dimension_semantics=("parallel",), vmem_limit_bytes=VMEM_LIMIT),
        name="dsa_step",
    )(z3, z3, z3, z3, cache_k, cache_v, cache_ki, gq, gk, bias)


def _out_ffn_step_kernel(x_ref, oa_ref, ob_ref, oc_ref, wo_ref, g2_ref, wg_ref, wu_ref, cw_ref, cb_ref, wd_ref,
                         e1_ref, e2_ref, y_ref, gate_ref, *, t_new, tf):
    tm = x_ref.shape[0]
    x1 = _mix_residual(x_ref, oa_ref, ob_ref, oc_ref, wo_ref)
    xn = _rms(x1, g2_ref[...]).astype(MXU_DTYPE)
    t = lax.broadcasted_iota(jnp.int32, (tm, 1), 0) % t_new
    acc = x1
    for f in range(FFN_DIM // tf):
        sl = slice(f * tf, (f + 1) * tf)
        g = _dot(xn, wg_ref[:, sl])
        gate_ref[:, sl] = g
        g1 = jnp.where(t == 0, e1_ref[:, sl], pltpu.roll(g, 1, 0))
        g2 = jnp.where(t <= 1, e2_ref[:, sl], pltpu.roll(g, 2, 0))
        acc = acc + _ffn_chunk(xn, g, g1, g2, sl, wu_ref, cw_ref, cb_ref, wd_ref)
    y_ref[...] = acc


def _out_ffn_step(x2d, oa, ob, oc, wo, g2, wg, wu, cw, cb, wd, conv_prev, *, t_new, tf):
    m = x2d.shape[0]
    b = m // t_new
    e1 = jnp.zeros((b, t_new, FFN_DIM), F32).at[:, 0].set(conv_prev[:, 1]).reshape(m, FFN_DIM)
    e2 = jnp.zeros((b, t_new, FFN_DIM), F32).at[:, 0].set(conv_prev[:, 0]).at[:, 1].set(conv_prev[:, 1])
    e2 = e2.reshape(m, FFN_DIM)
    kern = functools.partial(_out_ffn_step_kernel, t_new=t_new, tf=tf)
    full = lambda shape: pl.BlockSpec(shape, lambda i: (0,) * len(shape))
    y, gate = pl.pallas_call(
        kern,
        grid=(1,),
        in_specs=[
            full((m, D_MODEL)), full((m, A_WIDTH)), full((m, S5_WIDTH)), full((m, C_WIDTH)),
            full((D_MODEL, D_MODEL)), full((1, D_MODEL)),
            full((D_MODEL, FFN_DIM)), full((D_MODEL, FFN_DIM)),
            full((CONV_W, FFN_DIM)), full((1, FFN_DIM)), full((FFN_DIM, D_MODEL)),
            full((m, FFN_DIM)), full((m, FFN_DIM)),
        ],
        out_specs=[full((m, D_MODEL)), full((m, FFN_DIM))],
        out_shape=[jax.ShapeDtypeStruct((m, D_MODEL), F32), jax.ShapeDtypeStruct((m, FFN_DIM), F32)],
        compiler_params=pltpu.CompilerParams(dimension_semantics=("arbitrary",), vmem_limit_bytes=VMEM_LIMIT),
        name="out_ffn_step",
    )(x2d, oa, ob, oc, wo, g2, wg, wu, cw, cb, wd, e1, e2)
    return y, gate.reshape(b, t_new, FFN_DIM)[:, t_new - (CONV_W - 1):]


def _sample_layer(x, p, dsa_bias, layer, ca_k, ca_v, h_re0, h_im0, cc_k, cc_v, cc_ki, conv_prev, *, tf):
    b, t_new, _ = x.shape
    m = b * t_new
    x2d = x.reshape(m, D_MODEL)
    z3 = _in_proj(x2d, p["norm1"], p["w_in"], m).reshape(b, t_new, Z_WIDTH)
    depth, _, a_past = ca_k.shape[:3]
    bias_c, bias_n = _attn_a_step_bias(p["a_rel"], a_past, t_new)
    oa, a_k, a_v = _attn_a_step(z3, ca_k.reshape(depth, b, a_past, A_WIDTH), ca_v.reshape(depth, b, a_past, A_WIDTH),
                                layer, bias_c, bias_n, p["a_gq"], p["a_gk"])
    ob, h_re, h_im = _s5(z3, h_re0.reshape(b, S5_FLAT), h_im0.reshape(b, S5_FLAT), p["s5"], p["s5_d"], p["s5_wg"],
                         p["s5_bg"], bt=8, tc=t_new, exact_in=True)
    oc, c_k, c_v, c_ki = _dsa_step(z3, cc_k, cc_v, cc_ki, layer, dsa_bias, p["c_gq"], p["c_gk"])
    y, conv = _out_ffn_step(x2d, oa.reshape(m, A_WIDTH), ob.reshape(m, S5_WIDTH), oc.reshape(m, C_WIDTH),
                            p["w_out"], p["norm2"], p["wg"], p["wu"], p["cw"], p["cb"], p["wd"], conv_prev,
                            t_new=t_new, tf=tf)
    states = (a_k.reshape(b, t_new, A_HEADS, HEAD_DIM), a_v.reshape(b, t_new, A_HEADS, HEAD_DIM),
              h_re.reshape(b, S5_GROUPS, S5_STATE), h_im.reshape(b, S5_GROUPS, S5_STATE), c_k, c_v, c_ki, conv)
    return y.reshape(b, t_new, D_MODEL), states


def kernel(x_prompt, x_sample, cache_a_k, cache_a_v, state_s5_re, state_s5_im, cache_c_k, cache_c_v, cache_c_idx_k,
           state_ffn_conv, t5_bias, norm1_g, w_in, a_q_gain, a_k_gain, a_rel_bias, s5_lam_re, s5_lam_im, s5_log_dt,
           s5_b_re, s5_b_im, s5_c_re, s5_c_im, s5_d, s5_w_glu, s5_b_glu, c_q_gain, c_k_gain, w_out, norm2_g,
           ffn_w_gate, ffn_w_up, ffn_conv_w, ffn_conv_b, ffn_w_down):
    depth = w_in.shape[0]
    dsa_bias = _dsa_bias_tiles(t5_bias)
    dsa_step_bias = _dsa_step_bias(t5_bias, cache_c_k.shape[2], x_sample.shape[1])
    yp, ys = x_prompt, x_sample
    prompt_states, sample_states = [], []
    for l in range(depth):
        p = _layer_params(l, t5_bias, norm1_g, w_in, a_q_gain, a_k_gain, a_rel_bias, s5_lam_re, s5_lam_im, s5_log_dt,
                          s5_b_re, s5_b_im, s5_c_re, s5_c_im, s5_d, s5_w_glu, s5_b_glu, c_q_gain, c_k_gain, w_out,
                          norm2_g, ffn_w_gate, ffn_w_up, ffn_conv_w, ffn_conv_b, ffn_w_down)
        yp, st_p = _prompt_layer(yp, p, dsa_bias, tm=512, tf=FFN_DIM, s5_bt=8, s5_tc=256)
        ys, st_s = _sample_layer(ys, p, dsa_step_bias, l, cache_a_k, cache_a_v, state_s5_re[l], state_s5_im[l],
                                 cache_c_k, cache_c_v, cache_c_idx_k, state_ffn_conv[l], tf=FFN_DIM)
        prompt_states.append(st_p)
        sample_states.append(st_s)
    (a_k_p, a_v_p, s5_re_p, s5_im_p, c_k_p, c_v_p, c_ki_p, conv_p) = [jnp.stack(z) for z in zip(*prompt_states)]
    (a_k_s, a_v_s, s5_re_s, s5_im_s, c_k_s, c_v_s, c_ki_s, conv_s) = [jnp.stack(z) for z in zip(*sample_states)]
    return (yp, ys, a_k_p, a_v_p, a_k_s, a_v_s, s5_re_p, s5_im_p, s5_re_s, s5_im_s,
            c_k_p, c_v_p, c_ki_p, c_k_s, c_v_s, c_ki_s, conv_p, conv_s)
```

```python
import functools
import math

import jax
import jax.numpy as jnp
from jax import lax
from jax.experimental import pallas as pl
from jax.experimental.pallas import tpu as pltpu

F32 = jnp.float32
MXU_DTYPE = jnp.bfloat16

D_MODEL = 1024
CHUNK = 64
HEAD_DIM = 64
A_HEADS = 6
A_WIDTH = A_HEADS * HEAD_DIM
A_BAND_PAST = 8 * CHUNK
A_MAX_REL = 128
S5_GROUPS = 16
S5_GROUP_CH = 16
S5_WIDTH = S5_GROUPS * S5_GROUP_CH
S5_STATE = 64
S5_FLAT = S5_GROUPS * S5_STATE
C_HEADS = 6
C_WIDTH = C_HEADS * HEAD_DIM
IDX_HEADS = 8
IDX_DIM = 32
INDEX_SCALE = (IDX_HEADS * IDX_DIM) ** -0.5
TOPK_MAX = 256
Q_BLOCK = 128
T5_BUCKETS = 32
T5_MAX_DIST = 128
FFN_DIM = 2816
CONV_W = 3
EPS = 1e-6
NEG_INF = -1e30
INT_MIN = -(2 ** 31)

COL_AQ, COL_AK, COL_AV, COL_CQ = 0, 384, 768, 1152
COL_U, COL_QI, COL_CKV, COL_KIW = 1536, 1792, 2048, 2176
Z_WIDTH = 2304
VMEM_LIMIT = 56 * 1024 * 1024


def _nt_dot(a, b):
    return lax.dot_general(a, b, (((1,), (1,)), ((), ())), preferred_element_type=F32)


def _dot(a, b):
    return jnp.dot(a, b, preferred_element_type=F32)


def _rows_at(table, idx):
    onehot = (idx[..., None] == jnp.arange(table.shape[0])).astype(F32)
    return jnp.einsum("...n,nw->...w", onehot, table.astype(F32), precision=lax.Precision.HIGHEST)


def _rms(x, g):
    return x * lax.rsqrt(jnp.mean(x * x, axis=-1, keepdims=True) + EPS) * g


def _in_proj_kernel(x_ref, g_ref, w_ref, z_ref):
    xn = _rms(x_ref[...], g_ref[...]).astype(MXU_DTYPE)
    z_ref[...] = _dot(xn, w_ref[...])


def _in_proj(x2d, g, w_r, tm):
    m = x2d.shape[0]
    return pl.pallas_call(
        _in_proj_kernel,
        grid=(m // tm,),
        in_specs=[
            pl.BlockSpec((tm, D_MODEL), lambda i: (i, 0)),
            pl.BlockSpec((1, D_MODEL), lambda i: (0, 0)),
            pl.BlockSpec((D_MODEL, Z_WIDTH), lambda i: (0, 0)),
        ],
        out_specs=pl.BlockSpec((tm, Z_WIDTH), lambda i: (i, 0)),
        out_shape=jax.ShapeDtypeStruct((m, Z_WIDTH), F32),
        compiler_params=pltpu.CompilerParams(dimension_semantics=("parallel",), vmem_limit_bytes=VMEM_LIMIT),
        name="in_proj",
    )(x2d, g, w_r)


A_QB = 2 * CHUNK
A_BAND = A_BAND_PAST + A_QB


def _eye(n, dtype):
    return (lax.broadcasted_iota(jnp.int32, (n, n), 0) == lax.broadcasted_iota(jnp.int32, (n, n), 1)).astype(dtype)


def _attn_a_kernel(q_ref, k_ref, v_ref, bias_ref, gq_ref, gk_ref, o_ref, ak_ref, av_ref, kn_s, vt_s, s_s, p_s,
                   *, seq, keep):
    j = pl.program_id(1)

    @pl.when(j == 0)
    def _prep():
        for h in range(A_HEADS):
            sl = slice(h * HEAD_DIM, (h + 1) * HEAD_DIM)
            khn = _rms(k_ref[0, :, sl], gk_ref[...])
            kn_s[h, 0:A_BAND_PAST, :] = jnp.zeros((A_BAND_PAST, HEAD_DIM), MXU_DTYPE)
            kn_s[h, A_BAND_PAST:A_BAND_PAST + seq, :] = khn.astype(MXU_DTYPE)
            ak_ref[0, :, sl] = khn[seq - keep:, :]
        v = v_ref[0]
        av_ref[0] = v[seq - keep:, :]
        vt_s[:, 0:A_BAND_PAST] = jnp.zeros((A_WIDTH, A_BAND_PAST), MXU_DTYPE)
        vt_s[:, A_BAND_PAST:A_BAND_PAST + seq] = _nt_dot(_eye(A_WIDTH, MXU_DTYPE), v.astype(MXU_DTYPE)).astype(MXU_DTYPE)

    start = pl.multiple_of(j * A_QB, A_QB)
    variant = jnp.minimum(j, A_LEAD_STEPS)
    for h in range(A_HEADS):
        sl = slice(h * HEAD_DIM, (h + 1) * HEAD_DIM)
        qh = (_rms(q_ref[0, :, sl], gq_ref[...]) * HEAD_DIM ** -0.5).astype(MXU_DTYPE)
        s_s[h] = _nt_dot(kn_s[h, pl.ds(start, A_BAND), :], qh) + bias_ref[variant, h]
    denoms = []
    for h in range(A_HEADS):
        s = s_s[h]
        p = jnp.exp(s - jnp.max(s, axis=0, keepdims=True))
        denoms.append(jnp.sum(p, axis=0, keepdims=True))
        p_s[h] = p.astype(MXU_DTYPE)
    outs = [_dot(vt_s[h * HEAD_DIM:(h + 1) * HEAD_DIM, pl.ds(start, A_BAND)], p_s[h]) / denoms[h]
            for h in range(A_HEADS)]
    o_ref[0] = jnp.concatenate(outs, axis=0).T


A_LEAD_STEPS = A_BAND_PAST // A_QB


def _attn_a_bias(rel_table):
    n_ext = A_BAND + A_QB - 1
    m = jnp.arange(n_ext)
    ext = _rows_at(rel_table.T, jnp.clip(m - (A_QB - 1) - A_BAND_PAST, -A_MAX_REL, A_MAX_REL) + A_MAX_REL).T
    rows = jnp.pad(jnp.broadcast_to(ext[:, None, :], (A_HEADS, A_QB, n_ext)), ((0, 0), (0, 0), (0, 1)))
    skew = rows.reshape(A_HEADS, A_QB * (n_ext + 1))[:, :A_QB * n_ext].reshape(A_HEADS, A_QB, n_ext)
    bias = skew[:, :, A_QB - 1:].transpose(0, 2, 1)
    c = jnp.arange(A_BAND)[:, None]
    lo = (jnp.arange(A_QB)[None, :] // CHUNK) * CHUNK
    in_band = (c >= lo) & (c < lo + A_BAND_PAST + CHUNK)
    first_live = (A_LEAD_STEPS - jnp.arange(A_LEAD_STEPS + 1)) * A_QB
    live = in_band[None] & (c[None] >= first_live[:, None, None])
    return jnp.where(live[:, None], bias[None], NEG_INF)


def _attn_a(z3, bias, gq, gk):
    b, seq, _ = z3.shape
    keep = min(A_BAND_PAST, seq)
    kern = functools.partial(_attn_a_kernel, seq=seq, keep=keep)
    return pl.pallas_call(
        kern,
        grid=(b, seq // A_QB),
        in_specs=[
            pl.BlockSpec((1, A_QB, A_WIDTH), lambda i, j: (i, j, COL_AQ // A_WIDTH)),
            pl.BlockSpec((1, seq, A_WIDTH), lambda i, j: (i, 0, COL_AK // A_WIDTH)),
            pl.BlockSpec((1, seq, A_WIDTH), lambda i, j: (i, 0, COL_AV // A_WIDTH)),
            pl.BlockSpec((A_LEAD_STEPS + 1, A_HEADS, A_BAND, A_QB), lambda i, j: (0, 0, 0, 0),
                         pipeline_mode=pl.Buffered(1)),
            pl.BlockSpec((1, HEAD_DIM), lambda i, j: (0, 0)),
            pl.BlockSpec((1, HEAD_DIM), lambda i, j: (0, 0)),
        ],
        out_specs=[
            pl.BlockSpec((1, A_QB, A_WIDTH), lambda i, j: (i, j, 0)),
            pl.BlockSpec((1, keep, A_WIDTH), lambda i, j: (i, 0, 0)),
            pl.BlockSpec((1, keep, A_WIDTH), lambda i, j: (i, 0, 0)),
        ],
        out_shape=[
            jax.ShapeDtypeStruct((b, seq, A_WIDTH), F32),
            jax.ShapeDtypeStruct((b, keep, A_WIDTH), F32),
            jax.ShapeDtypeStruct((b, keep, A_WIDTH), F32),
        ],
        scratch_shapes=[
            pltpu.VMEM((A_HEADS, A_BAND_PAST + seq, HEAD_DIM), MXU_DTYPE),
            pltpu.VMEM((A_WIDTH, A_BAND_PAST + seq), MXU_DTYPE),
            pltpu.VMEM((A_HEADS, A_BAND, A_QB), F32),
            pltpu.VMEM((A_HEADS, A_BAND, A_QB), MXU_DTYPE),
        ],
        compiler_params=pltpu.CompilerParams(
            dimension_semantics=("parallel", "arbitrary"), vmem_limit_bytes=VMEM_LIMIT),
        name="attn_a",
    )(z3, z3, z3, bias, gq, gk)


def _s5_kernel(u_ref, h0r_ref, h0i_ref, ar_ref, ai_ref, wbr_ref, wbi_ref, wcr_ref, wci_ref, d_ref, wg_ref, bg_ref,
               o_ref, hr_out, hi_out, xr_s, xi_s, hr_c, hi_c, *, bt, tc, exact_in):
    c = pl.program_id(1)

    @pl.when(c == 0)
    def _init():
        hr_c[...] = h0r_ref[...]
        hi_c[...] = h0i_ref[...]

    u = jnp.swapaxes(u_ref[...], 0, 1).reshape(tc * bt, S5_WIDTH)
    if exact_in:
        xr_s[...] = jnp.dot(u, wbr_ref[...], preferred_element_type=F32, precision=lax.Precision.HIGHEST)
        xi_s[...] = jnp.dot(u, wbi_ref[...], preferred_element_type=F32, precision=lax.Precision.HIGHEST)
    else:
        ub = u.astype(MXU_DTYPE)
        xr_s[...] = _dot(ub, wbr_ref[...])
        xi_s[...] = _dot(ub, wbi_ref[...])
    ar = ar_ref[...]
    ai = ai_ref[...]

    def step(t, carry):
        hr, hi = carry
        rows = pl.ds(pl.multiple_of(t * bt, bt), bt)
        nhr = ar * hr - ai * hi + xr_s[rows, :]
        nhi = ar * hi + ai * hr + xi_s[rows, :]
        xr_s[rows, :] = nhr
        xi_s[rows, :] = nhi
        return nhr, nhi

    hr, hi = lax.fori_loop(0, tc, step, (hr_c[...], hi_c[...]))
    hr_c[...] = hr
    hi_c[...] = hi
    hr_out[...] = hr
    hi_out[...] = hi
    y = (_dot(xr_s[...].astype(MXU_DTYPE), wcr_ref[...]) - _dot(xi_s[...].astype(MXU_DTYPE), wci_ref[...])
         + d_ref[...] * u)
    g = jax.nn.gelu(y)
    out = g * jax.nn.sigmoid(_dot(g.astype(MXU_DTYPE), wg_ref[...]) + bg_ref[...])
    o_ref[...] = jnp.swapaxes(out.reshape(tc, bt, S5_WIDTH), 0, 1)


def _s5_params(lam_re, lam_im, log_dt, b_re, b_im, c_re, c_im):
    dt = jnp.exp(log_dt.astype(F32))[:, None]
    lr, li = lam_re.astype(F32), lam_im.astype(F32)
    mag = jnp.exp(lr * dt)
    ab_re, ab_im = mag * jnp.cos(li * dt), mag * jnp.sin(li * dt)
    den = lr * lr + li * li
    cr = ((ab_re - 1.0) * lr + ab_im * li) / den
    ci = (ab_im * lr - (ab_re - 1.0) * li) / den
    br, bi = b_re.astype(F32), b_im.astype(F32)
    bb_re = cr[..., None] * br - ci[..., None] * bi
    bb_im = cr[..., None] * bi + ci[..., None] * br
    eye = jnp.eye(S5_GROUPS, dtype=F32)
    wb_re = jnp.einsum("gpc,gh->gchp", bb_re, eye).reshape(S5_WIDTH, S5_FLAT)
    wb_im = jnp.einsum("gpc,gh->gchp", bb_im, eye).reshape(S5_WIDTH, S5_FLAT)
    wc_re = jnp.einsum("gcp,gh->gphc", c_re.astype(F32), eye).reshape(S5_FLAT, S5_WIDTH).astype(MXU_DTYPE)
    wc_im = jnp.einsum("gcp,gh->gphc", c_im.astype(F32), eye).reshape(S5_FLAT, S5_WIDTH).astype(MXU_DTYPE)
    return ab_re.reshape(1, S5_FLAT), ab_im.reshape(1, S5_FLAT), wb_re, wb_im, wc_re, wc_im


def _s5(z3, h0r, h0i, sp, d, wg, bg, *, bt, tc, exact_in):
    b, seq, _ = z3.shape
    ar, ai, wbr, wbi, wcr, wci = sp
    if not exact_in:
        wbr, wbi = wbr.astype(MXU_DTYPE), wbi.astype(MXU_DTYPE)
    kern = functools.partial(_s5_kernel, bt=bt, tc=tc, exact_in=exact_in)
    const = lambda shape: pl.BlockSpec(shape, lambda i, c: (0,) * len(shape))
    return pl.pallas_call(
        kern,
        grid=(b // bt, seq // tc),
        in_specs=[
            pl.BlockSpec((bt, tc, S5_WIDTH), lambda i, c: (i, c, COL_U // S5_WIDTH)),
            pl.BlockSpec((bt, S5_FLAT), lambda i, c: (i, 0)),
            pl.BlockSpec((bt, S5_FLAT), lambda i, c: (i, 0)),
            const((1, S5_FLAT)), const((1, S5_FLAT)),
            const((S5_WIDTH, S5_FLAT)), const((S5_WIDTH, S5_FLAT)),
            const((S5_FLAT, S5_WIDTH)), const((S5_FLAT, S5_WIDTH)),
            const((1, S5_WIDTH)), const((S5_WIDTH, S5_WIDTH)), const((1, S5_WIDTH)),
        ],
        out_specs=[
            pl.BlockSpec((bt, tc, S5_WIDTH), lambda i, c: (i, c, 0)),
            pl.BlockSpec((bt, S5_FLAT), lambda i, c: (i, 0)),
            pl.BlockSpec((bt, S5_FLAT), lambda i, c: (i, 0)),
        ],
        out_shape=[
            jax.ShapeDtypeStruct((b, seq, S5_WIDTH), F32),
            jax.ShapeDtypeStruct((b, S5_FLAT), F32),
            jax.ShapeDtypeStruct((b, S5_FLAT), F32),
        ],
        scratch_shapes=[
            pltpu.VMEM((bt * tc, S5_FLAT), F32), pltpu.VMEM((bt * tc, S5_FLAT), F32),
            pltpu.VMEM((bt, S5_FLAT), F32), pltpu.VMEM((bt, S5_FLAT), F32),
        ],
        compiler_params=pltpu.CompilerParams(
            dimension_semantics=("parallel", "arbitrary"), vmem_limit_bytes=VMEM_LIMIT),
        name="s5",
    )(z3, h0r, h0i, ar, ai, wbr, wbi, wcr, wci, d, wg, bg)


def _sortable_key(score):
    bits = lax.bitcast_convert_type(score + 0.0, jnp.int32)
    return bits ^ ((bits >> 31) & jnp.int32(0x7FFFFFFF))


DSA_TRIP_BLOCKS = 4
DSA_KT = DSA_TRIP_BLOCKS * Q_BLOCK
I16_MIN = -(2 ** 15)


def _fold_rows(x, rows, op=jnp.add):
    parts = [x[i:i + rows] for i in range(0, x.shape[0], rows)]
    while len(parts) > 1:
        parts = [op(a, b) for a, b in zip(parts[0::2], parts[1::2])] + parts[len(parts) & ~1:]
    return parts[0]


def _bisect16(count_ge, k):
    def body(it, res):
        cand = res | lax.shift_left(jnp.int32(1), jnp.int32(15) - it)
        cnt = count_ge((cand + I16_MIN).astype(jnp.int16))
        return jnp.where(cnt >= k, cand, res)

    return lax.fori_loop(0, 16, body, jnp.zeros((1, Q_BLOCK), jnp.int32))


def _dsa_kernel(cq_ref, ckv_ref, qi_ref, kiwq_ref, kiwk_ref, gq_ref, gk_ref, bias_ref,
                o_ref, ck_out, cv_out, cki_out, kn_s, vt_s, ki_s, keys_s, hi_s, lo_s, negm_s, s_s, p_s,
                *, seq, topk):
    j = pl.program_id(1)
    nkt = j // DSA_TRIP_BLOCKS + 1

    @pl.when(j == 0)
    def _prep():
        kn = _rms(ckv_ref[0, :, 0:HEAD_DIM], gk_ref[...])
        cv = ckv_ref[0, :, HEAD_DIM:2 * HEAD_DIM]
        ck_out[0] = kn
        cv_out[0] = cv
        kn_s[...] = kn.astype(MXU_DTYPE)
        eye = (lax.broadcasted_iota(jnp.int32, (HEAD_DIM, HEAD_DIM), 0)
               == lax.broadcasted_iota(jnp.int32, (HEAD_DIM, HEAD_DIM), 1)).astype(MXU_DTYPE)
        vt_s[...] = _nt_dot(eye, cv.astype(MXU_DTYPE)).astype(MXU_DTYPE)
        ki = kiwk_ref[0, :, 0:IDX_DIM]
        cki_out[0] = ki
        ki_s[...] = ki.astype(MXU_DTYPE)

    cq = cq_ref[0]
    q_all = jnp.concatenate(
        [(_rms(cq[:, h * HEAD_DIM:(h + 1) * HEAD_DIM], gq_ref[...]) * HEAD_DIM ** -0.5).astype(MXU_DTYPE)
         for h in range(C_HEADS)], axis=0)
    qi = qi_ref[0].astype(MXU_DTYPE)
    qi_all = jnp.concatenate([qi[:, h * IDX_DIM:(h + 1) * IDX_DIM] for h in range(IDX_HEADS)], axis=0)
    w_t = kiwq_ref[0].T
    w_rows = [w_t[IDX_DIM + h:IDX_DIM + h + 1, :] * INDEX_SCALE for h in range(IDX_HEADS)]
    q_chunk = (j * Q_BLOCK + lax.broadcasted_iota(jnp.int32, (1, Q_BLOCK), 1)) // CHUNK
    kt_iota = lax.broadcasted_iota(jnp.int32, (DSA_KT, 1), 0)

    def trip_rows(kt):
        return pl.multiple_of(kt * DSA_KT, DSA_KT)

    def idx_body(kt, _):
        off = trip_rows(kt)
        dots = _nt_dot(ki_s[pl.ds(off, DSA_KT), :], qi_all)
        score = w_rows[0] * jnp.maximum(dots[:, 0:Q_BLOCK], 0.0)
        for h in range(1, IDX_HEADS):
            score = score + w_rows[h] * jnp.maximum(dots[:, h * Q_BLOCK:(h + 1) * Q_BLOCK], 0.0)
        adm = ((off + kt_iota) // CHUNK) <= q_chunk
        key = jnp.where(adm, _sortable_key(score), jnp.int32(INT_MIN))
        keys_s[pl.ds(off, DSA_KT), :] = key
        hi_s[pl.ds(off, DSA_KT), :] = (key >> 16).astype(jnp.int16)
        lo_s[pl.ds(off, DSA_KT), :] = ((key & 0xFFFF) + I16_MIN).astype(jnp.int16)
        return 0

    lax.fori_loop(0, nkt, idx_body, 0)

    def count16(ref, pred, trips):
        parts = [_fold_rows(jnp.where(pred(ref[r:r + DSA_KT, :]), jnp.int16(1), jnp.int16(0)), 16)
                 for r in range(0, trips * DSA_KT, DSA_KT)]
        acc = _fold_rows(jnp.concatenate(parts, axis=0), 16)
        return jnp.sum(acc.astype(jnp.int32), axis=0, keepdims=True)

    def count(pred):
        def body(kt, acc):
            off = trip_rows(kt)
            hit = jnp.where(pred(keys_s[pl.ds(off, DSA_KT), :], off + kt_iota), 1, 0)
            return acc + jnp.sum(hit.reshape(DSA_KT // 8, 8, Q_BLOCK), axis=0)

        acc = lax.fori_loop(0, nkt, body, jnp.zeros((8, Q_BLOCK), jnp.int32))
        return jnp.sum(acc, axis=0, keepdims=True)

    def kth_largest(trips):
        hi_u = _bisect16(lambda c: count16(hi_s, lambda v: v >= c, trips), topk)
        thr_hi = (hi_u + I16_MIN).astype(jnp.int16)
        need_lo = topk - count16(hi_s, lambda v: v > thr_hi, trips)
        for r in range(0, trips * DSA_KT, DSA_KT):
            lo_s[r:r + DSA_KT, :] = jnp.where(hi_s[r:r + DSA_KT, :] == thr_hi, lo_s[r:r + DSA_KT, :],
                                              jnp.int16(I16_MIN))
        lo_u = _bisect16(lambda c: count16(lo_s, lambda v: v >= c, trips), need_lo)
        n_tied = count16(lo_s, lambda v: v >= (lo_u + I16_MIN).astype(jnp.int16), trips)
        return lax.shift_left(hi_u + I16_MIN, 16) | lo_u, n_tied - need_lo

    def take_all():
        return jnp.full((1, Q_BLOCK), INT_MIN, jnp.int32), jnp.zeros((1, Q_BLOCK), jnp.int32)

    branch = jnp.where((j + 1) * Q_BLOCK <= topk, 0, nkt)
    thr, surplus = lax.switch(
        branch, [take_all] + [functools.partial(kth_largest, n) for n in range(1, seq // DSA_KT + 1)])
    real_thr = thr != jnp.int32(INT_MIN)
    pos_bits = (seq - 1).bit_length()

    def _tie_search():
        need = topk - count(lambda k, pos: k > thr)

        def body(it, v):
            cand = v | lax.shift_left(jnp.int32(1), jnp.int32(pos_bits - 1) - it)
            return jnp.where(count(lambda k, pos: (k == thr) & (pos < cand)) < need, cand, v)

        return lax.fori_loop(0, pos_bits, body, jnp.zeros((1, Q_BLOCK), jnp.int32))

    has_ties = jnp.max(jnp.where(real_thr & (surplus > 0), 1, 0)) > 0
    tie_last = lax.cond(has_ties, _tie_search, lambda: jnp.full((1, Q_BLOCK), 2 ** pos_bits - 1, jnp.int32))
    tie_last = jnp.where(real_thr, tie_last, -1)

    def mask_body(kt, _):
        off = trip_rows(kt)
        keyc = keys_s[pl.ds(off, DSA_KT), :]
        sel = (keyc > thr) | ((keyc == thr) & (off + kt_iota <= tie_last))
        negm_s[pl.ds(off, DSA_KT), :] = jnp.where(sel, 0.0, NEG_INF)
        return 0

    lax.fori_loop(0, nkt, mask_body, 0)

    heads = [slice(h * Q_BLOCK, (h + 1) * Q_BLOCK) for h in range(C_HEADS)]

    def score_body(kt, m8):
        off = trip_rows(kt)
        rows = pl.ds(off, DSA_KT)
        s_all = _nt_dot(kn_s[rows, :], q_all)
        negm = negm_s[rows, :]
        bidx = [jnp.clip(j - (DSA_TRIP_BLOCKS * kt + t), 0, 2) for t in range(DSA_TRIP_BLOCKS)]
        tops = []
        for ls in heads:
            bias = jnp.concatenate([bias_ref[bi, :, ls] for bi in bidx], axis=0)
            s = s_all[:, ls] + bias + negm
            s_s[rows, ls] = s
            tops.append(_fold_rows(s, 8, jnp.maximum))
        return jnp.maximum(m8, jnp.concatenate(tops, axis=1))

    m8 = lax.fori_loop(0, nkt, score_body, jnp.full((8, C_HEADS * Q_BLOCK), NEG_INF, F32))
    m = jnp.max(m8, axis=0, keepdims=True)

    def prob_body(kt, l8):
        rows = pl.ds(trip_rows(kt), DSA_KT)
        p = jnp.exp(s_s[rows, :] - m)
        p_s[rows, :] = p.astype(MXU_DTYPE)
        return l8 + _fold_rows(p, 8)

    l8 = lax.fori_loop(0, nkt, prob_body, jnp.zeros((8, C_HEADS * Q_BLOCK), F32))
    denom = jnp.sum(l8, axis=0, keepdims=True)

    def value_body(kt, acc):
        rows = pl.ds(trip_rows(kt), DSA_KT)
        return acc + _dot(vt_s[:, rows], p_s[rows, :])

    acc = lax.fori_loop(0, nkt, value_body, jnp.zeros((HEAD_DIM, C_HEADS * Q_BLOCK), F32)) / denom
    o_ref[0] = jnp.concatenate([acc[:, ls] for ls in heads], axis=0).T


def _t5_bucket(rel):
    half = T5_BUCKETS // 2
    max_exact = half // 2
    n = jnp.abs(rel)
    log_val = jnp.log(jnp.maximum(n, 1).astype(F32) / max_exact) / math.log(T5_MAX_DIST / max_exact)
    large = jnp.minimum(max_exact + (log_val * (half - max_exact)).astype(jnp.int32), half - 1)
    return jnp.where(rel > 0, half, 0) + jnp.where(n < max_exact, n, large)


def _dsa_bias_tiles(t5_table):
    s = jnp.arange(Q_BLOCK)[:, None]
    t = jnp.arange(Q_BLOCK)[None, :]
    tiles = []
    for d in range(3):
        rel = s - d * Q_BLOCK - t
        tile = _rows_at(t5_table, _t5_bucket(rel))
        tiles.append(tile.transpose(0, 2, 1).reshape(Q_BLOCK, C_HEADS * Q_BLOCK))
    return jnp.stack(tiles)


def _dsa(z3, bias, gq, gk):
    b, seq, _ = z3.shape
    assert seq % DSA_KT == 0
    topk = min(TOPK_MAX, seq // 4)
    kern = functools.partial(_dsa_kernel, seq=seq, topk=topk)
    return pl.pallas_call(
        kern,
        grid=(b, seq // Q_BLOCK),
        in_specs=[
            pl.BlockSpec((1, Q_BLOCK, C_WIDTH), lambda i, j: (i, j, COL_CQ // C_WIDTH)),
            pl.BlockSpec((1, seq, 128), lambda i, j: (i, 0, COL_CKV // 128)),
            pl.BlockSpec((1, Q_BLOCK, 256), lambda i, j: (i, j, COL_QI // 256)),
            pl.BlockSpec((1, Q_BLOCK, 128), lambda i, j: (i, j, COL_KIW // 128)),
            pl.BlockSpec((1, seq, 128), lambda i, j: (i, 0, COL_KIW // 128)),
            pl.BlockSpec((1, HEAD_DIM), lambda i, j: (0, 0)),
            pl.BlockSpec((1, HEAD_DIM), lambda i, j: (0, 0)),
            pl.BlockSpec((3, Q_BLOCK, C_HEADS * Q_BLOCK), lambda i, j: (0, 0, 0)),
        ],
        out_specs=[
            pl.BlockSpec((1, Q_BLOCK, C_WIDTH), lambda i, j: (i, j, 0)),
            pl.BlockSpec((1, seq, HEAD_DIM), lambda i, j: (i, 0, 0)),
            pl.BlockSpec((1, seq, HEAD_DIM), lambda i, j: (i, 0, 0)),
            pl.BlockSpec((1, seq, IDX_DIM), lambda i, j: (i, 0, 0)),
        ],
        out_shape=[
            jax.ShapeDtypeStruct((b, seq, C_WIDTH), F32),
            jax.ShapeDtypeStruct((b, seq, HEAD_DIM), F32),
            jax.ShapeDtypeStruct((b, seq, HEAD_DIM), F32),
            jax.ShapeDtypeStruct((b, seq, IDX_DIM), F32),
        ],
        scratch_shapes=[
            pltpu.VMEM((seq, HEAD_DIM), MXU_DTYPE),
            pltpu.VMEM((HEAD_DIM, seq), MXU_DTYPE),
            pltpu.VMEM((seq, IDX_DIM), MXU_DTYPE),
            pltpu.VMEM((seq, Q_BLOCK), jnp.int32),
            pltpu.VMEM((seq, Q_BLOCK), jnp.int16),
            pltpu.VMEM((seq, Q_BLOCK), jnp.int16),
            pltpu.VMEM((seq, Q_BLOCK), F32),
            pltpu.VMEM((seq, C_HEADS * Q_BLOCK), F32),
            pltpu.VMEM((seq, C_HEADS * Q_BLOCK), MXU_DTYPE),
        ],
        compiler_params=pltpu.CompilerParams(
            dimension_semantics=("parallel", "arbitrary"), vmem_limit_bytes=VMEM_LIMIT),
        name="dsa",
    )(z3, z3, z3, z3, z3, gq, gk, bias)


def _mix_residual(x_ref, oa_ref, ob_ref, oc_ref, wo_ref):
    mix = (_dot(oa_ref[...].astype(MXU_DTYPE), wo_ref[0:A_WIDTH, :])
           + _dot(ob_ref[...].astype(MXU_DTYPE), wo_ref[A_WIDTH:A_WIDTH + S5_WIDTH, :])
           + _dot(oc_ref[...].astype(MXU_DTYPE), wo_ref[A_WIDTH + S5_WIDTH:, :]))
    return x_ref[...] + mix


def _ffn_chunk(xn, g, g1, g2, sl, wu_ref, cw_ref, cb_ref, wd_ref):
    up = _dot(xn, wu_ref[:, sl])
    gc = cw_ref[0:1, sl] * g2 + cw_ref[1:2, sl] * g1 + cw_ref[2:3, sl] * g + cb_ref[:, sl]
    return _dot((jax.nn.silu(gc) * up).astype(MXU_DTYPE), wd_ref[sl, :])


def _out_ffn_kernel(x_ref, oa_ref, ob_ref, oc_ref, wo_ref, g2_ref, wg_ref, wu_ref, cw_ref, cb_ref, wd_ref,
                    y_ref, conv_ref, carry_s, *, tiles_per_seq, tf):
    i = pl.program_id(0)
    tm = x_ref.shape[0]
    x1 = _mix_residual(x_ref, oa_ref, ob_ref, oc_ref, wo_ref)
    xn = _rms(x1, g2_ref[...]).astype(MXU_DTYPE)

    @pl.when(i % tiles_per_seq == 0)
    def _seq_start():
        carry_s[...] = jnp.zeros(carry_s.shape, F32)

    row = lax.broadcasted_iota(jnp.int32, (tm, 1), 0)
    acc = x1
    for f in range(FFN_DIM // tf):
        sl = slice(f * tf, (f + 1) * tf)
        g = _dot(xn, wg_ref[:, sl])
        prev2, prev1 = carry_s[0:1, sl], carry_s[1:2, sl]
        g1 = jnp.where(row == 0, prev1, pltpu.roll(g, 1, 0))
        g2 = jnp.where(row == 0, prev2, jnp.where(row == 1, prev1, pltpu.roll(g, 2, 0)))
        acc = acc + _ffn_chunk(xn, g, g1, g2, sl, wu_ref, cw_ref, cb_ref, wd_ref)
        carry_s[0:2, sl] = g[tm - 2:tm, :]
    y_ref[...] = acc
    conv_ref[0] = carry_s[0:2, :]


def _out_ffn(x2d, oa, ob, oc, wo, g2, wg, wu, cw, cb, wd, *, seq, tm, tf):
    m = x2d.shape[0]
    tiles_per_seq = seq // tm
    kern = functools.partial(_out_ffn_kernel, tiles_per_seq=tiles_per_seq, tf=tf)
    row = lambda w: pl.BlockSpec((tm, w), lambda i: (i, 0))
    const = lambda shape: pl.BlockSpec(shape, lambda i: (0,) * len(shape), pipeline_mode=pl.Buffered(1))
    return pl.pallas_call(
        kern,
        grid=(m // tm,),
        in_specs=[
            row(D_MODEL), row(A_WIDTH), row(S5_WIDTH), row(C_WIDTH),
            const((D_MODEL, D_MODEL)), const((1, D_MODEL)),
            const((D_MODEL, FFN_DIM)), const((D_MODEL, FFN_DIM)),
            const((CONV_W, FFN_DIM)), const((1, FFN_DIM)), const((FFN_DIM, D_MODEL)),
        ],
        out_specs=[
            row(D_MODEL),
            pl.BlockSpec((1, CONV_W - 1, FFN_DIM), lambda i: (i // tiles_per_seq, 0, 0)),
        ],
        out_shape=[
            jax.ShapeDtypeStruct((m, D_MODEL), F32),
            jax.ShapeDtypeStruct((m // seq, CONV_W - 1, FFN_DIM), F32),
        ],
        scratch_shapes=[pltpu.VMEM((8, FFN_DIM), F32)],
        compiler_params=pltpu.CompilerParams(dimension_semantics=("arbitrary",), vmem_limit_bytes=VMEM_LIMIT),
        name="out_ffn",
    )(x2d, oa, ob, oc, wo, g2, wg, wu, cw, cb, wd)


def _layer_params(l, t5_bias, norm1_g, w_in, a_q_gain, a_k_gain, a_rel_bias, s5_lam_re, s5_lam_im, s5_log_dt,
                  s5_b_re, s5_b_im, s5_c_re, s5_c_im, s5_d, s5_w_glu, s5_b_glu, c_q_gain, c_k_gain, w_out,
                  norm2_g, ffn_w_gate, ffn_w_up, ffn_conv_w, ffn_conv_b, ffn_w_down):
    w = w_in[l]
    sizes = (A_WIDTH, A_WIDTH, A_WIDTH, S5_WIDTH, C_WIDTH, HEAD_DIM, HEAD_DIM, IDX_HEADS * IDX_DIM, IDX_DIM, IDX_HEADS)
    cuts = [0]
    for s in sizes:
        cuts.append(cuts[-1] + s)
    aq, ak, av, u, cq, ck, cv, qi, ki, wi = [w[:, cuts[n]:cuts[n + 1]] for n in range(len(sizes))]
    pad = jnp.zeros((D_MODEL, Z_WIDTH - COL_KIW - IDX_DIM - IDX_HEADS), w.dtype)
    w_r = jnp.concatenate([aq, ak, av, cq, u, qi, ck, cv, ki, wi, pad], axis=1).astype(MXU_DTYPE)
    return dict(
        norm1=norm1_g[l].reshape(1, D_MODEL), w_in=w_r,
        a_gq=a_q_gain[l].reshape(1, HEAD_DIM), a_gk=a_k_gain[l].reshape(1, HEAD_DIM), a_rel=a_rel_bias[l],
        s5=_s5_params(s5_lam_re[l], s5_lam_im[l], s5_log_dt[l], s5_b_re[l], s5_b_im[l], s5_c_re[l], s5_c_im[l]),
        s5_d=s5_d[l].reshape(1, S5_WIDTH), s5_wg=s5_w_glu[l].astype(MXU_DTYPE), s5_bg=s5_b_glu[l].reshape(1, S5_WIDTH),
        c_gq=c_q_gain[l].reshape(1, HEAD_DIM), c_gk=c_k_gain[l].reshape(1, HEAD_DIM),
        w_out=w_out[l].astype(MXU_DTYPE), norm2=norm2_g[l].reshape(1, D_MODEL),
        wg=ffn_w_gate[l].astype(MXU_DTYPE), wu=ffn_w_up[l].astype(MXU_DTYPE), cw=ffn_conv_w[l],
        cb=ffn_conv_b[l].reshape(1, FFN_DIM), wd=ffn_w_down[l].astype(MXU_DTYPE),
    )


def _prompt_layer(x, p, dsa_bias, *, tm, tf, s5_bt, s5_tc):
    b, seq, _ = x.shape
    x2d = x.reshape(b * seq, D_MODEL)
    z3 = _in_proj(x2d, p["norm1"], p["w_in"], tm).reshape(b, seq, Z_WIDTH)
    oa, a_k, a_v = _attn_a(z3, _attn_a_bias(p["a_rel"]), p["a_gq"], p["a_gk"])
    zero_h = jnp.zeros((b, S5_FLAT), F32)
    ob, h_re, h_im = _s5(z3, zero_h, zero_h, p["s5"], p["s5_d"], p["s5_wg"], p["s5_bg"],
                         bt=s5_bt, tc=s5_tc, exact_in=False)
    oc, c_k, c_v, c_ki = _dsa(z3, dsa_bias, p["c_gq"], p["c_gk"])
    y, conv = _out_ffn(x2d, oa.reshape(b * seq, A_WIDTH), ob.reshape(b * seq, S5_WIDTH),
                       oc.reshape(b * seq, C_WIDTH), p["w_out"], p["norm2"], p["wg"], p["wu"], p["cw"], p["cb"],
                       p["wd"], seq=seq, tm=tm, tf=tf)
    keep = a_k.shape[1]
    states = (a_k.reshape(b, keep, A_HEADS, HEAD_DIM), a_v.reshape(b, keep, A_HEADS, HEAD_DIM),
              h_re.reshape(b, S5_GROUPS, S5_STATE), h_im.reshape(b, S5_GROUPS, S5_STATE), c_k, c_v, c_ki, conv)
    return y.reshape(b, seq, D_MODEL), states


def _attn_a_step_kernel(q_ref, k_ref, v_ref, ck_ref, cv_ref, bc_ref, bn_ref, gq_ref, gk_ref, o_ref, ak_ref, av_ref):
    v = v_ref[0]
    av_ref[0] = v
    for h in range(A_HEADS):
        sl = slice(h * HEAD_DIM, (h + 1) * HEAD_DIM)
        qh = (_rms(q_ref[0, :, sl], gq_ref[...]) * HEAD_DIM ** -0.5).astype(MXU_DTYPE)
        khn = _rms(k_ref[0, :, sl], gk_ref[...])
        ak_ref[0, :, sl] = khn
        s_c = _nt_dot(qh, ck_ref[0, :, sl].astype(MXU_DTYPE)) + bc_ref[h]
        s_n = _nt_dot(qh, khn.astype(MXU_DTYPE)) + bn_ref[h]
        m = jnp.maximum(jnp.max(s_c, axis=-1, keepdims=True), jnp.max(s_n, axis=-1, keepdims=True))
        p_c = jnp.exp(s_c - m)
        p_n = jnp.exp(s_n - m)
        denom = jnp.sum(p_c, axis=-1, keepdims=True) + jnp.sum(p_n, axis=-1, keepdims=True)
        oh = (_dot(p_c.astype(MXU_DTYPE), cv_ref[0, :, sl].astype(MXU_DTYPE))
              + _dot(p_n.astype(MXU_DTYPE), v[:, sl].astype(MXU_DTYPE)))
        o_ref[0, :, sl] = oh / denom


def _attn_a_step_bias(rel_table, past, t_new):
    t = jnp.arange(t_new)[:, None]
    rel = jnp.concatenate([jnp.arange(past) - past, jnp.arange(t_new)])[None, :] - t
    bias = _rows_at(rel_table.T, jnp.clip(rel, -A_MAX_REL, A_MAX_REL) + A_MAX_REL).transpose(2, 0, 1)
    return bias[:, :, :past], bias[:, :, past:]


def _attn_a_step(z3, cache_k, cache_v, layer, bias_c, bias_n, gq, gk):
    b, t_new, _ = z3.shape
    past = cache_k.shape[2]
    new = lambda col: pl.BlockSpec((1, t_new, A_WIDTH), lambda i: (i, 0, col // A_WIDTH))
    const = lambda shape: pl.BlockSpec(shape, lambda i: (0,) * len(shape))
    return pl.pallas_call(
        _attn_a_step_kernel,
        grid=(b,),
        in_specs=[
            new(COL_AQ), new(COL_AK), new(COL_AV),
            pl.BlockSpec((None, 1, past, A_WIDTH), lambda i: (layer, i, 0, 0)),
            pl.BlockSpec((None, 1, past, A_WIDTH), lambda i: (layer, i, 0, 0)),
            const((A_HEADS, t_new, past)), const((A_HEADS, t_new, t_new)),
            const((1, HEAD_DIM)), const((1, HEAD_DIM)),
        ],
        out_specs=[pl.BlockSpec((1, t_new, A_WIDTH), lambda i: (i, 0, 0))] * 3,
        out_shape=[jax.ShapeDtypeStruct((b, t_new, A_WIDTH), F32)] * 3,
        compiler_params=pltpu.CompilerParams(dimension_semantics=("parallel",), vmem_limit_bytes=VMEM_LIMIT),
        name="attn_a_step",
    )(z3, z3, z3, cache_k, cache_v, bias_c, bias_n, gq, gk)


def _dsa_step_kernel(cq_ref, ckv_ref, qi_ref, kiw_ref, ck_ref, cv_ref, cki_ref, gq_ref, gk_ref, bias_ref,
                     o_ref, ck_out, cv_out, cki_out, *, past, t_new, topk):
    n_keys = past + t_new
    kn_new = _rms(ckv_ref[0, :, 0:HEAD_DIM], gk_ref[...])
    cv_new = ckv_ref[0, :, HEAD_DIM:2 * HEAD_DIM]
    ki_new = kiw_ref[0, :, 0:IDX_DIM]
    ck_out[0] = kn_new
    cv_out[0] = cv_new
    cki_out[0] = ki_new
    k_all = jnp.concatenate([ck_ref[0], kn_new], axis=0).astype(MXU_DTYPE)
    v_all = jnp.concatenate([cv_ref[0], cv_new], axis=0).astype(MXU_DTYPE)
    ki_all = jnp.concatenate([cki_ref[0], ki_new], axis=0).astype(MXU_DTYPE)
    v_t = _nt_dot(_eye(HEAD_DIM, MXU_DTYPE), v_all).astype(MXU_DTYPE)

    def spread(groups, dtype):
        r = lax.broadcasted_iota(jnp.int32, (t_new, groups * t_new), 0)
        c = lax.broadcasted_iota(jnp.int32, (t_new, groups * t_new), 1)
        return (c % t_new == r).astype(dtype)

    qi = qi_ref[0].astype(MXU_DTYPE)
    qi_all = jnp.concatenate([qi[:, h * IDX_DIM:(h + 1) * IDX_DIM] for h in range(IDX_HEADS)], axis=0)
    kiw_pad = jnp.concatenate([kiw_ref[0], jnp.zeros((128 - t_new, 128), F32)], axis=0)
    w_t = kiw_pad.T[:, 0:t_new]
    w_flat = jnp.concatenate([w_t[IDX_DIM + h:IDX_DIM + h + 1, :] for h in range(IDX_HEADS)], axis=1) * INDEX_SCALE
    weighted = w_flat * jnp.maximum(_nt_dot(ki_all, qi_all), 0.0)
    score = lax.dot_general(weighted, spread(IDX_HEADS, F32), (((1,), (1,)), ((), ())),
                            preferred_element_type=F32, precision=lax.Precision.HIGHEST)
    pos = lax.broadcasted_iota(jnp.int32, (n_keys, 1), 0)
    q_chunk = (past + lax.broadcasted_iota(jnp.int32, (1, t_new), 1)) // CHUNK
    keys = jnp.where((pos // CHUNK) <= q_chunk, _sortable_key(score), jnp.int32(INT_MIN))

    def count(pred):
        return jnp.sum(jnp.where(pred(keys, pos), 1, 0), axis=0, keepdims=True)

    def kth_body(it, res):
        cand = res | lax.shift_left(jnp.int32(1), jnp.int32(31) - it)
        thr_c = cand ^ jnp.int32(INT_MIN)
        return jnp.where(count(lambda k, p: k >= thr_c) >= topk, cand, res)

    thr = lax.fori_loop(0, 32, kth_body, jnp.zeros((1, t_new), jnp.int32)) ^ jnp.int32(INT_MIN)
    real_thr = thr != jnp.int32(INT_MIN)
    pos_bits = (n_keys - 1).bit_length()

    def _tie_search():
        need = topk - count(lambda k, p: k > thr)

        def tie_body(it, v):
            cand = v | lax.shift_left(jnp.int32(1), jnp.int32(pos_bits - 1) - it)
            return jnp.where(count(lambda k, p: (k == thr) & (p < cand)) < need, cand, v)

        return lax.fori_loop(0, pos_bits, tie_body, jnp.zeros((1, t_new), jnp.int32))

    has_ties = jnp.max(jnp.where(real_thr & (count(lambda k, p: k >= thr) > topk), 1, 0)) > 0
    tie_last = lax.cond(has_ties, _tie_search, lambda: jnp.full((1, t_new), 2 ** pos_bits - 1, jnp.int32))
    tie_last = jnp.where(real_thr, tie_last, -1)
    dropped = jnp.where((keys > thr) | ((keys == thr) & (pos <= tie_last)), 0.0, 1.0).astype(MXU_DTYPE)
    negm = _dot(dropped, spread(C_HEADS, MXU_DTYPE)) * NEG_INF

    cq = cq_ref[0]
    q_all = jnp.concatenate(
        [(_rms(cq[:, h * HEAD_DIM:(h + 1) * HEAD_DIM], gq_ref[...]) * HEAD_DIM ** -0.5).astype(MXU_DTYPE)
         for h in range(C_HEADS)], axis=0)
    s = _nt_dot(k_all, q_all) + bias_ref[...] + negm
    p = jnp.exp(s - jnp.max(s, axis=0, keepdims=True))
    out_t = _dot(v_t, p.astype(MXU_DTYPE)) / jnp.sum(p, axis=0, keepdims=True)
    hq = C_HEADS * t_new
    out_sq = jnp.concatenate([jnp.concatenate([out_t, jnp.zeros((HEAD_DIM, 128 - hq), F32)], axis=1),
                              jnp.zeros((128 - HEAD_DIM, 128), F32)], axis=0).T
    o_ref[0] = jnp.concatenate([out_sq[h * t_new:(h + 1) * t_new, 0:HEAD_DIM] for h in range(C_HEADS)], axis=1)


def _dsa_step_bias(t5_table, past, t_new):
    s = jnp.arange(past + t_new)[:, None]
    t = jnp.arange(t_new)[None, :]
    return _rows_at(t5_table, _t5_bucket(s - (past + t))).transpose(0, 2, 1).reshape(past + t_new, C_HEADS * t_new)


def _dsa_step(z3, cache_k, cache_v, cache_ki, layer, bias, gq, gk):
    b, t_new, _ = z3.shape
    past = cache_k.shape[2]
    topk = min(TOPK_MAX, (past + t_new) // 4)
    kern = functools.partial(_dsa_step_kernel, past=past, t_new=t_new, topk=topk)
    new = lambda w, col: pl.BlockSpec((1, t_new, w), lambda i: (i, 0, col // w))
    per_seq = lambda n, w: pl.BlockSpec((1, n, w), lambda i: (i, 0, 0))
    cached = lambda w: pl.BlockSpec((None, 1, past, w), lambda i: (layer, i, 0, 0))
    const = lambda shape: pl.BlockSpec(shape, lambda i: (0,) * len(shape))
    return pl.pallas_call(
        kern,
        grid=(b,),
        in_specs=[
            new(C_WIDTH, COL_CQ), new(128, COL_CKV), new(256, COL_QI), new(128, COL_KIW),
            cached(HEAD_DIM), cached(HEAD_DIM), cached(IDX_DIM),
            const((1, HEAD_DIM)), const((1, HEAD_DIM)), const((past + t_new, C_HEADS * t_new)),
        ],
        out_specs=[per_seq(t_new, C_WIDTH), per_seq(t_new, HEAD_DIM), per_seq(t_new, HEAD_DIM),
                   per_seq(t_new, IDX_DIM)],
        out_shape=[
            jax.ShapeDtypeStruct((b, t_new, C_WIDTH), F32),
            jax.ShapeDtypeStruct((b, t_new, HEAD_DIM), F32),
            jax.ShapeDtypeStruct((b, t_new, HEAD_DIM), F32),
            jax.ShapeDtypeStruct((b, t_new, IDX_DIM), F32),
        ],
        compiler_params=pltpu.CompilerParams(dimension_semantics=("parallel",), vmem_limit_bytes=VMEM_LIMIT),
        name="dsa_step",
    )(z3, z3, z3, z3, cache_k, cache_v, cache_ki, gq, gk, bias)


def _out_ffn_step_kernel(x_ref, oa_ref, ob_ref, oc_ref, wo_ref, g2_ref, wg_ref, wu_ref, cw_ref, cb_ref, wd_ref,
                         e1_ref, e2_ref, y_ref, gate_ref, *, t_new, tf):
    tm = x_ref.shape[0]
    x1 = _mix_residual(x_ref, oa_ref, ob_ref, oc_ref, wo_ref)
    xn = _rms(x1, g2_ref[...]).astype(MXU_DTYPE)
    t = lax.broadcasted_iota(jnp.int32, (tm, 1), 0) % t_new
    acc = x1
    for f in range(FFN_DIM // tf):
        sl = slice(f * tf, (f + 1) * tf)
        g = _dot(xn, wg_ref[:, sl])
        gate_ref[:, sl] = g
        g1 = jnp.where(t == 0, e1_ref[:, sl], pltpu.roll(g, 1, 0))
        g2 = jnp.where(t <= 1, e2_ref[:, sl], pltpu.roll(g, 2, 0))
        acc = acc + _ffn_chunk(xn, g, g1, g2, sl, wu_ref, cw_ref, cb_ref, wd_ref)
    y_ref[...] = acc


def _out_ffn_step(x2d, oa, ob, oc, wo, g2, wg, wu, cw, cb, wd, conv_prev, *, t_new, tf):
    m = x2d.shape[0]
    b = m // t_new
    e1 = jnp.zeros((b, t_new, FFN_DIM), F32).at[:, 0].set(conv_prev[:, 1]).reshape(m, FFN_DIM)
    e2 = jnp.zeros((b, t_new, FFN_DIM), F32).at[:, 0].set(conv_prev[:, 0]).at[:, 1].set(conv_prev[:, 1])
    e2 = e2.reshape(m, FFN_DIM)
    kern = functools.partial(_out_ffn_step_kernel, t_new=t_new, tf=tf)
    full = lambda shape: pl.BlockSpec(shape, lambda i: (0,) * len(shape))
    y, gate = pl.pallas_call(
        kern,
        grid=(1,),
        in_specs=[
            full((m, D_MODEL)), full((m, A_WIDTH)), full((m, S5_WIDTH)), full((m, C_WIDTH)),
            full((D_MODEL, D_MODEL)), full((1, D_MODEL)),
            full((D_MODEL, FFN_DIM)), full((D_MODEL, FFN_DIM)),
            full((CONV_W, FFN_DIM)), full((1, FFN_DIM)), full((FFN_DIM, D_MODEL)),
            full((m, FFN_DIM)), full((m, FFN_DIM)),
        ],
        out_specs=[full((m, D_MODEL)), full((m, FFN_DIM))],
        out_shape=[jax.ShapeDtypeStruct((m, D_MODEL), F32), jax.ShapeDtypeStruct((m, FFN_DIM), F32)],
        compiler_params=pltpu.CompilerParams(dimension_semantics=("arbitrary",), vmem_limit_bytes=VMEM_LIMIT),
        name="out_ffn_step",
    )(x2d, oa, ob, oc, wo, g2, wg, wu, cw, cb, wd, e1, e2)
    return y, gate.reshape(b, t_new, FFN_DIM)[:, t_new - (CONV_W - 1):]


def _sample_layer(x, p, dsa_bias, layer, ca_k, ca_v, h_re0, h_im0, cc_k, cc_v, cc_ki, conv_prev, *, tf):
    b, t_new, _ = x.shape
    m = b * t_new
    x2d = x.reshape(m, D_MODEL)
    z3 = _in_proj(x2d, p["norm1"], p["w_in"], m).reshape(b, t_new, Z_WIDTH)
    depth, _, a_past = ca_k.shape[:3]
    bias_c, bias_n = _attn_a_step_bias(p["a_rel"], a_past, t_new)
    oa, a_k, a_v = _attn_a_step(z3, ca_k.reshape(depth, b, a_past, A_WIDTH), ca_v.reshape(depth, b, a_past, A_WIDTH),
                                layer, bias_c, bias_n, p["a_gq"], p["a_gk"])
    ob, h_re, h_im = _s5(z3, h_re0.reshape(b, S5_FLAT), h_im0.reshape(b, S5_FLAT), p["s5"], p["s5_d"], p["s5_wg"],
                         p["s5_bg"], bt=8, tc=t_new, exact_in=True)
    oc, c_k, c_v, c_ki = _dsa_step(z3, cc_k, cc_v, cc_ki, layer, dsa_bias, p["c_gq"], p["c_gk"])
    y, conv = _out_ffn_step(x2d, oa.reshape(m, A_WIDTH), ob.reshape(m, S5_WIDTH), oc.reshape(m, C_WIDTH),
                            p["w_out"], p["norm2"], p["wg"], p["wu"], p["cw"], p["cb"], p["wd"], conv_prev,
                            t_new=t_new, tf=tf)
    states = (a_k.reshape(b, t_new, A_HEADS, HEAD_DIM), a_v.reshape(b, t_new, A_HEADS, HEAD_DIM),
              h_re.reshape(b, S5_GROUPS, S5_STATE), h_im.reshape(b, S5_GROUPS, S5_STATE), c_k, c_v, c_ki, conv)
    return y.reshape(b, t_new, D_MODEL), states


def kernel(x_prompt, x_sample, cache_a_k, cache_a_v, state_s5_re, state_s5_im, cache_c_k, cache_c_v, cache_c_idx_k,
           state_ffn_conv, t5_bias, norm1_g, w_in, a_q_gain, a_k_gain, a_rel_bias, s5_lam_re, s5_lam_im, s5_log_dt,
           s5_b_re, s5_b_im, s5_c_re, s5_c_im, s5_d, s5_w_glu, s5_b_glu, c_q_gain, c_k_gain, w_out, norm2_g,
           ffn_w_gate, ffn_w_up, ffn_conv_w, ffn_conv_b, ffn_w_down):
    depth = w_in.shape[0]
    dsa_bias = _dsa_bias_tiles(t5_bias)
    dsa_step_bias = _dsa_step_bias(t5_bias, cache_c_k.shape[2], x_sample.shape[1])
    yp, ys = x_prompt, x_sample
    prompt_states, sample_states = [], []
    for l in range(depth):
        p = _layer_params(l, t5_bias, norm1_g, w_in, a_q_gain, a_k_gain, a_rel_bias, s5_lam_re, s5_lam_im, s5_log_dt,
                          s5_b_re, s5_b_im, s5_c_re, s5_c_im, s5_d, s5_w_glu, s5_b_glu, c_q_gain, c_k_gain, w_out,
                          norm2_g, ffn_w_gate, ffn_w_up, ffn_conv_w, ffn_conv_b, ffn_w_down)
        yp, st_p = _prompt_layer(yp, p, dsa_bias, tm=512, tf=FFN_DIM, s5_bt=8, s5_tc=256)
        ys, st_s = _sample_layer(ys, p, dsa_step_bias, l, cache_a_k, cache_a_v, state_s5_re[l], state_s5_im[l],
                                 cache_c_k, cache_c_v, cache_c_idx_k, state_ffn_conv[l], tf=FFN_DIM)
        prompt_states.append(st_p)
        sample_states.append(st_s)
    (a_k_p, a_v_p, s5_re_p, s5_im_p, c_k_p, c_v_p, c_ki_p, conv_p) = [jnp.stack(z) for z in zip(*prompt_states)]
    (a_k_s, a_v_s, s5_re_s, s5_im_s, c_k_s, c_v_s, c_ki_s, conv_s) = [jnp.stack(z) for z in zip(*sample_states)]
    return (yp, ys, a_k_p, a_v_p, a_k_s, a_v_s, s5_re_p, s5_im_p, s5_re_s, s5_im_s,
            c_k_p, c_v_p, c_ki_p, c_k_s, c_v_s, c_ki_s, conv_p, conv_s)
```
